```python
import jax, jax.numpy as jnp
from jax import lax
import numpy as np

D_MODEL = 1024
BATCH = 8
SEQ = 2048
DEPTH = 4
DEC_BATCH = 128
DEC_SEQ = 8
PAST_LEN = 16384
PAGE_SIZE = 128

D_MIX = D_MODEL
A_HEADS = 4
A_WIDTH = D_MIX // 4
A_HEAD_DIM = A_WIDTH // A_HEADS
CHUNK = 128
B_WIDTH = D_MIX // 4
CONV_B = 3
C_WIDTH = D_MIX // 2
SSM_HEAD_DIM = 64
SSM_HEADS = C_WIDTH // SSM_HEAD_DIM
SSM_GROUPS = 2
SSM_HPG = SSM_HEADS // SSM_GROUPS
D_STATE = 128
CONV_C = 4
SSD_CHUNK = 128
SSM_CONV_DIM = C_WIDTH + 2 * SSM_GROUPS * D_STATE
D_FF = 4 * D_MODEL
EPS = 1e-5
D_IN = 2 * A_WIDTH + 3 * B_WIDTH + C_WIDTH + SSM_CONV_DIM + SSM_HEADS

kernel_name = "hybrid_chunkmlp_shortconv_ssd_decoder_step"


def rmsnorm(x, g):
    xf = x.astype(jnp.float32)
    y = xf * lax.rsqrt(jnp.mean(xf * xf, axis=-1, keepdims=True) + EPS)
    return (y * g.astype(jnp.float32)).astype(x.dtype)


def causal_dwconv(inp, prev, w, b=None):
    K = w.shape[0]
    T = inp.shape[1]
    full = jnp.concatenate([prev.astype(inp.dtype), inp], axis=1)
    out = full[:, 0:T] * w[0]
    for k in range(1, K):
        out = out + full[:, k:k + T] * w[k]
    if b is not None:
        out = out + b
    return out, full[:, T:]


def chunk_mlp(u, v, w_s, b_s):
    bsz, T, _ = u.shape
    n_c = -(-T // CHUNK)
    pad = n_c * CHUNK - T
    vp = jnp.pad(v, ((0, 0), (0, pad), (0, 0))).reshape(bsz, n_c, CHUNK, A_HEADS, A_HEAD_DIM)
    mask = jnp.tril(jnp.ones((CHUNK, CHUNK), dtype=bool))
    wm = jnp.where(mask[None], w_s, jnp.zeros((), w_s.dtype))
    s = jnp.einsum('hts,bcshe->bcthe', wm, vp) + jnp.transpose(b_s)[None, None, :, :, None]
    s = s.reshape(bsz, n_c * CHUNK, A_WIDTH)[:, :T]
    return u * s


def ssd(x, dt, A, Bm, Cm, h0):
    bsz, T = x.shape[0], x.shape[1]
    L = min(SSD_CHUNK, T)
    n_c = -(-T // L)
    pad = n_c * L - T
    padt = lambda a: jnp.pad(a, [(0, 0), (0, pad)] + [(0, 0)] * (a.ndim - 2))
    xs = padt(x).reshape(bsz, n_c, L, SSM_GROUPS, SSM_HPG, SSM_HEAD_DIM)
    dts = padt(dt).reshape(bsz, n_c, L, SSM_GROUPS, SSM_HPG)
    Bs = padt(Bm).reshape(bsz, n_c, L, SSM_GROUPS, D_STATE)
    Cs = padt(Cm).reshape(bsz, n_c, L, SSM_GROUPS, D_STATE)
    a = dts * A.reshape(SSM_GROUPS, SSM_HPG)
    cum = jnp.cumsum(a, axis=2)
    tril = jnp.tril(jnp.ones((L, L), dtype=bool))[None, None, :, :, None, None]
    seg = cum[:, :, :, None] - cum[:, :, None, :]
    decay = jnp.exp(jnp.where(tril, seg, -jnp.inf))
    cb = jnp.einsum('bctgn,bcsgn->bctsg', Cs, Bs)
    wts = cb[..., None] * decay * dts[:, :, None]
    y_diag = jnp.einsum('bctsgk,bcsgkp->bctgkp', wts, xs)
    decay_end = jnp.exp(cum[:, :, -1:] - cum)
    states = jnp.einsum('bcsgn,bcsgk,bcsgkp->bcgkpn', Bs, decay_end * dts, xs)
    chunk_decay = jnp.exp(cum[:, :, -1])

    def step(h, inp):
        st, dec = inp
        return dec[..., None, None] * h + st, h

    h_init = h0.reshape(bsz, SSM_GROUPS, SSM_HPG, SSM_HEAD_DIM, D_STATE)
    h_final, h_starts = lax.scan(step, h_init,
                                 (jnp.moveaxis(states, 1, 0), jnp.moveaxis(chunk_decay, 1, 0)))
    h_starts = jnp.moveaxis(h_starts, 0, 1)
    y_off = jnp.einsum('bctgn,bcgkpn->bctgkp', Cs, h_starts) * jnp.exp(cum)[..., None]
    y = (y_diag + y_off).reshape(bsz, n_c * L, SSM_HEADS, SSM_HEAD_DIM)[:, :T]
    return y, h_final.reshape(bsz, SSM_HEADS, SSM_HEAD_DIM, D_STATE)


def layer(x, conv_prev, sconv_prev, ssm_prev, g1, w_in, w_s, b_s, conv_w, sconv_w, sconv_b,
          dt_bias, a_log, d_skip, ssm_norm, w_out, g2, w_ff1, w_ff2):
    bsz, T, _ = x.shape
    h = rmsnorm(x, g1)
    proj = h @ w_in
    cuts = np.cumsum([A_WIDTH, A_WIDTH, B_WIDTH, B_WIDTH, B_WIDTH, C_WIDTH, SSM_CONV_DIM])
    u, v, bgate, cgate, hb, z, xbc, dt_raw = jnp.split(proj, cuts, axis=-1)

    u = jax.nn.gelu(u, approximate=False)
    v = jax.nn.gelu(v, approximate=False)
    ya = chunk_mlp(u, v, w_s, b_s)
    v_rows = v[:, ((T - 1) // CHUNK) * CHUNK:]

    conv_out, conv_new = causal_dwconv(cgate * hb, conv_prev, conv_w)
    yb = bgate * conv_out

    xbc_c, sconv_new = causal_dwconv(xbc, sconv_prev, sconv_w, sconv_b)
    xbc_c = jax.nn.silu(xbc_c)
    xs, Bm, Cm = jnp.split(xbc_c, [C_WIDTH, C_WIDTH + SSM_GROUPS * D_STATE], axis=-1)
    xs = xs.reshape(bsz, T, SSM_HEADS, SSM_HEAD_DIM).astype(jnp.float32)
    Bm = Bm.reshape(bsz, T, SSM_GROUPS, D_STATE).astype(jnp.float32)
    Cm = Cm.reshape(bsz, T, SSM_GROUPS, D_STATE).astype(jnp.float32)
    dt = jax.nn.softplus(dt_raw.astype(jnp.float32) + dt_bias.astype(jnp.float32))
    A = -jnp.exp(a_log.astype(jnp.float32))
    y, ssm_new = ssd(xs, dt, A, Bm, Cm, ssm_prev.astype(jnp.float32))
    y = y + d_skip.astype(jnp.float32)[:, None] * xs
    y = y.reshape(bsz, T, C_WIDTH).astype(x.dtype)
    yc = rmsnorm(y * jax.nn.silu(z), ssm_norm)

    x = x + jnp.concatenate([ya, yb, yc], axis=-1) @ w_out
    f = jnp.square(jax.nn.relu(rmsnorm(x, g2) @ w_ff1))
    x = x + f @ w_ff2
    return x, v_rows, conv_new, sconv_new, ssm_new.astype(x.dtype)


def trunk(x, conv0, sconv0, ssm0, norm1, w_in, w_s, b_s, conv_w, ssm_conv_w, ssm_conv_b,
          dt_bias, a_log, d_skip, ssm_norm, w_out, norm2, w_ff1, w_ff2, final_norm):
    vs, cs, scs, ss = [], [], [], []
    for l in range(DEPTH):
        x, v_rows, c_new, sc_new, s_new = layer(
            x, conv0[l], sconv0[l], ssm0[l], norm1[l], w_in[l], w_s[l], b_s[l], conv_w[l],
            ssm_conv_w[l], ssm_conv_b[l], dt_bias[l], a_log[l], d_skip[l], ssm_norm[l],
            w_out[l], norm2[l], w_ff1[l], w_ff2[l])
        vs.append(v_rows); cs.append(c_new); scs.append(sc_new); ss.append(s_new)
    y = rmsnorm(x, final_norm)
    return y, jnp.stack(vs), jnp.stack(cs), jnp.stack(scs), jnp.stack(ss)


def setup_inputs(seed: int = 0) -> dict:
    key = jax.random.key(seed)
    ks = jax.random.split(key, 24)
    nrm = lambda k, shape, s: jax.random.normal(k, shape, jnp.float32) * s
    dt0 = jnp.exp(jax.random.uniform(ks[10], (DEPTH, SSM_HEADS), jnp.float32)
                  * (np.log(0.1) - np.log(0.001)) + np.log(0.001))
    return {
        "x_prompt": nrm(ks[0], (BATCH, SEQ, D_MODEL), 1.0),
        "x_sample": nrm(ks[1], (DEC_BATCH, DEC_SEQ, D_MODEL), 1.0),
        "state_conv": nrm(ks[2], (DEPTH, DEC_BATCH, CONV_B - 1, B_WIDTH), 0.5),
        "state_ssm_conv": nrm(ks[3], (DEPTH, DEC_BATCH, CONV_C - 1, SSM_CONV_DIM), 1.0),
        "state_ssm": nrm(ks[4], (DEPTH, DEC_BATCH, SSM_HEADS, SSM_HEAD_DIM, D_STATE), 0.1),
        "norm1": 1.0 + nrm(ks[5], (DEPTH, D_MODEL), 0.02),
        "w_in": nrm(ks[6], (DEPTH, D_MODEL, D_IN), D_MODEL ** -0.5),
        "w_s": nrm(ks[7], (DEPTH, A_HEADS, CHUNK, CHUNK), CHUNK ** -0.5),
        "b_s": nrm(ks[8], (DEPTH, A_HEADS, CHUNK), 0.1),
        "conv_w": nrm(ks[9], (DEPTH, CONV_B, B_WIDTH), CONV_B ** -0.5),
        "ssm_conv_w": nrm(ks[11], (DEPTH, CONV_C, SSM_CONV_DIM), CONV_C ** -0.5),
        "ssm_conv_b": nrm(ks[12], (DEPTH, SSM_CONV_DIM), 0.02),
        "dt_bias": dt0 + jnp.log(-jnp.expm1(-dt0)),
        "a_log": jnp.log(jax.random.uniform(ks[13], (DEPTH, SSM_HEADS), jnp.float32, 1.0, 16.0)),
        "d_skip": 1.0 + nrm(ks[14], (DEPTH, SSM_HEADS), 0.02),
        "ssm_norm": 1.0 + nrm(ks[15], (DEPTH, C_WIDTH), 0.02),
        "w_out": nrm(ks[16], (DEPTH, D_MIX, D_MODEL), D_MIX ** -0.5),
        "norm2": 1.0 + nrm(ks[17], (DEPTH, D_MODEL), 0.02),
        "w_ff1": nrm(ks[18], (DEPTH, D_MODEL, D_FF), D_MODEL ** -0.5),
        "w_ff2": nrm(ks[19], (DEPTH, D_FF, D_MODEL), D_FF ** -0.5),
        "final_norm": 1.0 + nrm(ks[20], (D_MODEL,), 0.02),
    }


def reference(x_prompt, x_sample, state_conv, state_ssm_conv, state_ssm, norm1, w_in, w_s, b_s,
              conv_w, ssm_conv_w, ssm_conv_b, dt_bias, a_log, d_skip, ssm_norm, w_out, norm2,
              w_ff1, w_ff2, final_norm):
    bp = x_prompt.shape[0]
    dtp = x_prompt.dtype
    conv0 = jnp.zeros((DEPTH, bp, CONV_B - 1, B_WIDTH), dtp)
    sconv0 = jnp.zeros((DEPTH, bp, CONV_C - 1, SSM_CONV_DIM), dtp)
    ssm0 = jnp.zeros((DEPTH, bp, SSM_HEADS, SSM_HEAD_DIM, D_STATE), dtp)
    y_prompt, chunk_v_prompt, conv_prompt, ssm_conv_prompt, ssm_prompt = trunk(
        x_prompt, conv0, sconv0, ssm0, norm1, w_in, w_s, b_s, conv_w, ssm_conv_w, ssm_conv_b,
        dt_bias, a_log, d_skip, ssm_norm, w_out, norm2, w_ff1, w_ff2, final_norm)
    y_sample, chunk_v_sample, conv_sample, ssm_conv_sample, ssm_sample = trunk(
        x_sample, state_conv, state_ssm_conv, state_ssm, norm1, w_in, w_s, b_s, conv_w,
        ssm_conv_w, ssm_conv_b, dt_bias, a_log, d_skip, ssm_norm, w_out, norm2, w_ff1, w_ff2,
        final_norm)
    return (y_prompt, y_sample, chunk_v_prompt, conv_prompt, ssm_conv_prompt, ssm_prompt,
            chunk_v_sample, conv_sample, ssm_conv_sample, ssm_sample)
```

```python
import functools

import jax
import jax.numpy as jnp
from jax import lax
from jax.experimental import pallas as pl
from jax.experimental.pallas import tpu as pltpu

D_MODEL = 1024
DEPTH = 4
A_WIDTH = 256
A_HEADS = 4
B_WIDTH = 256
C_WIDTH = 512
CONV_B = 3
CONV_C = 4
SSM_HEADS = 8
SSM_HEAD_DIM = 64
SSM_GROUPS = 2
D_STATE = 128
SSM_CONV_DIM = 1024
CHUNK = 128
D_FF = 4096
EPS = 1e-5
DEC_SEQ = 8

COL_U, COL_V, COL_BG, COL_CG, COL_HB, COL_Z, COL_XBC, COL_DT = 0, 256, 512, 768, 1024, 1280, 1792, 2816
D_IN = 2824
LANES = 128
SUBLANES = 8
D_IN_PAD = 2944
GROUP_W = C_WIDTH // SSM_GROUPS

PROMPT_ROWS = 256
SAMPLE_SEQS = 16
FF_ROWS = 512
FF_COLS = 1024
VMEM_LIMIT = 56 * 1024 * 1024

F32 = jnp.float32
BF16 = jnp.bfloat16


def _bf(x):
    return x.astype(BF16)


def _dot(a, b):
    return jnp.dot(a, b, preferred_element_type=F32)


def _dot_nt(a, b):
    return lax.dot_general(a, b, (((1,), (1,)), ((), ())), preferred_element_type=F32)


def _split3(a):
    a1 = _bf(a)
    r1 = a - a1.astype(F32)
    a2 = _bf(r1)
    r2 = r1 - a2.astype(F32)
    return a1, a2, _bf(r2)


def _sel_left(m01, a):
    a1, a2, a3 = _split3(a)
    return _dot(m01, a1) + _dot(m01, a2) + _dot(m01, a3)


def _sel_right(a, m01):
    a1, a2, a3 = _split3(a)
    return _dot(a1, m01) + _dot(a2, m01) + _dot(a3, m01)


def _rmsnorm(x, g):
    ms = jnp.mean(x * x, axis=-1, keepdims=True)
    return (x * lax.rsqrt(ms + EPS)) * g


def _gelu(x):
    return 0.5 * x * (1.0 + lax.erf(x * (0.5 ** 0.5)))


def _silu(x):
    return x * jax.nn.sigmoid(x)


def _softplus(x):
    return jnp.maximum(x, 0.0) + jnp.log1p(jnp.exp(-jnp.abs(x)))


def _block_diag_pair(t):
    lane = lax.broadcasted_iota(jnp.int32, t.shape, 1)
    lo = jnp.where(lane < SSM_HEAD_DIM, t, 0.0)
    hi = jnp.where(lane >= SSM_HEAD_DIM, t, 0.0)
    return _bf(jnp.concatenate([lo, hi], axis=0))


def _mix_kernel(*refs, prompt, rows):
    if prompt:
        (x_ref, g1_ref, win_ref, wa_ref, ba_ref, cw_ref, scw_ref, scb_ref, dtb_ref, alog_ref,
         dsk_ref, snorm_ref, wout_ref,
         xo_ref, vrow_ref, convo_ref, sconvo_ref, ssmo_ref,
         proj_ref, mix_ref, cbufb_ref, cbufc_ref, st_ref) = refs
    else:
        (x_ref, g1_ref, win_ref, wa_ref, ba_ref, cw_ref, scw_ref, scb_ref, dtb_ref, alog_ref,
         dsk_ref, snorm_ref, wout_ref, conv0_ref, sconv0_ref, ssm0_ref,
         xo_ref, vrow_ref, convo_ref, sconvo_ref, ssmo_ref,
         proj_ref, mix_ref, cbufb_ref, cbufc_ref) = refs
    n_sub = rows // CHUNK
    seqs_per_sub = CHUNK // DEC_SEQ
    j = pl.program_id(1) if prompt else None

    x = x_ref[...]
    h = _bf(_rmsnorm(x, g1_ref[...]))
    proj_ref[...] = _dot(h, win_ref[...])

    cg_in = proj_ref[:, COL_CG:COL_CG + B_WIDTH] * proj_ref[:, COL_HB:COL_HB + B_WIDTH]
    xbc_in = proj_ref[:, COL_XBC:COL_XBC + SSM_CONV_DIM]
    if prompt:
        @pl.when(j == 0)
        def _():
            cbufb_ref[0:SUBLANES, :] = jnp.zeros((SUBLANES, B_WIDTH), F32)
            cbufc_ref[0:SUBLANES, :] = jnp.zeros((SUBLANES, SSM_CONV_DIM), F32)
            st_ref[...] = jnp.zeros(st_ref.shape, F32)
        cbufb_ref[SUBLANES:SUBLANES + rows, :] = cg_in
        cbufc_ref[SUBLANES:SUBLANES + rows, :] = xbc_in
    else:
        nseq = rows // DEC_SEQ
        cbufb_ref[:, SUBLANES - (CONV_B - 1):SUBLANES, :] = conv0_ref[...]
        cbufc_ref[:, SUBLANES - (CONV_C - 1):SUBLANES, :] = sconv0_ref[...]
        cbufb_ref[:, SUBLANES:2 * SUBLANES, :] = cg_in.reshape(nseq, DEC_SEQ, B_WIDTH)
        cbufc_ref[:, SUBLANES:2 * SUBLANES, :] = xbc_in.reshape(nseq, DEC_SEQ, SSM_CONV_DIM)

    def shifted(buf_ref, c, back, width):
        if prompt:
            return buf_ref[pl.ds(SUBLANES + c * CHUNK - back, CHUNK), :]
        s0 = c * seqs_per_sub
        blk = buf_ref[s0:s0 + seqs_per_sub, SUBLANES - back:2 * SUBLANES - back, :]
        return blk.reshape(CHUNK, width)

    row = lax.broadcasted_iota(jnp.int32, (CHUNK, CHUNK), 0)
    col = lax.broadcasted_iota(jnp.int32, (CHUNK, CHUNK), 1)
    if prompt:
        causal = col <= row
    else:
        same_seq = (row >> 3) == (col >> 3)
        causal = same_seq & (col <= row)
        same01 = _bf(jnp.where(same_seq, 1.0, 0.0))
    causal01 = _bf(jnp.where(causal, 1.0, 0.0))
    erow = lax.broadcasted_iota(jnp.int32, (LANES, C_WIDTH), 0)
    ecol = lax.broadcasted_iota(jnp.int32, (LANES, C_WIDTH), 1)
    expand01 = _bf(jnp.where((ecol >> 6) == erow, 1.0, 0.0))

    def expand(a):
        a1 = _bf(a)
        a2 = _bf(a - a1.astype(F32))
        return _dot(a1, expand01) + _dot(a2, expand01)

    a_neg = -jnp.exp(alog_ref[...])
    if prompt:
        arow = lax.broadcasted_iota(jnp.int32, (CHUNK, A_HEADS * CHUNK), 0)
        acol = lax.broadcasted_iota(jnp.int32, (CHUNK, A_HEADS * CHUNK), 1)
        wa_b = _bf(jnp.where((acol & (CHUNK - 1)) <= arow, wa_ref[...], 0.0))

    for c in range(n_sub):
        r0 = c * CHUNK
        sl = slice(r0, r0 + CHUNK)

        u = _gelu(proj_ref[sl, COL_U:COL_U + A_WIDTH])
        v = _gelu(proj_ref[sl, COL_V:COL_V + A_WIDTH])
        if prompt:
            pairs = []
            for p in range(A_HEADS // 2):
                rhs = _block_diag_pair(v[:, p * LANES:(p + 1) * LANES])
                pairs.append(_dot(wa_b[:, p * 2 * CHUNK:(p + 1) * 2 * CHUNK], rhs))
            s = jnp.concatenate(pairs, axis=1) + ba_ref[...]
            vrow_ref[...] = v
        else:
            v3 = v.reshape(seqs_per_sub, DEC_SEQ, A_WIDTH)
            trow = lax.broadcasted_iota(jnp.int32, (DEC_SEQ, A_WIDTH), 0)
            s3 = jnp.zeros((seqs_per_sub, DEC_SEQ, A_WIDTH), F32)
            for jj in range(DEC_SEQ):
                coef = jnp.where(trow >= jj, wa_ref[jj], 0.0)
                s3 = s3 + coef[None] * v3[:, jj:jj + 1, :]
            s = (s3 + ba_ref[0:DEC_SEQ, :][None]).reshape(CHUNK, A_WIDTH)
            vrow_ref[c * seqs_per_sub:(c + 1) * seqs_per_sub] = v3
        ya = u * s

        conv = shifted(cbufb_ref, c, CONV_B - 1, B_WIDTH) * cw_ref[0:1, :]
        for k in range(1, CONV_B):
            conv = conv + shifted(cbufb_ref, c, CONV_B - 1 - k, B_WIDTH) * cw_ref[k:k + 1, :]
        yb = proj_ref[sl, COL_BG:COL_BG + B_WIDTH] * conv

        xbc = shifted(cbufc_ref, c, CONV_C - 1, SSM_CONV_DIM) * scw_ref[0:1, :]
        for k in range(1, CONV_C):
            xbc = xbc + shifted(cbufc_ref, c, CONV_C - 1 - k, SSM_CONV_DIM) * scw_ref[k:k + 1, :]
        xbc = _silu(xbc + scb_ref[...])
        xs = xbc[:, 0:C_WIDTH]
        bm = xbc[:, C_WIDTH:C_WIDTH + SSM_GROUPS * D_STATE]
        cm = xbc[:, C_WIDTH + SSM_GROUPS * D_STATE:]
        dt = _softplus(proj_ref[sl, COL_DT:COL_DT + LANES] + dtb_ref[...])
        a = dt * a_neg
        cum = _sel_left(causal01, a)
        if prompt:
            tot = jnp.broadcast_to(cum[CHUNK - 1:CHUNK, :], (CHUNK, LANES))
        else:
            tot = _sel_left(same01, a)
        expcum = jnp.exp(cum)
        w_state = jnp.exp(tot - cum) * dt
        exptot = jnp.exp(tot)
        ex = expand(jnp.concatenate([w_state, expcum, exptot], axis=0))
        w_state_x, expcum_x, exptot_x = ex[0:CHUNK], ex[CHUNK:2 * CHUNK], ex[2 * CHUNK:]
        cum_t = cum.T
        dt_t = dt.T
        xw = _bf(xs * w_state_x)
        bm_b = _bf(bm)
        cm_b = _bf(cm)

        y_pairs = []
        for g in range(SSM_GROUPS):
            gs = slice(g * D_STATE, (g + 1) * D_STATE)
            cb = _dot_nt(cm_b[:, gs], bm_b[:, gs])
            wts = []
            for kk in range(SSM_HEADS // SSM_GROUPS):
                k = g * (SSM_HEADS // SSM_GROUPS) + kk
                seg = cum[:, k:k + 1] - cum_t[k:k + 1, :]
                decay = jnp.exp(jnp.where(causal, seg, -jnp.inf))
                wts.append(_bf(cb * decay * dt_t[k:k + 1, :]))
            for p in range(2):
                k0 = g * (SSM_HEADS // SSM_GROUPS) + 2 * p
                lhs = jnp.concatenate([wts[2 * p], wts[2 * p + 1]], axis=1)
                rhs = _block_diag_pair(xs[:, k0 * SSM_HEAD_DIM:(k0 + 2) * SSM_HEAD_DIM])
                y_pairs.append(_dot(lhs, rhs))
        y = jnp.concatenate(y_pairs, axis=1)

        if prompt:
            y_off = []
            for g in range(SSM_GROUPS):
                gs = slice(g * D_STATE, (g + 1) * D_STATE)
                gw = slice(g * GROUP_W, (g + 1) * GROUP_W)
                st_g = st_ref[:, gw]
                y_off.append(_dot(cm_b[:, gs], _bf(st_g)))
                st_ref[:, gw] = exptot_x[0:1, gw] * st_g + _dot(_bf(bm[:, gs].T), xw[:, gw])
            y = y + jnp.concatenate(y_off, axis=1) * expcum_x
        else:
            xw_t = [_bf((xs * w_state_x)[:, g * GROUP_W:(g + 1) * GROUP_W].T)
                    for g in range(SSM_GROUPS)]
            y_off = []
            for i in range(seqs_per_sub):
                si = c * seqs_per_sub + i
                rs = slice(i * DEC_SEQ, (i + 1) * DEC_SEQ)
                own_rows = (row >> 3) == i
                parts = []
                for g in range(SSM_GROUPS):
                    gs = slice(g * D_STATE, (g + 1) * D_STATE)
                    gw = slice(g * GROUP_W, (g + 1) * GROUP_W)
                    h0 = ssm0_ref[si, gw, :]
                    parts.append(_dot_nt(_bf(cm[rs, gs]), _bf(h0)))
                    upd = _dot(xw_t[g], _bf(jnp.where(own_rows, bm[:, gs], 0.0)))
                    for kk in range(SSM_HEADS // SSM_GROUPS):
                        k = g * (SSM_HEADS // SSM_GROUPS) + kk
                        hs = slice(kk * SSM_HEAD_DIM, (kk + 1) * SSM_HEAD_DIM)
                        dec = jnp.broadcast_to(exptot[i * DEC_SEQ:i * DEC_SEQ + 1, k:k + 1],
                                               (SSM_HEAD_DIM, D_STATE))
                        ssmo_ref[si, k * SSM_HEAD_DIM:(k + 1) * SSM_HEAD_DIM, :] = (
                            dec * h0[hs, :] + upd[hs, :])
                y_off.append(jnp.concatenate(parts, axis=1))
            y = y + jnp.concatenate(y_off, axis=0) * expcum_x

        y = y + dsk_ref[...] * xs
        yc = _rmsnorm(y * _silu(proj_ref[sl, COL_Z:COL_Z + C_WIDTH]), snorm_ref[...])

        mix_ref[sl, 0:A_WIDTH] = _bf(ya)
        mix_ref[sl, A_WIDTH:A_WIDTH + B_WIDTH] = _bf(yb)
        mix_ref[sl, A_WIDTH + B_WIDTH:] = _bf(yc)

    xo_ref[...] = x + _dot(mix_ref[...], wout_ref[...])

    if prompt:
        convo_ref[...] = cbufb_ref[SUBLANES + rows - (CONV_B - 1):SUBLANES + rows, :]
        sconvo_ref[...] = cbufc_ref[SUBLANES + rows - (CONV_C - 1):SUBLANES + rows, :]
        cbufb_ref[0:SUBLANES, :] = cbufb_ref[rows:rows + SUBLANES, :]
        cbufc_ref[0:SUBLANES, :] = cbufc_ref[rows:rows + SUBLANES, :]

        @pl.when(j == pl.num_programs(1) - 1)
        def _():
            ssmo_ref[...] = st_ref[...].T
    else:
        convo_ref[...] = cbufb_ref[:, 2 * SUBLANES - (CONV_B - 1):2 * SUBLANES, :]
        sconvo_ref[...] = cbufc_ref[:, 2 * SUBLANES - (CONV_C - 1):2 * SUBLANES, :]


def _ff_kernel(x_ref, g2_ref, w1_ref, w2_ref, gf_ref, o_ref, f_ref, *, final):
    x = x_ref[...]
    h = _bf(_rmsnorm(x, g2_ref[...]))
    for c in range(D_FF // FF_COLS):
        cs = slice(c * FF_COLS, (c + 1) * FF_COLS)
        f = jnp.maximum(_dot(h, w1_ref[:, cs]), 0.0)
        f_ref[:, cs] = _bf(f * f)
    y = x + _dot(f_ref[...], w2_ref[...])
    if final:
        y = _rmsnorm(y, gf_ref[...])
    o_ref[...] = y


def _layer_spec(shape, layer, n_grid):
    zeros = (0,) * len(shape)
    if n_grid == 1:
        index_map = lambda i: (layer,) + zeros
    else:
        index_map = lambda i, j: (layer,) + zeros
    return pl.BlockSpec((None,) + tuple(shape), index_map, pipeline_mode=pl.Buffered(1))


def _mix_call(prompt, layer, x2d, params, states=None):
    (g1, win, wa, ba, cw, scw, scb, dtb, alog, dsk, snorm, wout) = params
    n_rows = x2d.shape[0]
    if prompt:
        rows = PROMPT_ROWS
        seq = 2048
        n_batch = n_rows // seq
        steps = seq // rows
        grid = (n_batch, steps)
        ng = 2
        xmap = lambda b, j: (b * steps + j, 0)
        bmap = lambda b, j: (b, 0, 0)
    else:
        rows = SAMPLE_SEQS * DEC_SEQ
        n_batch = n_rows // DEC_SEQ
        grid = (n_batch // SAMPLE_SEQS,)
        ng = 1
        xmap = lambda i: (i, 0)
        bmap = lambda i: (i, 0, 0)

    par_specs = [_layer_spec(p.shape[1:], layer, ng) for p in params]
    in_specs = [pl.BlockSpec((rows, D_MODEL), xmap)] + par_specs
    args = [x2d] + list(params)
    if prompt:
        out_shape = [
            jax.ShapeDtypeStruct((n_rows, D_MODEL), F32),
            jax.ShapeDtypeStruct((n_batch, CHUNK, A_WIDTH), F32),
            jax.ShapeDtypeStruct((n_batch, CONV_B - 1, B_WIDTH), F32),
            jax.ShapeDtypeStruct((n_batch, CONV_C - 1, SSM_CONV_DIM), F32),
            jax.ShapeDtypeStruct((n_batch, C_WIDTH, D_STATE), F32),
        ]
        out_specs = [
            pl.BlockSpec((rows, D_MODEL), xmap),
            pl.BlockSpec((None, CHUNK, A_WIDTH), bmap),
            pl.BlockSpec((None, CONV_B - 1, B_WIDTH), bmap),
            pl.BlockSpec((None, CONV_C - 1, SSM_CONV_DIM), bmap),
            pl.BlockSpec((None, C_WIDTH, D_STATE), bmap),
        ]
        scratch = [
            pltpu.VMEM((rows, D_IN_PAD), F32),
            pltpu.VMEM((rows, D_MODEL), BF16),
            pltpu.VMEM((SUBLANES + rows, B_WIDTH), F32),
            pltpu.VMEM((SUBLANES + rows, SSM_CONV_DIM), F32),
            pltpu.VMEM((D_STATE, C_WIDTH), F32),
        ]
        sem = ("parallel", "arbitrary")
    else:
        conv0, sconv0, ssm0 = states
        ns = SAMPLE_SEQS
        lmap = lambda i: (layer, i, 0, 0)
        in_specs += [
            pl.BlockSpec((None, ns, CONV_B - 1, B_WIDTH), lmap),
            pl.BlockSpec((None, ns, CONV_C - 1, SSM_CONV_DIM), lmap),
            pl.BlockSpec((None, ns, C_WIDTH, D_STATE), lmap),
        ]
        args += [conv0, sconv0, ssm0]
        out_shape = [
            jax.ShapeDtypeStruct((n_rows, D_MODEL), F32),
            jax.ShapeDtypeStruct((n_batch, DEC_SEQ, A_WIDTH), F32),
            jax.ShapeDtypeStruct((n_batch, CONV_B - 1, B_WIDTH), F32),
            jax.ShapeDtypeStruct((n_batch, CONV_C - 1, SSM_CONV_DIM), F32),
            jax.ShapeDtypeStruct((n_batch, C_WIDTH, D_STATE), F32),
        ]
        out_specs = [
            pl.BlockSpec((rows, D_MODEL), xmap),
            pl.BlockSpec((ns, DEC_SEQ, A_WIDTH), bmap),
            pl.BlockSpec((ns, CONV_B - 1, B_WIDTH), bmap),
            pl.BlockSpec((ns, CONV_C - 1, SSM_CONV_DIM), bmap),
            pl.BlockSpec((ns, C_WIDTH, D_STATE), bmap),
        ]
        scratch = [
            pltpu.VMEM((rows, D_IN_PAD), F32),
            pltpu.VMEM((rows, D_MODEL), BF16),
            pltpu.VMEM((ns, 2 * SUBLANES, B_WIDTH), F32),
            pltpu.VMEM((ns, 2 * SUBLANES, SSM_CONV_DIM), F32),
        ]
        sem = ("parallel",)

    return pl.pallas_call(
        functools.partial(_mix_kernel, prompt=prompt, rows=rows),
        grid=grid,
        in_specs=in_specs,
        out_specs=out_specs,
        out_shape=out_shape,
        scratch_shapes=scratch,
        compiler_params=pltpu.CompilerParams(dimension_semantics=sem, vmem_limit_bytes=VMEM_LIMIT),
        name="mix_prompt" if prompt else "mix_sample",
    )(*args)


def _ff_call(layer, x2d, g2, w1, w2, gf, final):
    n_rows = x2d.shape[0]
    return pl.pallas_call(
        functools.partial(_ff_kernel, final=final),
        grid=(n_rows // FF_ROWS,),
        in_specs=[
            pl.BlockSpec((FF_ROWS, D_MODEL), lambda i: (i, 0)),
            _layer_spec(g2.shape[1:], layer, 1),
            _layer_spec(w1.shape[1:], layer, 1),
            _layer_spec(w2.shape[1:], layer, 1),
            pl.BlockSpec(gf.shape, lambda i: (0, 0), pipeline_mode=pl.Buffered(1)),
        ],
        out_specs=pl.BlockSpec((FF_ROWS, D_MODEL), lambda i: (i, 0)),
        out_shape=jax.ShapeDtypeStruct((n_rows, D_MODEL), F32),
        scratch_shapes=[pltpu.VMEM((FF_ROWS, D_FF), BF16)],
        compiler_params=pltpu.CompilerParams(dimension_semantics=("parallel",),
                                             vmem_limit_bytes=VMEM_LIMIT),
        name="ffn",
    )(x2d, g2, w1, w2, gf)


def kernel(x_prompt, x_sample, state_conv, state_ssm_conv, state_ssm, norm1, w_in, w_s, b_s, conv_w,
           ssm_conv_w, ssm_conv_b, dt_bias, a_log, d_skip, ssm_norm, w_out, norm2, w_ff1, w_ff2,
           final_norm):
    bp, seq, _ = x_prompt.shape
    bs, dseq, _ = x_sample.shape
    assert seq % PROMPT_ROWS == 0 and dseq == DEC_SEQ and bs % SAMPLE_SEQS == 0

    win_b = _bf(jnp.pad(w_in, ((0, 0), (0, 0), (0, D_IN_PAD - D_IN))))
    wout_b = _bf(w_out)
    w1_b = _bf(w_ff1)
    w2_b = _bf(w_ff2)
    g1 = norm1.reshape(DEPTH, 1, D_MODEL)
    g2 = norm2.reshape(DEPTH, 1, D_MODEL)
    gf = final_norm.reshape(1, D_MODEL)
    wa_prompt = jnp.transpose(w_s, (0, 2, 1, 3)).reshape(DEPTH, CHUNK, A_HEADS * CHUNK)
    wa_sample = jnp.repeat(jnp.transpose(w_s[:, :, :DEC_SEQ, :DEC_SEQ], (0, 3, 2, 1)),
                           A_WIDTH // A_HEADS, axis=-1)
    ba = jnp.repeat(jnp.transpose(b_s, (0, 2, 1)), A_WIDTH // A_HEADS, axis=-1)
    scb = ssm_conv_b.reshape(DEPTH, 1, SSM_CONV_DIM)
    pad_heads = lambda p: jnp.pad(p, ((0, 0), (0, LANES - SSM_HEADS))).reshape(DEPTH, 1, LANES)
    dtb = pad_heads(dt_bias)
    alog = pad_heads(a_log)
    dsk = jnp.repeat(d_skip, SSM_HEAD_DIM, axis=-1).reshape(DEPTH, 1, C_WIDTH)
    snorm = ssm_norm.reshape(DEPTH, 1, C_WIDTH)
    common = (g1, win_b, None, ba, conv_w, ssm_conv_w, scb, dtb, alog, dsk, snorm, wout_b)
    par_prompt = common[:2] + (wa_prompt,) + common[3:]
    par_sample = common[:2] + (wa_sample,) + common[3:]

    xp = x_prompt.reshape(bp * seq, D_MODEL)
    xs = x_sample.reshape(bs * dseq, D_MODEL)
    ssm_in = state_ssm.reshape(DEPTH, bs, C_WIDTH, D_STATE)
    outs_p, outs_s = [], []
    for l in range(DEPTH):
        final = l == DEPTH - 1
        xp, *st_p = _mix_call(True, l, xp, par_prompt)
        xp = _ff_call(l, xp, g2, w1_b, w2_b, gf, final)
        xs, *st_s = _mix_call(False, l, xs, par_sample, (state_conv, state_ssm_conv, ssm_in))
        xs = _ff_call(l, xs, g2, w1_b, w2_b, gf, final)
        outs_p.append(st_p)
        outs_s.append(st_s)

    def stacked(outs, i, shape):
        return jnp.stack([o[i] for o in outs]).reshape(shape)

    y_prompt = xp.reshape(bp, seq, D_MODEL)
    y_sample = xs.reshape(bs, dseq, D_MODEL)
    ssm_shape = lambda b: (DEPTH, b, SSM_HEADS, SSM_HEAD_DIM, D_STATE)
    return (y_prompt, y_sample,
            stacked(outs_p, 0, (DEPTH, bp, CHUNK, A_WIDTH)),
            stacked(outs_p, 1, (DEPTH, bp, CONV_B - 1, B_WIDTH)),
            stacked(outs_p, 2, (DEPTH, bp, CONV_C - 1, SSM_CONV_DIM)),
            stacked(outs_p, 3, ssm_shape(bp)),
            stacked(outs_s, 0, (DEPTH, bs, DEC_SEQ, A_WIDTH)),
            stacked(outs_s, 1, (DEPTH, bs, CONV_B - 1, B_WIDTH)),
            stacked(outs_s, 2, (DEPTH, bs, CONV_C - 1, SSM_CONV_DIM)),
            stacked(outs_s, 3, ssm_shape(bs)))
```

```python
import functools

import jax
import jax.numpy as jnp
from jax import lax
from jax.experimental import pallas as pl
from jax.experimental.pallas import tpu as pltpu

D_MODEL = 1024
DEPTH = 4
A_WIDTH = 256
A_HEADS = 4
B_WIDTH = 256
C_WIDTH = 512
CONV_B = 3
CONV_C = 4
SSM_HEADS = 8
SSM_HEAD_DIM = 64
SSM_GROUPS = 2
HEADS_PER_GROUP = SSM_HEADS // SSM_GROUPS
D_STATE = 128
SSM_CONV_DIM = 1024
CHUNK = 128
D_FF = 4096
EPS = 1e-5
DEC_SEQ = 8

COL_U, COL_V, COL_BG, COL_CG, COL_HB, COL_Z, COL_XBC, COL_DT = 0, 256, 512, 768, 1024, 1280, 1792, 2816
D_IN = 2824
LANES = 128
SUBLANES = 8
D_IN_PAD = 2944
GROUP_W = C_WIDTH // SSM_GROUPS

PROMPT_ROWS = 256
SAMPLE_SEQS = 16
FF_ROWS = 512
FF_COLS = 1024
VMEM_LIMIT = 56 * 1024 * 1024

F32 = jnp.float32
BF16 = jnp.bfloat16


def _bf(x):
    return x.astype(BF16)


def _dot(a, b):
    return jnp.dot(a, b, preferred_element_type=F32)


def _dot_nt(a, b):
    return lax.dot_general(a, b, (((1,), (1,)), ((), ())), preferred_element_type=F32)


def _split3(a):
    a1 = _bf(a)
    r1 = a - a1.astype(F32)
    a2 = _bf(r1)
    r2 = r1 - a2.astype(F32)
    return a1, a2, _bf(r2)


def _sel_left(m01, a):
    a1, a2, a3 = _split3(a)
    return _dot(m01, a1) + _dot(m01, a2) + _dot(m01, a3)


def _rmsnorm(x, g):
    ms = jnp.mean(x * x, axis=-1, keepdims=True)
    return (x * lax.rsqrt(ms + EPS)) * g


def _gelu(x):
    return 0.5 * x * (1.0 + lax.erf(x * (0.5 ** 0.5)))


def _silu(x):
    return x * jax.nn.sigmoid(x)


def _softplus(x):
    return jnp.maximum(x, 0.0) + jnp.log1p(jnp.exp(-jnp.abs(x)))


def _block_diag_pair(t):
    lane = lax.broadcasted_iota(jnp.int32, t.shape, 1)
    lo = jnp.where(lane < SSM_HEAD_DIM, t, 0.0)
    hi = jnp.where(lane >= SSM_HEAD_DIM, t, 0.0)
    return _bf(jnp.concatenate([lo, hi], axis=0))


def _causal_taps_rows(xin, prev8, w_ref, n_taps):
    row8 = lax.broadcasted_iota(jnp.int32, prev8.shape, 0)
    out = None
    for k in range(n_taps):
        back = n_taps - 1 - k
        if back == 0:
            sh = xin
        else:
            r = pltpu.roll(xin, back, axis=0)
            first = jnp.where(row8 >= back, r[0:SUBLANES], pltpu.roll(prev8, back, axis=0))
            sh = jnp.concatenate([first, r[SUBLANES:]], axis=0)
        term = sh * w_ref[k:k + 1, :]
        out = term if out is None else out + term
    return out


def _causal_taps_seqs(buf_ref, s0, n_seq, w_ref, n_taps, width):
    out = None
    for k in range(n_taps):
        back = n_taps - 1 - k
        sh = buf_ref[s0:s0 + n_seq, SUBLANES - back:2 * SUBLANES - back, :].reshape(n_seq * DEC_SEQ, width)
        term = sh * w_ref[k:k + 1, :]
        out = term if out is None else out + term
    return out


def _mix_kernel(*refs, prompt, rows):
    if prompt:
        (x_ref, g1_ref, win_ref, wa_ref, ba_ref, cw_ref, scw_ref, scb_ref, dtb_ref, alog_ref,
         dsk_ref, snorm_ref, wout_ref,
         xo_ref, vrow_ref, convo_ref, sconvo_ref, ssmo_ref,
         tailb_ref, tailc_ref, st_ref) = refs
    else:
        (x_ref, g1_ref, win_ref, wa_ref, ba_ref, cw_ref, scw_ref, scb_ref, dtb_ref, alog_ref,
         dsk_ref, snorm_ref, wout_ref, conv0_ref, sconv0_ref, ssm0_ref,
         xo_ref, vrow_ref, convo_ref, sconvo_ref, ssmo_ref,
         cbufb_ref, cbufc_ref) = refs
    n_sub = rows // CHUNK
    seqs_per_sub = CHUNK // DEC_SEQ

    if prompt:
        j = pl.program_id(1)

        @pl.when(j == 0)
        def _():
            tailb_ref[...] = jnp.zeros(tailb_ref.shape, F32)
            tailc_ref[...] = jnp.zeros(tailc_ref.shape, F32)
            st_ref[...] = jnp.zeros(st_ref.shape, F32)
    else:
        cbufb_ref[:, SUBLANES - (CONV_B - 1):SUBLANES, :] = conv0_ref[...]
        cbufc_ref[:, SUBLANES - (CONV_C - 1):SUBLANES, :] = sconv0_ref[...]

    row = lax.broadcasted_iota(jnp.int32, (CHUNK, CHUNK), 0)
    col = lax.broadcasted_iota(jnp.int32, (CHUNK, CHUNK), 1)
    if prompt:
        causal = col <= row
    else:
        causal = ((row >> 3) == (col >> 3)) & (col <= row)
    causal01 = _bf(jnp.where(causal, 1.0, 0.0))
    erow = lax.broadcasted_iota(jnp.int32, (LANES, C_WIDTH), 0)
    ecol = lax.broadcasted_iota(jnp.int32, (LANES, C_WIDTH), 1)
    expand01 = _bf(jnp.where((ecol >> 6) == erow, 1.0, 0.0))

    def expand(a):
        a1 = _bf(a)
        a2 = _bf(a - a1.astype(F32))
        return _dot(a1, expand01) + _dot(a2, expand01)

    a_neg = -jnp.exp(alog_ref[...])
    if prompt:
        arow = lax.broadcasted_iota(jnp.int32, (CHUNK, A_HEADS * CHUNK), 0)
        acol = lax.broadcasted_iota(jnp.int32, (CHUNK, A_HEADS * CHUNK), 1)
        wa_b = _bf(jnp.where((acol & (CHUNK - 1)) <= arow, wa_ref[...], 0.0))
        prev_b = tailb_ref[...]
        prev_c = tailc_ref[...]

    for c in range(n_sub):
        sl = slice(c * CHUNK, (c + 1) * CHUNK)
        x = x_ref[sl, :]
        proj = _dot(_bf(_rmsnorm(x, g1_ref[...])), win_ref[...])

        u = _gelu(proj[:, COL_U:COL_U + A_WIDTH])
        v = _gelu(proj[:, COL_V:COL_V + A_WIDTH])
        if prompt:
            pairs = []
            for p in range(A_HEADS // 2):
                rhs = _block_diag_pair(v[:, p * LANES:(p + 1) * LANES])
                pairs.append(_dot(wa_b[:, p * 2 * CHUNK:(p + 1) * 2 * CHUNK], rhs))
            s = jnp.concatenate(pairs, axis=1) + ba_ref[...]
            if c == n_sub - 1:
                vrow_ref[...] = v
        else:
            s0 = c * seqs_per_sub
            v3 = v.reshape(seqs_per_sub, DEC_SEQ, A_WIDTH)
            trow = lax.broadcasted_iota(jnp.int32, (DEC_SEQ, A_WIDTH), 0)
            s3 = jnp.zeros((seqs_per_sub, DEC_SEQ, A_WIDTH), F32)
            for jj in range(DEC_SEQ):
                coef = jnp.where(trow >= jj, wa_ref[jj], 0.0)
                s3 = s3 + coef[None] * v3[:, jj:jj + 1, :]
            s = (s3 + ba_ref[0:DEC_SEQ, :][None]).reshape(CHUNK, A_WIDTH)
            vrow_ref[s0:s0 + seqs_per_sub] = v3
        ya = u * s

        cg_in = proj[:, COL_CG:COL_CG + B_WIDTH] * proj[:, COL_HB:COL_HB + B_WIDTH]
        xbc_in = proj[:, COL_XBC:COL_XBC + SSM_CONV_DIM]
        if prompt:
            conv = _causal_taps_rows(cg_in, prev_b, cw_ref, CONV_B)
            xbc = _causal_taps_rows(xbc_in, prev_c, scw_ref, CONV_C)
            prev_b = cg_in[CHUNK - SUBLANES:, :]
            prev_c = xbc_in[CHUNK - SUBLANES:, :]
        else:
            cbufb_ref[s0:s0 + seqs_per_sub, SUBLANES:, :] = cg_in.reshape(seqs_per_sub, DEC_SEQ, B_WIDTH)
            cbufc_ref[s0:s0 + seqs_per_sub, SUBLANES:, :] = xbc_in.reshape(seqs_per_sub, DEC_SEQ, SSM_CONV_DIM)
            conv = _causal_taps_seqs(cbufb_ref, s0, seqs_per_sub, cw_ref, CONV_B, B_WIDTH)
            xbc = _causal_taps_seqs(cbufc_ref, s0, seqs_per_sub, scw_ref, CONV_C, SSM_CONV_DIM)
        yb = proj[:, COL_BG:COL_BG + B_WIDTH] * conv

        xbc = _silu(xbc + scb_ref[...])
        xs = xbc[:, 0:C_WIDTH]
        bm = xbc[:, C_WIDTH:C_WIDTH + SSM_GROUPS * D_STATE]
        cm = xbc[:, C_WIDTH + SSM_GROUPS * D_STATE:]
        dt = _softplus(proj[:, COL_DT:COL_DT + LANES] + dtb_ref[...])
        cum = _sel_left(causal01, dt * a_neg)
        if prompt:
            tot = cum[CHUNK - 1:CHUNK, :]
        else:
            cum3 = cum.reshape(seqs_per_sub, DEC_SEQ, LANES)
            tot = jnp.broadcast_to(cum3[:, DEC_SEQ - 1:DEC_SEQ, :], cum3.shape).reshape(CHUNK, LANES)
        expcum = jnp.exp(cum)
        w_state = jnp.exp(tot - cum) * dt
        ex = expand(jnp.concatenate([w_state, expcum], axis=0))
        w_state_x, expcum_x = ex[0:CHUNK], ex[CHUNK:]
        cum_t = cum.T
        dt_t = dt.T
        xw = xs * w_state_x
        bm_b = _bf(bm)
        cm_b = _bf(cm)

        y_pairs = []
        for g in range(SSM_GROUPS):
            gs = slice(g * D_STATE, (g + 1) * D_STATE)
            cb = _dot_nt(cm_b[:, gs], bm_b[:, gs])
            wts = []
            for kk in range(HEADS_PER_GROUP):
                k = g * HEADS_PER_GROUP + kk
                seg = cum[:, k:k + 1] - cum_t[k:k + 1, :]
                decay = jnp.exp(jnp.where(causal, seg, -jnp.inf))
                wts.append(_bf(cb * decay * dt_t[k:k + 1, :]))
            for p in range(HEADS_PER_GROUP // 2):
                k0 = g * HEADS_PER_GROUP + 2 * p
                lhs = jnp.concatenate([wts[2 * p], wts[2 * p + 1]], axis=1)
                rhs = _block_diag_pair(xs[:, k0 * SSM_HEAD_DIM:(k0 + 2) * SSM_HEAD_DIM])
                y_pairs.append(_dot(lhs, rhs))
        y = jnp.concatenate(y_pairs, axis=1)

        if prompt:
            xw_b = _bf(xw)
            y_off = []
            for g in range(SSM_GROUPS):
                gs = slice(g * D_STATE, (g + 1) * D_STATE)
                gw = slice(g * GROUP_W, (g + 1) * GROUP_W)
                st_g = st_ref[:, gw]
                y_off.append(_dot(cm_b[:, gs], _bf(st_g)))
                st_ref[:, gw] = expcum_x[CHUNK - 1:CHUNK, gw] * st_g + _dot(_bf(bm[:, gs].T), xw_b[:, gw])
            y = y + jnp.concatenate(y_off, axis=1) * expcum_x
        else:
            exptot = jnp.exp(tot)
            xw_t = [_bf(xw[:, g * GROUP_W:(g + 1) * GROUP_W].T) for g in range(SSM_GROUPS)]
            y_off = []
            for i in range(seqs_per_sub):
                rs = slice(i * DEC_SEQ, (i + 1) * DEC_SEQ)
                own_rows = (row >> 3) == i
                parts = []
                for g in range(SSM_GROUPS):
                    gs = slice(g * D_STATE, (g + 1) * D_STATE)
                    gw = slice(g * GROUP_W, (g + 1) * GROUP_W)
                    h0 = ssm0_ref[s0 + i, gw, :]
                    parts.append(_dot_nt(_bf(cm[rs, gs]), _bf(h0)))
                    upd = _dot(xw_t[g], _bf(jnp.where(own_rows, bm[:, gs], 0.0)))
                    for kk in range(HEADS_PER_GROUP):
                        k = g * HEADS_PER_GROUP + kk
                        hs = slice(kk * SSM_HEAD_DIM, (kk + 1) * SSM_HEAD_DIM)
                        dec = jnp.broadcast_to(exptot[i * DEC_SEQ:i * DEC_SEQ + 1, k:k + 1],
                                               (SSM_HEAD_DIM, D_STATE))
                        ssmo_ref[s0 + i, k * SSM_HEAD_DIM:(k + 1) * SSM_HEAD_DIM, :] = (
                            dec * h0[hs, :] + upd[hs, :])
                y_off.append(jnp.concatenate(parts, axis=1))
            y = y + jnp.concatenate(y_off, axis=0) * expcum_x

        y = y + dsk_ref[...] * xs
        yc = _rmsnorm(y * _silu(proj[:, COL_Z:COL_Z + C_WIDTH]), snorm_ref[...])
        mix = jnp.concatenate([_bf(ya), _bf(yb), _bf(yc)], axis=1)
        xo_ref[sl, :] = x + _dot(mix, wout_ref[...])

    if prompt:
        tailb_ref[...] = prev_b
        tailc_ref[...] = prev_c
        convo_ref[...] = prev_b[SUBLANES - (CONV_B - 1):, :]
        sconvo_ref[...] = prev_c[SUBLANES - (CONV_C - 1):, :]

        @pl.when(j == pl.num_programs(1) - 1)
        def _():
            ssmo_ref[...] = st_ref[...].T
    else:
        convo_ref[...] = cbufb_ref[:, 2 * SUBLANES - (CONV_B - 1):, :]
        sconvo_ref[...] = cbufc_ref[:, 2 * SUBLANES - (CONV_C - 1):, :]


def _ff_kernel(x_ref, g2_ref, w1_ref, w2_ref, gf_ref, o_ref, f_ref, *, final):
    x = x_ref[...]
    h = _bf(_rmsnorm(x, g2_ref[...]))
    for c in range(D_FF // FF_COLS):
        cs = slice(c * FF_COLS, (c + 1) * FF_COLS)
        f = jnp.maximum(_dot(h, w1_ref[:, cs]), 0.0)
        f_ref[:, cs] = _bf(f * f)
    y = x + _dot(f_ref[...], w2_ref[...])
    if final:
        y = _rmsnorm(y, gf_ref[...])
    o_ref[...] = y


def _layer_spec(shape, layer, n_grid):
    zeros = (0,) * len(shape)
    if n_grid == 1:
        index_map = lambda i: (layer,) + zeros
    else:
        index_map = lambda i, j: (layer,) + zeros
    return pl.BlockSpec((None,) + tuple(shape), index_map, pipeline_mode=pl.Buffered(1))


def _mix_call(prompt, layer, x2d, params, states=None):
    n_rows = x2d.shape[0]
    if prompt:
        rows = PROMPT_ROWS
        seq = 2048
        n_batch = n_rows // seq
        steps = seq // rows
        grid = (n_batch, steps)
        ng = 2
        xmap = lambda b, j: (b * steps + j, 0)
        bmap = lambda b, j: (b, 0, 0)
    else:
        rows = SAMPLE_SEQS * DEC_SEQ
        n_batch = n_rows // DEC_SEQ
        grid = (n_batch // SAMPLE_SEQS,)
        ng = 1
        xmap = lambda i: (i, 0)
        bmap = lambda i: (i, 0, 0)

    par_specs = [_layer_spec(p.shape[1:], layer, ng) for p in params]
    in_specs = [pl.BlockSpec((rows, D_MODEL), xmap)] + par_specs
    args = [x2d] + list(params)
    if prompt:
        out_shape = [
            jax.ShapeDtypeStruct((n_rows, D_MODEL), F32),
            jax.ShapeDtypeStruct((n_batch, CHUNK, A_WIDTH), F32),
            jax.ShapeDtypeStruct((n_batch, CONV_B - 1, B_WIDTH), F32),
            jax.ShapeDtypeStruct((n_batch, CONV_C - 1, SSM_CONV_DIM), F32),
            jax.ShapeDtypeStruct((n_batch, C_WIDTH, D_STATE), F32),
        ]
        out_specs = [
            pl.BlockSpec((rows, D_MODEL), xmap),
            pl.BlockSpec((None, CHUNK, A_WIDTH), bmap),
            pl.BlockSpec((None, CONV_B - 1, B_WIDTH), bmap),
            pl.BlockSpec((None, CONV_C - 1, SSM_CONV_DIM), bmap),
            pl.BlockSpec((None, C_WIDTH, D_STATE), bmap),
        ]
        scratch = [
            pltpu.VMEM((SUBLANES, B_WIDTH), F32),
            pltpu.VMEM((SUBLANES, SSM_CONV_DIM), F32),
            pltpu.VMEM((D_STATE, C_WIDTH), F32),
        ]
        sem = ("parallel", "arbitrary")
    else:
        conv0, sconv0, ssm0 = states
        ns = SAMPLE_SEQS
        lmap = lambda i: (layer, i, 0, 0)
        in_specs += [
            pl.BlockSpec((None, ns, CONV_B - 1, B_WIDTH), lmap),
            pl.BlockSpec((None, ns, CONV_C - 1, SSM_CONV_DIM), lmap),
            pl.BlockSpec((None, ns, C_WIDTH, D_STATE), lmap),
        ]
        args += [conv0, sconv0, ssm0]
        out_shape = [
            jax.ShapeDtypeStruct((n_rows, D_MODEL), F32),
            jax.ShapeDtypeStruct((n_batch, DEC_SEQ, A_WIDTH), F32),
            jax.ShapeDtypeStruct((n_batch, CONV_B - 1, B_WIDTH), F32),
            jax.ShapeDtypeStruct((n_batch, CONV_C - 1, SSM_CONV_DIM), F32),
            jax.ShapeDtypeStruct((n_batch, C_WIDTH, D_STATE), F32),
        ]
        out_specs = [
            pl.BlockSpec((rows, D_MODEL), xmap),
            pl.BlockSpec((ns, DEC_SEQ, A_WIDTH), bmap),
            pl.BlockSpec((ns, CONV_B - 1, B_WIDTH), bmap),
            pl.BlockSpec((ns, CONV_C - 1, SSM_CONV_DIM), bmap),
            pl.BlockSpec((ns, C_WIDTH, D_STATE), bmap),
        ]
        scratch = [
            pltpu.VMEM((ns, 2 * SUBLANES, B_WIDTH), F32),
            pltpu.VMEM((ns, 2 * SUBLANES, SSM_CONV_DIM), F32),
        ]
        sem = ("parallel",)

    return pl.pallas_call(
        functools.partial(_mix_kernel, prompt=prompt, rows=rows),
        grid=grid,
        in_specs=in_specs,
        out_specs=out_specs,
        out_shape=out_shape,
        scratch_shapes=scratch,
        compiler_params=pltpu.CompilerParams(dimension_semantics=sem, vmem_limit_bytes=VMEM_LIMIT),
        name="mix_prompt" if prompt else "mix_sample",
    )(*args)


def _ff_call(layer, x2d, g2, w1, w2, gf, final):
    n_rows = x2d.shape[0]
    return pl.pallas_call(
        functools.partial(_ff_kernel, final=final),
        grid=(n_rows // FF_ROWS,),
        in_specs=[
            pl.BlockSpec((FF_ROWS, D_MODEL), lambda i: (i, 0)),
            _layer_spec(g2.shape[1:], layer, 1),
            _layer_spec(w1.shape[1:], layer, 1),
            _layer_spec(w2.shape[1:], layer, 1),
            pl.BlockSpec(gf.shape, lambda i: (0, 0), pipeline_mode=pl.Buffered(1)),
        ],
        out_specs=pl.BlockSpec((FF_ROWS, D_MODEL), lambda i: (i, 0)),
        out_shape=jax.ShapeDtypeStruct((n_rows, D_MODEL), F32),
        scratch_shapes=[pltpu.VMEM((FF_ROWS, D_FF), BF16)],
        compiler_params=pltpu.CompilerParams(dimension_semantics=("parallel",),
                                             vmem_limit_bytes=VMEM_LIMIT),
        name="ffn",
    )(x2d, g2, w1, w2, gf)


def kernel(x_prompt, x_sample, state_conv, state_ssm_conv, state_ssm, norm1, w_in, w_s, b_s, conv_w,
           ssm_conv_w, ssm_conv_b, dt_bias, a_log, d_skip, ssm_norm, w_out, norm2, w_ff1, w_ff2,
           final_norm):
    bp, seq, _ = x_prompt.shape
    bs, dseq, _ = x_sample.shape
    assert seq % PROMPT_ROWS == 0 and dseq == DEC_SEQ and bs % SAMPLE_SEQS == 0

    win_b = _bf(jnp.pad(w_in, ((0, 0), (0, 0), (0, D_IN_PAD - D_IN))))
    wout_b = _bf(w_out)
    w1_b = _bf(w_ff1)
    w2_b = _bf(w_ff2)
    g1 = norm1.reshape(DEPTH, 1, D_MODEL)
    g2 = norm2.reshape(DEPTH, 1, D_MODEL)
    gf = final_norm.reshape(1, D_MODEL)
    wa_prompt = jnp.transpose(w_s, (0, 2, 1, 3)).reshape(DEPTH, CHUNK, A_HEADS * CHUNK)
    wa_sample = jnp.repeat(jnp.transpose(w_s[:, :, :DEC_SEQ, :DEC_SEQ], (0, 3, 2, 1)),
                           A_WIDTH // A_HEADS, axis=-1)
    ba = jnp.repeat(jnp.transpose(b_s, (0, 2, 1)), A_WIDTH // A_HEADS, axis=-1)
    scb = ssm_conv_b.reshape(DEPTH, 1, SSM_CONV_DIM)
    pad_heads = lambda p: jnp.pad(p, ((0, 0), (0, LANES - SSM_HEADS))).reshape(DEPTH, 1, LANES)
    dtb = pad_heads(dt_bias)
    alog = pad_heads(a_log)
    dsk = jnp.repeat(d_skip, SSM_HEAD_DIM, axis=-1).reshape(DEPTH, 1, C_WIDTH)
    snorm = ssm_norm.reshape(DEPTH, 1, C_WIDTH)
    common = (g1, win_b, None, ba, conv_w, ssm_conv_w, scb, dtb, alog, dsk, snorm, wout_b)
    par_prompt = common[:2] + (wa_prompt,) + common[3:]
    par_sample = common[:2] + (wa_sample,) + common[3:]

    xp = x_prompt.reshape(bp * seq, D_MODEL)
    xs = x_sample.reshape(bs * dseq, D_MODEL)
    ssm_in = state_ssm.reshape(DEPTH, bs, C_WIDTH, D_STATE)
    outs_p, outs_s = [], []
    for l in range(DEPTH):
        final = l == DEPTH - 1
        xp, *st_p = _mix_call(True, l, xp, par_prompt)
        xp = _ff_call(l, xp, g2, w1_b, w2_b, gf, final)
        xs, *st_s = _mix_call(False, l, xs, par_sample, (state_conv, state_ssm_conv, ssm_in))
        xs = _ff_call(l, xs, g2, w1_b, w2_b, gf, final)
        outs_p.append(st_p)
        outs_s.append(st_s)

    def stacked(outs, i, shape):
        return jnp.stack([o[i] for o in outs]).reshape(shape)

    y_prompt = xp.reshape(bp, seq, D_MODEL)
    y_sample = xs.reshape(bs, dseq, D_MODEL)
    ssm_shape = lambda b: (DEPTH, b, SSM_HEADS, SSM_HEAD_DIM, D_STATE)
    return (y_prompt, y_sample,
            stacked(outs_p, 0, (DEPTH, bp, CHUNK, A_WIDTH)),
            stacked(outs_p, 1, (DEPTH, bp, CONV_B - 1, B_WIDTH)),
            stacked(outs_p, 2, (DEPTH, bp, CONV_C - 1, SSM_CONV_DIM)),
            stacked(outs_p, 3, ssm_shape(bp)),
            stacked(outs_s, 0, (DEPTH, bs, DEC_SEQ, A_WIDTH)),
            stacked(outs_s, 1, (DEPTH, bs, CONV_B - 1, B_WIDTH)),
            stacked(outs_s, 2, (DEPTH, bs, CONV_C - 1, SSM_CONV_DIM)),
            stacked(outs_s, 3, ssm_shape(bs)))
```

```python
import functools

import jax
import jax.numpy as jnp
from jax import lax
from jax.experimental import pallas as pl
from jax.experimental.pallas import tpu as pltpu

D_MODEL = 1024
DEPTH = 4
A_WIDTH = 256
A_HEADS = 4
B_WIDTH = 256
C_WIDTH = 512
CONV_B = 3
CONV_C = 4
SSM_HEADS = 8
SSM_HEAD_DIM = 64
SSM_GROUPS = 2
HEADS_PER_GROUP = SSM_HEADS // SSM_GROUPS
D_STATE = 128
SSM_CONV_DIM = 1024
CHUNK = 128
D_FF = 4096
EPS = 1e-5
DEC_SEQ = 8

COL_U, COL_V, COL_BG, COL_CG, COL_HB, COL_Z, COL_XBC, COL_DT = 0, 256, 512, 768, 1024, 1280, 1792, 2816
D_IN = 2824
LANES = 128
SUBLANES = 8
D_IN_PAD = 2944
GROUP_W = C_WIDTH // SSM_GROUPS

PROMPT_ROWS = 512
SAMPLE_SEQS = 16
FILL_COLS = 256
FF_ROWS = 512
FF_COLS = 1024
VMEM_LIMIT = 56 * 1024 * 1024

F32 = jnp.float32
BF16 = jnp.bfloat16


def _bf(x):
    return x.astype(BF16)


def _dot(a, b):
    return jnp.dot(a, b, preferred_element_type=F32)


def _dot_nt(a, b):
    return lax.dot_general(a, b, (((1,), (1,)), ((), ())), preferred_element_type=F32)


def _split3(a):
    a1 = _bf(a)
    r1 = a - a1.astype(F32)
    a2 = _bf(r1)
    r2 = r1 - a2.astype(F32)
    return a1, a2, _bf(r2)


def _sel_left(m01, a):
    a1, a2, a3 = _split3(a)
    return _dot(m01, a1) + _dot(m01, a2) + _dot(m01, a3)


def _rmsnorm(x, g):
    ms = jnp.mean(x * x, axis=-1, keepdims=True)
    return (x * lax.rsqrt(ms + EPS)) * g


def _gelu(x):
    return 0.5 * x * (1.0 + lax.erf(x * (0.5 ** 0.5)))


def _silu(x):
    return x * jax.nn.sigmoid(x)


def _softplus(x):
    return jnp.maximum(x, 0.0) + jnp.log1p(jnp.exp(-jnp.abs(x)))


def _block_diag_pair(t):
    lane = lax.broadcasted_iota(jnp.int32, t.shape, 1)
    lo = jnp.where(lane < SSM_HEAD_DIM, t, 0.0)
    hi = jnp.where(lane >= SSM_HEAD_DIM, t, 0.0)
    return _bf(jnp.concatenate([lo, hi], axis=0))


def _causal_taps_rows(xin, prev8, w_ref, n_taps):
    row8 = lax.broadcasted_iota(jnp.int32, prev8.shape, 0)
    out = None
    for k in range(n_taps):
        back = n_taps - 1 - k
        if back == 0:
            sh = xin
        else:
            r = pltpu.roll(xin, back, axis=0)
            first = jnp.where(row8 >= back, r[0:SUBLANES], pltpu.roll(prev8, back, axis=0))
            sh = jnp.concatenate([first, r[SUBLANES:]], axis=0)
        term = sh * w_ref[k:k + 1, :]
        out = term if out is None else out + term
    return out


def _causal_taps_seqs(buf_ref, s0, n_seq, w_ref, n_taps, width):
    out = None
    for k in range(n_taps):
        back = n_taps - 1 - k
        sh = buf_ref[s0:s0 + n_seq, SUBLANES - back:2 * SUBLANES - back, :].reshape(n_seq * DEC_SEQ, width)
        term = sh * w_ref[k:k + 1, :]
        out = term if out is None else out + term
    return out


def _mix_kernel(*refs, prompt, rows):
    if prompt:
        (x_ref, xn_ref, g1_ref, win_ref, wa_ref, ba_ref, cw_ref, scw_ref, scb_ref, dtb_ref, alog_ref,
         dsk_ref, snorm_ref, wout_ref,
         xo_ref, vrow_ref, convo_ref, sconvo_ref, ssmo_ref,
         tailb_ref, tailc_ref, st_ref, proja_ref, projb_ref, hn_ref) = refs
    else:
        (x_ref, g1_ref, win_ref, wa_ref, ba_ref, cw_ref, scw_ref, scb_ref, dtb_ref, alog_ref,
         dsk_ref, snorm_ref, wout_ref, conv0_ref, sconv0_ref, ssm0_ref,
         xo_ref, vrow_ref, convo_ref, sconvo_ref, ssmo_ref,
         cbufb_ref, cbufc_ref) = refs
    n_sub = rows // CHUNK
    seqs_per_sub = CHUNK // DEC_SEQ

    def in_proj(x_rows):
        return _dot(_bf(_rmsnorm(x_rows, g1_ref[...])), win_ref[...])

    if prompt:
        half = rows // 2
        subs_per_half = n_sub // 2
        n_fill = -(-D_IN_PAD // FILL_COLS)
        j = pl.program_id(1)

        @pl.when(j == 0)
        def _():
            tailb_ref[...] = jnp.zeros(tailb_ref.shape, F32)
            tailc_ref[...] = jnp.zeros(tailc_ref.shape, F32)
            st_ref[...] = jnp.zeros(st_ref.shape, F32)

        @pl.when((pl.program_id(0) == 0) & (j == 0))
        def _():
            proja_ref[...] = in_proj(x_ref[0:half, :])
    else:
        cbufb_ref[:, SUBLANES - (CONV_B - 1):SUBLANES, :] = conv0_ref[...]
        cbufc_ref[:, SUBLANES - (CONV_C - 1):SUBLANES, :] = sconv0_ref[...]

    row = lax.broadcasted_iota(jnp.int32, (CHUNK, CHUNK), 0)
    col = lax.broadcasted_iota(jnp.int32, (CHUNK, CHUNK), 1)
    if prompt:
        causal = col <= row
    else:
        causal = ((row >> 3) == (col >> 3)) & (col <= row)
    causal01 = _bf(jnp.where(causal, 1.0, 0.0))
    erow = lax.broadcasted_iota(jnp.int32, (LANES, C_WIDTH), 0)
    ecol = lax.broadcasted_iota(jnp.int32, (LANES, C_WIDTH), 1)
    expand01 = _bf(jnp.where((ecol >> 6) == erow, 1.0, 0.0))

    def expand(a):
        a1 = _bf(a)
        a2 = _bf(a - a1.astype(F32))
        return _dot(a1, expand01) + _dot(a2, expand01)

    a_neg = -jnp.exp(alog_ref[...])
    if prompt:
        arow = lax.broadcasted_iota(jnp.int32, (CHUNK, A_HEADS * CHUNK), 0)
        acol = lax.broadcasted_iota(jnp.int32, (CHUNK, A_HEADS * CHUNK), 1)
        wa_b = _bf(jnp.where((acol & (CHUNK - 1)) <= arow, wa_ref[...], 0.0))
        prev_b = tailb_ref[...]
        prev_c = tailc_ref[...]

    fills = []

    def fill_cols(dst, k):
        cs = slice(k * FILL_COLS, min((k + 1) * FILL_COLS, D_IN_PAD))
        dst[:, cs] = _dot(hn_ref[...], win_ref[:, cs])

    def fill():
        if fills:
            fills.pop(0)()

    for c in range(n_sub):
        sl = slice(c * CHUNK, (c + 1) * CHUNK)
        x = x_ref[sl, :]
        if prompt:
            if c % subs_per_half == 0:
                nxt = x_ref[half:rows, :] if c == 0 else xn_ref[...]
                dst = projb_ref if c == 0 else proja_ref
                hn_ref[...] = _bf(_rmsnorm(nxt, g1_ref[...]))
                fills.extend(functools.partial(fill_cols, dst, k) for k in range(n_fill))
            pref = proja_ref if c < subs_per_half else projb_ref
            lr = slice((c % subs_per_half) * CHUNK, (c % subs_per_half + 1) * CHUNK)
            P = lambda c0, w, pref=pref, lr=lr: pref[lr, c0:c0 + w]
        else:
            proj = in_proj(x)
            P = lambda c0, w, proj=proj: proj[:, c0:c0 + w]

        u = _gelu(P(COL_U, A_WIDTH))
        v = _gelu(P(COL_V, A_WIDTH))
        if prompt:
            pairs = []
            for p in range(A_HEADS // 2):
                rhs = _block_diag_pair(v[:, p * LANES:(p + 1) * LANES])
                pairs.append(_dot(wa_b[:, p * 2 * CHUNK:(p + 1) * 2 * CHUNK], rhs))
            s = jnp.concatenate(pairs, axis=1) + ba_ref[...]
            if c == n_sub - 1:
                vrow_ref[...] = v
        else:
            s0 = c * seqs_per_sub
            v3 = v.reshape(seqs_per_sub, DEC_SEQ, A_WIDTH)
            trow = lax.broadcasted_iota(jnp.int32, (DEC_SEQ, A_WIDTH), 0)
            s3 = jnp.zeros((seqs_per_sub, DEC_SEQ, A_WIDTH), F32)
            for jj in range(DEC_SEQ):
                coef = jnp.where(trow >= jj, wa_ref[jj], 0.0)
                s3 = s3 + coef[None] * v3[:, jj:jj + 1, :]
            s = (s3 + ba_ref[0:DEC_SEQ, :][None]).reshape(CHUNK, A_WIDTH)
            vrow_ref[s0:s0 + seqs_per_sub] = v3
        ya = u * s
        fill()

        cg_in = P(COL_CG, B_WIDTH) * P(COL_HB, B_WIDTH)
        xbc_in = P(COL_XBC, SSM_CONV_DIM)
        if prompt:
            conv = _causal_taps_rows(cg_in, prev_b, cw_ref, CONV_B)
            xbc = _causal_taps_rows(xbc_in, prev_c, scw_ref, CONV_C)
            prev_b = cg_in[CHUNK - SUBLANES:, :]
            prev_c = xbc_in[CHUNK - SUBLANES:, :]
        else:
            cbufb_ref[s0:s0 + seqs_per_sub, SUBLANES:, :] = cg_in.reshape(seqs_per_sub, DEC_SEQ, B_WIDTH)
            cbufc_ref[s0:s0 + seqs_per_sub, SUBLANES:, :] = xbc_in.reshape(seqs_per_sub, DEC_SEQ, SSM_CONV_DIM)
            conv = _causal_taps_seqs(cbufb_ref, s0, seqs_per_sub, cw_ref, CONV_B, B_WIDTH)
            xbc = _causal_taps_seqs(cbufc_ref, s0, seqs_per_sub, scw_ref, CONV_C, SSM_CONV_DIM)
        yb = P(COL_BG, B_WIDTH) * conv
        fill()

        xbc = _silu(xbc + scb_ref[...])
        xs = xbc[:, 0:C_WIDTH]
        bm = xbc[:, C_WIDTH:C_WIDTH + SSM_GROUPS * D_STATE]
        cm = xbc[:, C_WIDTH + SSM_GROUPS * D_STATE:]
        dt = _softplus(P(COL_DT, LANES) + dtb_ref[...])
        cum = _sel_left(causal01, dt * a_neg)
        if prompt:
            tot = cum[CHUNK - 1:CHUNK, :]
        else:
            cum3 = cum.reshape(seqs_per_sub, DEC_SEQ, LANES)
            tot = jnp.broadcast_to(cum3[:, DEC_SEQ - 1:DEC_SEQ, :], cum3.shape).reshape(CHUNK, LANES)
        expcum = jnp.exp(cum)
        w_state = jnp.exp(tot - cum) * dt
        ex = expand(jnp.concatenate([w_state, expcum], axis=0))
        w_state_x, expcum_x = ex[0:CHUNK], ex[CHUNK:]
        cum_t = cum.T
        dt_t = dt.T
        xw = xs * w_state_x
        bm_b = _bf(bm)
        cm_b = _bf(cm)
        fill()

        y_pairs = []
        for g in range(SSM_GROUPS):
            gs = slice(g * D_STATE, (g + 1) * D_STATE)
            cb = _dot_nt(cm_b[:, gs], bm_b[:, gs])
            wts = []
            for kk in range(HEADS_PER_GROUP):
                k = g * HEADS_PER_GROUP + kk
                seg = cum[:, k:k + 1] - cum_t[k:k + 1, :]
                decay = jnp.exp(jnp.where(causal, seg, -jnp.inf))
                wts.append(_bf(cb * decay * dt_t[k:k + 1, :]))
            for p in range(HEADS_PER_GROUP // 2):
                k0 = g * HEADS_PER_GROUP + 2 * p
                lhs = jnp.concatenate([wts[2 * p], wts[2 * p + 1]], axis=1)
                rhs = _block_diag_pair(xs[:, k0 * SSM_HEAD_DIM:(k0 + 2) * SSM_HEAD_DIM])
                y_pairs.append(_dot(lhs, rhs))
            fill()
        y = jnp.concatenate(y_pairs, axis=1)

        if prompt:
            xw_b = _bf(xw)
            y_off = []
            for g in range(SSM_GROUPS):
                gs = slice(g * D_STATE, (g + 1) * D_STATE)
                gw = slice(g * GROUP_W, (g + 1) * GROUP_W)
                st_g = st_ref[:, gw]
                y_off.append(_dot(cm_b[:, gs], _bf(st_g)))
                st_ref[:, gw] = expcum_x[CHUNK - 1:CHUNK, gw] * st_g + _dot(_bf(bm[:, gs].T), xw_b[:, gw])
            y = y + jnp.concatenate(y_off, axis=1) * expcum_x
        else:
            exptot = jnp.exp(tot)
            xw_t = [_bf(xw[:, g * GROUP_W:(g + 1) * GROUP_W].T) for g in range(SSM_GROUPS)]
            y_off = []
            for i in range(seqs_per_sub):
                rs = slice(i * DEC_SEQ, (i + 1) * DEC_SEQ)
                own_rows = (row >> 3) == i
                parts = []
                for g in range(SSM_GROUPS):
                    gs = slice(g * D_STATE, (g + 1) * D_STATE)
                    gw = slice(g * GROUP_W, (g + 1) * GROUP_W)
                    h0 = ssm0_ref[s0 + i, gw, :]
                    parts.append(_dot_nt(_bf(cm[rs, gs]), _bf(h0)))
                    upd = _dot(xw_t[g], _bf(jnp.where(own_rows, bm[:, gs], 0.0)))
                    for kk in range(HEADS_PER_GROUP):
                        k = g * HEADS_PER_GROUP + kk
                        hs = slice(kk * SSM_HEAD_DIM, (kk + 1) * SSM_HEAD_DIM)
                        dec = jnp.broadcast_to(exptot[i * DEC_SEQ:i * DEC_SEQ + 1, k:k + 1],
                                               (SSM_HEAD_DIM, D_STATE))
                        ssmo_ref[s0 + i, k * SSM_HEAD_DIM:(k + 1) * SSM_HEAD_DIM, :] = (
                            dec * h0[hs, :] + upd[hs, :])
                y_off.append(jnp.concatenate(parts, axis=1))
            y = y + jnp.concatenate(y_off, axis=0) * expcum_x

        y = y + dsk_ref[...] * xs
        yc = _rmsnorm(y * _silu(P(COL_Z, C_WIDTH)), snorm_ref[...])
        fill()
        if prompt and (c + 1) % subs_per_half == 0:
            while fills:
                fill()
        mix = jnp.concatenate([_bf(ya), _bf(yb), _bf(yc)], axis=1)
        xo_ref[sl, :] = x + _dot(mix, wout_ref[...])

    if prompt:
        tailb_ref[...] = prev_b
        tailc_ref[...] = prev_c
        convo_ref[...] = prev_b[SUBLANES - (CONV_B - 1):, :]
        sconvo_ref[...] = prev_c[SUBLANES - (CONV_C - 1):, :]

        @pl.when(j == pl.num_programs(1) - 1)
        def _():
            ssmo_ref[...] = st_ref[...].T
    else:
        convo_ref[...] = cbufb_ref[:, 2 * SUBLANES - (CONV_B - 1):, :]
        sconvo_ref[...] = cbufc_ref[:, 2 * SUBLANES - (CONV_C - 1):, :]


def _ff_kernel(x_ref, g2_ref, w1_ref, w2_ref, gf_ref, o_ref, f_ref, *, final):
    x = x_ref[...]
    h = _bf(_rmsnorm(x, g2_ref[...]))
    for c in range(D_FF // FF_COLS):
        cs = slice(c * FF_COLS, (c + 1) * FF_COLS)
        f = jnp.maximum(_dot(h, w1_ref[:, cs]), 0.0)
        f_ref[:, cs] = _bf(f * f)
    y = x + _dot(f_ref[...], w2_ref[...])
    if final:
        y = _rmsnorm(y, gf_ref[...])
    o_ref[...] = y


def _layer_spec(shape, layer, n_grid):
    zeros = (0,) * len(shape)
    if n_grid == 1:
        index_map = lambda i: (layer,) + zeros
    else:
        index_map = lambda i, j: (layer,) + zeros
    return pl.BlockSpec((None,) + tuple(shape), index_map, pipeline_mode=pl.Buffered(1))


def _mix_call(prompt, layer, x2d, params, states=None):
    n_rows = x2d.shape[0]
    if prompt:
        rows = PROMPT_ROWS
        seq = 2048
        n_batch = n_rows // seq
        steps = seq // rows
        grid = (n_batch, steps)
        ng = 2
        xmap = lambda b, j: (b * steps + j, 0)
        bmap = lambda b, j: (b, 0, 0)
    else:
        rows = SAMPLE_SEQS * DEC_SEQ
        n_batch = n_rows // DEC_SEQ
        grid = (n_batch // SAMPLE_SEQS,)
        ng = 1
        xmap = lambda i: (i, 0)
        bmap = lambda i: (i, 0, 0)

    par_specs = [_layer_spec(p.shape[1:], layer, ng) for p in params]
    in_specs = [pl.BlockSpec((rows, D_MODEL), xmap)] + par_specs
    args = [x2d] + list(params)
    if prompt:
        half = rows // 2
        last_half = n_rows // half - 1
        nmap = lambda b, j: (jnp.minimum(2 * (b * steps + j) + 2, last_half), 0)
        in_specs.insert(1, pl.BlockSpec((half, D_MODEL), nmap))
        args.insert(1, x2d)
        out_shape = [
            jax.ShapeDtypeStruct((n_rows, D_MODEL), F32),
            jax.ShapeDtypeStruct((n_batch, CHUNK, A_WIDTH), F32),
            jax.ShapeDtypeStruct((n_batch, CONV_B - 1, B_WIDTH), F32),
            jax.ShapeDtypeStruct((n_batch, CONV_C - 1, SSM_CONV_DIM), F32),
            jax.ShapeDtypeStruct((n_batch, C_WIDTH, D_STATE), F32),
        ]
        out_specs = [
            pl.BlockSpec((rows, D_MODEL), xmap),
            pl.BlockSpec((None, CHUNK, A_WIDTH), bmap),
            pl.BlockSpec((None, CONV_B - 1, B_WIDTH), bmap),
            pl.BlockSpec((None, CONV_C - 1, SSM_CONV_DIM), bmap),
            pl.BlockSpec((None, C_WIDTH, D_STATE), bmap),
        ]
        scratch = [
            pltpu.VMEM((SUBLANES, B_WIDTH), F32),
            pltpu.VMEM((SUBLANES, SSM_CONV_DIM), F32),
            pltpu.VMEM((D_STATE, C_WIDTH), F32),
            pltpu.VMEM((half, D_IN_PAD), F32),
            pltpu.VMEM((half, D_IN_PAD), F32),
            pltpu.VMEM((half, D_MODEL), BF16),
        ]
        sem = ("arbitrary", "arbitrary")
    else:
        conv0, sconv0, ssm0 = states
        ns = SAMPLE_SEQS
        lmap = lambda i: (layer, i, 0, 0)
        in_specs += [
            pl.BlockSpec((None, ns, CONV_B - 1, B_WIDTH), lmap),
            pl.BlockSpec((None, ns, CONV_C - 1, SSM_CONV_DIM), lmap),
            pl.BlockSpec((None, ns, C_WIDTH, D_STATE), lmap),
        ]
        args += [conv0, sconv0, ssm0]
        out_shape = [
            jax.ShapeDtypeStruct((n_rows, D_MODEL), F32),
            jax.ShapeDtypeStruct((n_batch, DEC_SEQ, A_WIDTH), F32),
            jax.ShapeDtypeStruct((n_batch, CONV_B - 1, B_WIDTH), F32),
            jax.ShapeDtypeStruct((n_batch, CONV_C - 1, SSM_CONV_DIM), F32),
            jax.ShapeDtypeStruct((n_batch, C_WIDTH, D_STATE), F32),
        ]
        out_specs = [
            pl.BlockSpec((rows, D_MODEL), xmap),
            pl.BlockSpec((ns, DEC_SEQ, A_WIDTH), bmap),
            pl.BlockSpec((ns, CONV_B - 1, B_WIDTH), bmap),
            pl.BlockSpec((ns, CONV_C - 1, SSM_CONV_DIM), bmap),
            pl.BlockSpec((ns, C_WIDTH, D_STATE), bmap),
        ]
        scratch = [
            pltpu.VMEM((ns, 2 * SUBLANES, B_WIDTH), F32),
            pltpu.VMEM((ns, 2 * SUBLANES, SSM_CONV_DIM), F32),
        ]
        sem = ("parallel",)

    return pl.pallas_call(
        functools.partial(_mix_kernel, prompt=prompt, rows=rows),
        grid=grid,
        in_specs=in_specs,
        out_specs=out_specs,
        out_shape=out_shape,
        scratch_shapes=scratch,
        compiler_params=pltpu.CompilerParams(dimension_semantics=sem, vmem_limit_bytes=VMEM_LIMIT),
        name="mix_prompt" if prompt else "mix_sample",
    )(*args)


def _ff_call(layer, x2d, g2, w1, w2, gf, final):
    n_rows = x2d.shape[0]
    return pl.pallas_call(
        functools.partial(_ff_kernel, final=final),
        grid=(n_rows // FF_ROWS,),
        in_specs=[
            pl.BlockSpec((FF_ROWS, D_MODEL), lambda i: (i, 0)),
            _layer_spec(g2.shape[1:], layer, 1),
            _layer_spec(w1.shape[1:], layer, 1),
            _layer_spec(w2.shape[1:], layer, 1),
            pl.BlockSpec(gf.shape, lambda i: (0, 0), pipeline_mode=pl.Buffered(1)),
        ],
        out_specs=pl.BlockSpec((FF_ROWS, D_MODEL), lambda i: (i, 0)),
        out_shape=jax.ShapeDtypeStruct((n_rows, D_MODEL), F32),
        scratch_shapes=[pltpu.VMEM((FF_ROWS, D_FF), BF16)],
        compiler_params=pltpu.CompilerParams(dimension_semantics=("parallel",),
                                             vmem_limit_bytes=VMEM_LIMIT),
        name="ffn",
    )(x2d, g2, w1, w2, gf)


def kernel(x_prompt, x_sample, state_conv, state_ssm_conv, state_ssm, norm1, w_in, w_s, b_s, conv_w,
           ssm_conv_w, ssm_conv_b, dt_bias, a_log, d_skip, ssm_norm, w_out, norm2, w_ff1, w_ff2,
           final_norm):
    bp, seq, _ = x_prompt.shape
    bs, dseq, _ = x_sample.shape
    assert seq % PROMPT_ROWS == 0 and dseq == DEC_SEQ and bs % SAMPLE_SEQS == 0

    win_b = _bf(jnp.pad(w_in, ((0, 0), (0, 0), (0, D_IN_PAD - D_IN))))
    wout_b = _bf(w_out)
    w1_b = _bf(w_ff1)
    w2_b = _bf(w_ff2)
    g1 = norm1.reshape(DEPTH, 1, D_MODEL)
    g2 = norm2.reshape(DEPTH, 1, D_MODEL)
    gf = final_norm.reshape(1, D_MODEL)
    wa_prompt = jnp.transpose(w_s, (0, 2, 1, 3)).reshape(DEPTH, CHUNK, A_HEADS * CHUNK)
    wa_sample = jnp.repeat(jnp.transpose(w_s[:, :, :DEC_SEQ, :DEC_SEQ], (0, 3, 2, 1)),
                           A_WIDTH // A_HEADS, axis=-1)
    ba = jnp.repeat(jnp.transpose(b_s, (0, 2, 1)), A_WIDTH // A_HEADS, axis=-1)
    scb = ssm_conv_b.reshape(DEPTH, 1, SSM_CONV_DIM)
    pad_heads = lambda p: jnp.pad(p, ((0, 0), (0, LANES - SSM_HEADS))).reshape(DEPTH, 1, LANES)
    dtb = pad_heads(dt_bias)
    alog = pad_heads(a_log)
    dsk = jnp.repeat(d_skip, SSM_HEAD_DIM, axis=-1).reshape(DEPTH, 1, C_WIDTH)
    snorm = ssm_norm.reshape(DEPTH, 1, C_WIDTH)
    common = (g1, win_b, None, ba, conv_w, ssm_conv_w, scb, dtb, alog, dsk, snorm, wout_b)
    par_prompt = common[:2] + (wa_prompt,) + common[3:]
    par_sample = common[:2] + (wa_sample,) + common[3:]

    xp = x_prompt.reshape(bp * seq, D_MODEL)
    xs = x_sample.reshape(bs * dseq, D_MODEL)
    ssm_in = state_ssm.reshape(DEPTH, bs, C_WIDTH, D_STATE)
    outs_p, outs_s = [], []
    for l in range(DEPTH):
        final = l == DEPTH - 1
        xp, *st_p = _mix_call(True, l, xp, par_prompt)
        xp = _ff_call(l, xp, g2, w1_b, w2_b, gf, final)
        xs, *st_s = _mix_call(False, l, xs, par_sample, (state_conv, state_ssm_conv, ssm_in))
        xs = _ff_call(l, xs, g2, w1_b, w2_b, gf, final)
        outs_p.append(st_p)
        outs_s.append(st_s)

    def stacked(outs, i, shape):
        return jnp.stack([o[i] for o in outs]).reshape(shape)

    y_prompt = xp.reshape(bp, seq, D_MODEL)
    y_sample = xs.reshape(bs, dseq, D_MODEL)
    ssm_shape = lambda b: (DEPTH, b, SSM_HEADS, SSM_HEAD_DIM, D_STATE)
    return (y_prompt, y_sample,
            stacked(outs_p, 0, (DEPTH, bp, CHUNK, A_WIDTH)),
            stacked(outs_p, 1, (DEPTH, bp, CONV_B - 1, B_WIDTH)),
            stacked(outs_p, 2, (DEPTH, bp, CONV_C - 1, SSM_CONV_DIM)),
            stacked(outs_p, 3, ssm_shape(bp)),
            stacked(outs_s, 0, (DEPTH, bs, DEC_SEQ, A_WIDTH)),
            stacked(outs_s, 1, (DEPTH, bs, CONV_B - 1, B_WIDTH)),
            stacked(outs_s, 2, (DEPTH, bs, CONV_C - 1, SSM_CONV_DIM)),
            stacked(outs_s, 3, ssm_shape(bs)))
```

```python
import functools

import jax
import jax.numpy as jnp
from jax import lax
from jax.experimental import pallas as pl
from jax.experimental.pallas import tpu as pltpu

D_MODEL = 1024
DEPTH = 4
A_WIDTH = 256
A_HEADS = 4
B_WIDTH = 256
C_WIDTH = 512
CONV_B = 3
CONV_C = 4
SSM_HEADS = 8
SSM_HEAD_DIM = 64
SSM_GROUPS = 2
HEADS_PER_GROUP = SSM_HEADS // SSM_GROUPS
D_STATE = 128
SSM_CONV_DIM = 1024
CHUNK = 128
D_FF = 4096
EPS = 1e-5
DEC_SEQ = 8

COL_U, COL_V, COL_BG, COL_CG, COL_HB, COL_Z, COL_XBC, COL_DT = 0, 256, 512, 768, 1024, 1280, 1792, 2816
D_IN = 2824
LANES = 128
SUBLANES = 8
D_IN_PAD = 2944
GROUP_W = C_WIDTH // SSM_GROUPS

PROMPT_ROWS = 512
SAMPLE_SEQS = 16
FILL_COLS = 256
OUT_COLS = 256
CONV_C_PARTS = 4
CONV_C_PART_W = SSM_CONV_DIM // CONV_C_PARTS
FILL_POINTS = 14
FF_ROWS = 512
FF_COLS = 1024
VMEM_LIMIT = 56 * 1024 * 1024

F32 = jnp.float32
BF16 = jnp.bfloat16


def _bf(x):
    return x.astype(BF16)


def _dot(a, b):
    return jnp.dot(a, b, preferred_element_type=F32)


def _dot_nt(a, b):
    return lax.dot_general(a, b, (((1,), (1,)), ((), ())), preferred_element_type=F32)


def _cumsum_rows(m01, a):
    a1 = _bf(a)
    a2 = _bf(a - a1.astype(F32))
    r = _dot(m01, jnp.concatenate([a1, a2], axis=1))
    return r[:, 0:LANES] + r[:, LANES:]


def _rmsnorm(x, g):
    ms = jnp.mean(x * x, axis=-1, keepdims=True)
    return (x * lax.rsqrt(ms + EPS)) * g


def _gelu(x):
    return 0.5 * x * (1.0 + lax.erf(x * (0.5 ** 0.5)))


def _silu(x):
    return x * jax.nn.sigmoid(x)


def _softplus(x):
    return jnp.maximum(x, 0.0) + jnp.log1p(jnp.exp(-jnp.abs(x)))


def _block_diag_pair(t):
    lane = lax.broadcasted_iota(jnp.int32, t.shape, 1)
    lo = jnp.where(lane < SSM_HEAD_DIM, t, 0.0)
    hi = jnp.where(lane >= SSM_HEAD_DIM, t, 0.0)
    return _bf(jnp.concatenate([lo, hi], axis=0))


def _causal_taps_rows(xin, prev8, w):
    n_taps = w.shape[0]
    row8 = lax.broadcasted_iota(jnp.int32, prev8.shape, 0)
    out = None
    for k in range(n_taps):
        back = n_taps - 1 - k
        if back == 0:
            sh = xin
        else:
            r = pltpu.roll(xin, back, axis=0)
            first = jnp.where(row8 >= back, r[0:SUBLANES], pltpu.roll(prev8, back, axis=0))
            sh = jnp.concatenate([first, r[SUBLANES:]], axis=0)
        term = sh * w[k:k + 1, :]
        out = term if out is None else out + term
    return out


def _causal_taps_seqs(buf_ref, s0, n_seq, w_ref, n_taps, width):
    out = None
    for k in range(n_taps):
        back = n_taps - 1 - k
        sh = buf_ref[s0:s0 + n_seq, SUBLANES - back:2 * SUBLANES - back, :].reshape(n_seq * DEC_SEQ, width)
        term = sh * w_ref[k:k + 1, :]
        out = term if out is None else out + term
    return out


def _mix_kernel(*refs, prompt, rows, n_in, n_carry):
    refs = refs[:n_in] + refs[n_in + n_carry:]
    if prompt:
        (x_ref, xn_ref, g1_ref, win_ref, wa_ref, ba_ref, cw_ref, scw_ref, scb_ref, dtb_ref, alog_ref,
         dsk_ref, snorm_ref, wout_ref,
         xo_ref, vrow_ref, convo_ref, sconvo_ref, ssmo_ref,
         tailb_ref, tailc_ref, st_ref, proja_ref, projb_ref, hn_ref, mixa_ref, mixb_ref) = refs
    else:
        (x_ref, g1_ref, win_ref, wa_ref, ba_ref, cw_ref, scw_ref, scb_ref, dtb_ref, alog_ref,
         dsk_ref, snorm_ref, wout_ref, conv0_ref, sconv0_ref, ssm0_ref,
         xo_ref, vrow_ref, convo_ref, sconvo_ref, ssmo_ref,
         cbufb_ref, cbufc_ref) = refs
    n_sub = rows // CHUNK
    seqs_per_sub = CHUNK // DEC_SEQ

    def in_proj(x_rows):
        return _dot(_bf(_rmsnorm(x_rows, g1_ref[...])), win_ref[...])

    if prompt:
        half = rows // 2
        subs_per_half = n_sub // 2
        n_fill = -(-D_IN_PAD // FILL_COLS)
        j = pl.program_id(1)

        @pl.when(j == 0)
        def _():
            tailb_ref[...] = jnp.zeros(tailb_ref.shape, F32)
            tailc_ref[...] = jnp.zeros(tailc_ref.shape, F32)
            st_ref[...] = jnp.zeros(st_ref.shape, F32)

        @pl.when((pl.program_id(0) == 0) & (j == 0))
        def _():
            proja_ref[...] = in_proj(x_ref[0:half, :])
    else:
        cbufb_ref[:, SUBLANES - (CONV_B - 1):SUBLANES, :] = conv0_ref[...]
        cbufc_ref[:, SUBLANES - (CONV_C - 1):SUBLANES, :] = sconv0_ref[...]

    row = lax.broadcasted_iota(jnp.int32, (CHUNK, CHUNK), 0)
    col = lax.broadcasted_iota(jnp.int32, (CHUNK, CHUNK), 1)
    if prompt:
        causal = col <= row
    else:
        causal = ((row >> 3) == (col >> 3)) & (col <= row)
    causal01 = _bf(jnp.where(causal, 1.0, 0.0))
    erow = lax.broadcasted_iota(jnp.int32, (LANES, C_WIDTH), 0)
    ecol = lax.broadcasted_iota(jnp.int32, (LANES, C_WIDTH), 1)
    expand01 = _bf(jnp.where((ecol >> 6) == erow, 1.0, 0.0))

    def expand(a):
        a1 = _bf(a)
        a2 = _bf(a - a1.astype(F32))
        return _dot(a1, expand01) + _dot(a2, expand01)

    a_neg = -jnp.exp(alog_ref[...])
    if prompt:
        arow = lax.broadcasted_iota(jnp.int32, (CHUNK, A_HEADS * CHUNK), 0)
        acol = lax.broadcasted_iota(jnp.int32, (CHUNK, A_HEADS * CHUNK), 1)
        wa_b = _bf(jnp.where((acol & (CHUNK - 1)) <= arow, wa_ref[...], 0.0))
        prev_b = tailb_ref[...]
        prev_c = [tailc_ref[:, q * CONV_C_PART_W:(q + 1) * CONV_C_PART_W] for q in range(CONV_C_PARTS)]
        mix_refs = (mixa_ref, mixb_ref)

    fills = []
    points_left = [0]

    def proj_cols(dst, k):
        cs = slice(k * FILL_COLS, min((k + 1) * FILL_COLS, D_IN_PAD))
        dst[:, cs] = _dot(hn_ref[...], win_ref[:, cs])

    def out_cols(mref, rs, k):
        cs = slice(k * OUT_COLS, (k + 1) * OUT_COLS)
        xo_ref[rs, cs] = x_ref[rs, cs] + _dot(mref[...], wout_ref[:, cs])

    def fill():
        if not prompt:
            return
        n = -(-len(fills) // max(points_left[0], 1))
        points_left[0] -= 1
        for _ in range(min(n, len(fills))):
            fills.pop(0)()

    for c in range(n_sub):
        sl = slice(c * CHUNK, (c + 1) * CHUNK)
        x = x_ref[sl, :]
        if prompt:
            if c % subs_per_half == 0:
                nxt = x_ref[half:rows, :] if c == 0 else xn_ref[...]
                dst = projb_ref if c == 0 else proja_ref
                hn_ref[...] = _bf(_rmsnorm(nxt, g1_ref[...]))
                fills.extend(functools.partial(proj_cols, dst, k) for k in range(n_fill))
                points_left[0] = FILL_POINTS * subs_per_half
            pref = proja_ref if c < subs_per_half else projb_ref
            lr = slice((c % subs_per_half) * CHUNK, (c % subs_per_half + 1) * CHUNK)
            P = lambda c0, w, pref=pref, lr=lr: pref[lr, c0:c0 + w]
        else:
            proj = in_proj(x)
            P = lambda c0, w, proj=proj: proj[:, c0:c0 + w]

        u = _gelu(P(COL_U, A_WIDTH))
        v = _gelu(P(COL_V, A_WIDTH))
        fill()
        if prompt:
            pairs = []
            for p in range(A_HEADS // 2):
                rhs = _block_diag_pair(v[:, p * LANES:(p + 1) * LANES])
                pairs.append(_dot(wa_b[:, p * 2 * CHUNK:(p + 1) * 2 * CHUNK], rhs))
            s = jnp.concatenate(pairs, axis=1) + ba_ref[...]
            if c == n_sub - 1:
                vrow_ref[...] = v
        else:
            s0 = c * seqs_per_sub
            v3 = v.reshape(seqs_per_sub, DEC_SEQ, A_WIDTH)
            trow = lax.broadcasted_iota(jnp.int32, (DEC_SEQ, A_WIDTH), 0)
            s3 = jnp.zeros((seqs_per_sub, DEC_SEQ, A_WIDTH), F32)
            for jj in range(DEC_SEQ):
                coef = jnp.where(trow >= jj, wa_ref[jj], 0.0)
                s3 = s3 + coef[None] * v3[:, jj:jj + 1, :]
            s = (s3 + ba_ref[0:DEC_SEQ, :][None]).reshape(CHUNK, A_WIDTH)
            vrow_ref[s0:s0 + seqs_per_sub] = v3
        ya = u * s
        fill()

        cg_in = P(COL_CG, B_WIDTH) * P(COL_HB, B_WIDTH)
        if prompt:
            conv = _causal_taps_rows(cg_in, prev_b, cw_ref[...])
            prev_b = cg_in[CHUNK - SUBLANES:, :]
        else:
            cbufb_ref[s0:s0 + seqs_per_sub, SUBLANES:, :] = cg_in.reshape(seqs_per_sub, DEC_SEQ, B_WIDTH)
            conv = _causal_taps_seqs(cbufb_ref, s0, seqs_per_sub, cw_ref, CONV_B, B_WIDTH)
        yb = P(COL_BG, B_WIDTH) * conv
        fill()

        if prompt:
            parts = []
            for q in range(CONV_C_PARTS):
                qs = slice(q * CONV_C_PART_W, (q + 1) * CONV_C_PART_W)
                xin = P(COL_XBC + q * CONV_C_PART_W, CONV_C_PART_W)
                parts.append(_silu(_causal_taps_rows(xin, prev_c[q], scw_ref[:, qs]) + scb_ref[:, qs]))
                prev_c[q] = xin[CHUNK - SUBLANES:, :]
                fill()
            xbc = jnp.concatenate(parts, axis=1)
        else:
            xbc_in = P(COL_XBC, SSM_CONV_DIM)
            cbufc_ref[s0:s0 + seqs_per_sub, SUBLANES:, :] = xbc_in.reshape(seqs_per_sub, DEC_SEQ, SSM_CONV_DIM)
            xbc = _causal_taps_seqs(cbufc_ref, s0, seqs_per_sub, scw_ref, CONV_C, SSM_CONV_DIM)
            xbc = _silu(xbc + scb_ref[...])

        xs = xbc[:, 0:C_WIDTH]
        bm = xbc[:, C_WIDTH:C_WIDTH + SSM_GROUPS * D_STATE]
        cm = xbc[:, C_WIDTH + SSM_GROUPS * D_STATE:]
        dt = _softplus(P(COL_DT, LANES) + dtb_ref[...])
        cum = _cumsum_rows(causal01, dt * a_neg)
        if prompt:
            tot = cum[CHUNK - 1:CHUNK, :]
        else:
            cum3 = cum.reshape(seqs_per_sub, DEC_SEQ, LANES)
            tot = jnp.broadcast_to(cum3[:, DEC_SEQ - 1:DEC_SEQ, :], cum3.shape).reshape(CHUNK, LANES)
        expcum = jnp.exp(cum)
        w_state = jnp.exp(tot - cum) * dt
        ex = expand(jnp.concatenate([w_state, expcum], axis=0))
        w_state_x, expcum_x = ex[0:CHUNK], ex[CHUNK:]
        cum_t = cum.T
        dt_t = dt.T
        xw = xs * w_state_x
        bm_b = _bf(bm)
        cm_b = _bf(cm)
        fill()

        zero_blk = jnp.zeros((CHUNK, D_STATE), BF16)
        b_diag = jnp.concatenate([jnp.concatenate([bm_b[:, 0:D_STATE], zero_blk], axis=1),
                                  jnp.concatenate([zero_blk, bm_b[:, D_STATE:]], axis=1)], axis=0)
        cb_all = _dot_nt(cm_b, b_diag)
        y_pairs = []
        for g in range(SSM_GROUPS):
            gs = slice(g * D_STATE, (g + 1) * D_STATE)
            cb = cb_all[:, gs]
            wts = []
            for kk in range(HEADS_PER_GROUP):
                k = g * HEADS_PER_GROUP + kk
                seg = cum[:, k:k + 1] - cum_t[k:k + 1, :]
                decay = jnp.exp(jnp.where(causal, seg, -jnp.inf))
                wts.append(_bf(cb * decay * dt_t[k:k + 1, :]))
            for p in range(HEADS_PER_GROUP // 2):
                k0 = g * HEADS_PER_GROUP + 2 * p
                lhs = jnp.concatenate([wts[2 * p], wts[2 * p + 1]], axis=1)
                rhs = _block_diag_pair(xs[:, k0 * SSM_HEAD_DIM:(k0 + 2) * SSM_HEAD_DIM])
                y_pairs.append(_dot(lhs, rhs))
                fill()
        y = jnp.concatenate(y_pairs, axis=1)

        if prompt:
            xw_b = _bf(xw)
            y_off = []
            for g in range(SSM_GROUPS):
                gs = slice(g * D_STATE, (g + 1) * D_STATE)
                gw = slice(g * GROUP_W, (g + 1) * GROUP_W)
                st_g = st_ref[:, gw]
                y_off.append(_dot(cm_b[:, gs], _bf(st_g)))
                st_ref[:, gw] = expcum_x[CHUNK - 1:CHUNK, gw] * st_g + _dot(_bf(bm[:, gs].T), xw_b[:, gw])
            y = y + jnp.concatenate(y_off, axis=1) * expcum_x
        else:
            exptot = jnp.exp(tot)
            xw_t = [_bf(xw[:, g * GROUP_W:(g + 1) * GROUP_W].T) for g in range(SSM_GROUPS)]
            y_off = []
            for i in range(seqs_per_sub):
                rs = slice(i * DEC_SEQ, (i + 1) * DEC_SEQ)
                own_rows = (row >> 3) == i
                parts = []
                for g in range(SSM_GROUPS):
                    gs = slice(g * D_STATE, (g + 1) * D_STATE)
                    gw = slice(g * GROUP_W, (g + 1) * GROUP_W)
                    h0 = ssm0_ref[s0 + i, gw, :]
                    parts.append(_dot_nt(_bf(cm[rs, gs]), _bf(h0)))
                    upd = _dot(xw_t[g], _bf(jnp.where(own_rows, bm[:, gs], 0.0)))
                    for kk in range(HEADS_PER_GROUP):
                        k = g * HEADS_PER_GROUP + kk
                        hs = slice(kk * SSM_HEAD_DIM, (kk + 1) * SSM_HEAD_DIM)
                        dec = jnp.broadcast_to(exptot[i * DEC_SEQ:i * DEC_SEQ + 1, k:k + 1],
                                               (SSM_HEAD_DIM, D_STATE))
                        ssmo_ref[s0 + i, k * SSM_HEAD_DIM:(k + 1) * SSM_HEAD_DIM, :] = (
                            dec * h0[hs, :] + upd[hs, :])
                y_off.append(jnp.concatenate(parts, axis=1))
            y = y + jnp.concatenate(y_off, axis=0) * expcum_x
        fill()

        y = y + dsk_ref[...] * xs
        yc = _rmsnorm(y * _silu(P(COL_Z, C_WIDTH)), snorm_ref[...])
        fill()
        mix = jnp.concatenate([_bf(ya), _bf(yb), _bf(yc)], axis=1)
        if prompt:
            mref = mix_refs[c % 2]
            mref[...] = mix
            outs = [functools.partial(out_cols, mref, sl, k) for k in range(D_MODEL // OUT_COLS)]
            rest = list(fills)
            fills.clear()
            while outs or rest:
                if outs:
                    fills.append(outs.pop(0))
                if rest:
                    fills.append(rest.pop(0))
        else:
            xo_ref[sl, :] = x + _dot(mix, wout_ref[...])

    if prompt:
        while fills:
            fills.pop(0)()
        tailb_ref[...] = prev_b
        convo_ref[...] = prev_b[SUBLANES - (CONV_B - 1):, :]
        for q in range(CONV_C_PARTS):
            qs = slice(q * CONV_C_PART_W, (q + 1) * CONV_C_PART_W)
            tailc_ref[:, qs] = prev_c[q]
            sconvo_ref[:, qs] = prev_c[q][SUBLANES - (CONV_C - 1):, :]

        @pl.when(j == pl.num_programs(1) - 1)
        def _():
            ssmo_ref[...] = st_ref[...].T
    else:
        convo_ref[...] = cbufb_ref[:, 2 * SUBLANES - (CONV_B - 1):, :]
        sconvo_ref[...] = cbufc_ref[:, 2 * SUBLANES - (CONV_C - 1):, :]


def _ff_kernel(x_ref, g2_ref, w1_ref, w2_ref, gf_ref, o_ref, f_ref, *, final):
    x = x_ref[...]
    h = _bf(_rmsnorm(x, g2_ref[...]))
    for c in range(D_FF // FF_COLS):
        cs = slice(c * FF_COLS, (c + 1) * FF_COLS)
        f = jnp.maximum(_dot(h, w1_ref[:, cs]), 0.0)
        f_ref[:, cs] = _bf(f * f)
    y = x + _dot(f_ref[...], w2_ref[...])
    if final:
        y = _rmsnorm(y, gf_ref[...])
    o_ref[...] = y


def _layer_spec(shape, layer, n_grid):
    zeros = (0,) * len(shape)
    if n_grid == 1:
        index_map = lambda i: (layer,) + zeros
    else:
        index_map = lambda i, j: (layer,) + zeros
    return pl.BlockSpec((None,) + tuple(shape), index_map, pipeline_mode=pl.Buffered(1))


def _mix_call(prompt, layer, x2d, params, states=None, carry=None):
    n_rows = x2d.shape[0]
    if prompt:
        rows = PROMPT_ROWS
        seq = 2048
        n_batch = n_rows // seq
        steps = seq // rows
        grid = (n_batch, steps)
        ng = 2
        xmap = lambda b, j: (b * steps + j, 0)
        smap = lambda b, j: (layer, b, 0, 0)
        sblk = (None, None)
        state_rows = (CHUNK, CONV_B - 1, CONV_C - 1, C_WIDTH)
    else:
        rows = SAMPLE_SEQS * DEC_SEQ
        n_batch = n_rows // DEC_SEQ
        grid = (n_batch // SAMPLE_SEQS,)
        ng = 1
        xmap = lambda i: (i, 0)
        smap = lambda i: (layer, i, 0, 0)
        sblk = (None, SAMPLE_SEQS)
        state_rows = (DEC_SEQ, CONV_B - 1, CONV_C - 1, C_WIDTH)
    state_cols = (A_WIDTH, B_WIDTH, SSM_CONV_DIM, D_STATE)

    par_specs = [_layer_spec(p.shape[1:], layer, ng) for p in params]
    in_specs = [pl.BlockSpec((rows, D_MODEL), xmap)] + par_specs
    args = [x2d] + list(params)
    if prompt:
        half = rows // 2
        last_half = n_rows // half - 1
        nmap = lambda b, j: (jnp.minimum(2 * (b * steps + j) + 2, last_half), 0)
        in_specs.insert(1, pl.BlockSpec((half, D_MODEL), nmap))
        args.insert(1, x2d)
        scratch = [
            pltpu.VMEM((SUBLANES, B_WIDTH), F32),
            pltpu.VMEM((SUBLANES, SSM_CONV_DIM), F32),
            pltpu.VMEM((D_STATE, C_WIDTH), F32),
            pltpu.VMEM((half, D_IN_PAD), F32),
            pltpu.VMEM((half, D_IN_PAD), F32),
            pltpu.VMEM((half, D_MODEL), BF16),
            pltpu.VMEM((CHUNK, D_MODEL), BF16),
            pltpu.VMEM((CHUNK, D_MODEL), BF16),
        ]
        sem = ("arbitrary", "arbitrary")
    else:
        in_specs += [pl.BlockSpec(sblk + (r, w), smap)
                     for r, w in zip(state_rows[1:], state_cols[1:])]
        args += list(states)
        scratch = [
            pltpu.VMEM((SAMPLE_SEQS, 2 * SUBLANES, B_WIDTH), F32),
            pltpu.VMEM((SAMPLE_SEQS, 2 * SUBLANES, SSM_CONV_DIM), F32),
        ]
        sem = ("parallel",)

    out_shape = [jax.ShapeDtypeStruct((n_rows, D_MODEL), F32)]
    out_specs = [pl.BlockSpec((rows, D_MODEL), xmap)]
    for r, w in zip(state_rows, state_cols):
        out_shape.append(jax.ShapeDtypeStruct((DEPTH, n_batch, r, w), F32))
        out_specs.append(pl.BlockSpec(sblk + (r, w), smap))
    n_in = len(args)
    aliases = {}
    if carry is not None:
        in_specs += [pl.BlockSpec(memory_space=pl.ANY)] * len(carry)
        args += list(carry)
        aliases = {n_in + k: 1 + k for k in range(len(carry))}

    return pl.pallas_call(
        functools.partial(_mix_kernel, prompt=prompt, rows=rows, n_in=n_in, n_carry=len(aliases)),
        grid=grid,
        in_specs=in_specs,
        out_specs=out_specs,
        out_shape=out_shape,
        scratch_shapes=scratch,
        input_output_aliases=aliases,
        compiler_params=pltpu.CompilerParams(dimension_semantics=sem, vmem_limit_bytes=VMEM_LIMIT),
        name="mix_prompt" if prompt else "mix_sample",
    )(*args)


def _ff_call(layer, x2d, g2, w1, w2, gf, final):
    n_rows = x2d.shape[0]
    return pl.pallas_call(
        functools.partial(_ff_kernel, final=final),
        grid=(n_rows // FF_ROWS,),
        in_specs=[
            pl.BlockSpec((FF_ROWS, D_MODEL), lambda i: (i, 0)),
            _layer_spec(g2.shape[1:], layer, 1),
            _layer_spec(w1.shape[1:], layer, 1),
            _layer_spec(w2.shape[1:], layer, 1),
            pl.BlockSpec(gf.shape, lambda i: (0, 0), pipeline_mode=pl.Buffered(1)),
        ],
        out_specs=pl.BlockSpec((FF_ROWS, D_MODEL), lambda i: (i, 0)),
        out_shape=jax.ShapeDtypeStruct((n_rows, D_MODEL), F32),
        scratch_shapes=[pltpu.VMEM((FF_ROWS, D_FF), BF16)],
        compiler_params=pltpu.CompilerParams(dimension_semantics=("parallel",),
                                             vmem_limit_bytes=VMEM_LIMIT),
        name="ffn",
    )(x2d, g2, w1, w2, gf)


def kernel(x_prompt, x_sample, state_conv, state_ssm_conv, state_ssm, norm1, w_in, w_s, b_s, conv_w,
           ssm_conv_w, ssm_conv_b, dt_bias, a_log, d_skip, ssm_norm, w_out, norm2, w_ff1, w_ff2,
           final_norm):
    bp, seq, _ = x_prompt.shape
    bs, dseq, _ = x_sample.shape
    assert seq % PROMPT_ROWS == 0 and dseq == DEC_SEQ and bs % SAMPLE_SEQS == 0

    win_b = _bf(jnp.pad(w_in, ((0, 0), (0, 0), (0, D_IN_PAD - D_IN))))
    wout_b = _bf(w_out)
    w1_b = _bf(w_ff1)
    w2_b = _bf(w_ff2)
    g1 = norm1.reshape(DEPTH, 1, D_MODEL)
    g2 = norm2.reshape(DEPTH, 1, D_MODEL)
    gf = final_norm.reshape(1, D_MODEL)
    wa_prompt = jnp.transpose(w_s, (0, 2, 1, 3)).reshape(DEPTH, CHUNK, A_HEADS * CHUNK)
    wa_sample = jnp.repeat(jnp.transpose(w_s[:, :, :DEC_SEQ, :DEC_SEQ], (0, 3, 2, 1)),
                           A_WIDTH // A_HEADS, axis=-1)
    ba = jnp.repeat(jnp.transpose(b_s, (0, 2, 1)), A_WIDTH // A_HEADS, axis=-1)
    scb = ssm_conv_b.reshape(DEPTH, 1, SSM_CONV_DIM)
    pad_heads = lambda p: jnp.pad(p, ((0, 0), (0, LANES - SSM_HEADS))).reshape(DEPTH, 1, LANES)
    dtb = pad_heads(dt_bias)
    alog = pad_heads(a_log)
    dsk = jnp.repeat(d_skip, SSM_HEAD_DIM, axis=-1).reshape(DEPTH, 1, C_WIDTH)
    snorm = ssm_norm.reshape(DEPTH, 1, C_WIDTH)
    common = (g1, win_b, None, ba, conv_w, ssm_conv_w, scb, dtb, alog, dsk, snorm, wout_b)
    par_prompt = common[:2] + (wa_prompt,) + common[3:]
    par_sample = common[:2] + (wa_sample,) + common[3:]

    xp = x_prompt.reshape(bp * seq, D_MODEL)
    xs = x_sample.reshape(bs * dseq, D_MODEL)
    ssm_in = state_ssm.reshape(DEPTH, bs, C_WIDTH, D_STATE)
    st_p = st_s = None
    for l in range(DEPTH):
        final = l == DEPTH - 1
        xp, *st_p = _mix_call(True, l, xp, par_prompt, carry=st_p)
        xp = _ff_call(l, xp, g2, w1_b, w2_b, gf, final)
        xs, *st_s = _mix_call(False, l, xs, par_sample, (state_conv, state_ssm_conv, ssm_in), carry=st_s)
        xs = _ff_call(l, xs, g2, w1_b, w2_b, gf, final)

    y_prompt = xp.reshape(bp, seq, D_MODEL)
    y_sample = xs.reshape(bs, dseq, D_MODEL)
    ssm_shape = lambda b: (DEPTH, b, SSM_HEADS, SSM_HEAD_DIM, D_STATE)
    return (y_prompt, y_sample,
            st_p[0], st_p[1], st_p[2], st_p[3].reshape(ssm_shape(bp)),
            st_s[0], st_s[1], st_s[2], st_s[3].reshape(ssm_shape(bs)))
```

```python
import functools

import jax
import jax.numpy as jnp
from jax import lax
from jax.experimental import pallas as pl
from jax.experimental.pallas import tpu as pltpu

D_MODEL = 1024
DEPTH = 4
A_WIDTH = 256
A_HEADS = 4
B_WIDTH = 256
C_WIDTH = 512
CONV_B = 3
CONV_C = 4
SSM_HEADS = 8
SSM_HEAD_DIM = 64
SSM_GROUPS = 2
HEADS_PER_GROUP = SSM_HEADS // SSM_GROUPS
D_STATE = 128
SSM_CONV_DIM = 1024
CHUNK = 128
D_FF = 4096
EPS = 1e-5
DEC_SEQ = 8

COL_U, COL_V, COL_BG, COL_CG, COL_HB, COL_Z, COL_XBC, COL_DT = 0, 256, 512, 768, 1024, 1280, 1792, 2816
D_IN = 2824
LANES = 128
SUBLANES = 8
D_IN_PAD = 2944
GROUP_W = C_WIDTH // SSM_GROUPS

PROMPT_ROWS = 512
SAMPLE_SEQS = 16
FILL_COLS = 256
FILL_POINTS = 6
FF_ROWS = 512
FF_COLS = 1024
VMEM_LIMIT = 56 * 1024 * 1024

F32 = jnp.float32
BF16 = jnp.bfloat16


def _bf(x):
    return x.astype(BF16)


def _dot(a, b):
    return jnp.dot(a, b, preferred_element_type=F32)


def _dot_nt(a, b):
    return lax.dot_general(a, b, (((1,), (1,)), ((), ())), preferred_element_type=F32)


def _split3(a):
    a1 = _bf(a)
    r1 = a - a1.astype(F32)
    a2 = _bf(r1)
    r2 = r1 - a2.astype(F32)
    return a1, a2, _bf(r2)


def _sel_left(m01, a):
    a1, a2, a3 = _split3(a)
    return _dot(m01, a1) + _dot(m01, a2) + _dot(m01, a3)


def _rmsnorm(x, g):
    ms = jnp.mean(x * x, axis=-1, keepdims=True)
    return (x * lax.rsqrt(ms + EPS)) * g


def _gelu(x):
    return 0.5 * x * (1.0 + lax.erf(x * (0.5 ** 0.5)))


def _silu(x):
    return x * jax.nn.sigmoid(x)


def _softplus(x):
    return jnp.maximum(x, 0.0) + jnp.log1p(jnp.exp(-jnp.abs(x)))


def _block_diag_pair(t):
    lane = lax.broadcasted_iota(jnp.int32, t.shape, 1)
    lo = jnp.where(lane < SSM_HEAD_DIM, t, 0.0)
    hi = jnp.where(lane >= SSM_HEAD_DIM, t, 0.0)
    return _bf(jnp.concatenate([lo, hi], axis=0))


def _causal_taps_rows(xin, prev8, w_ref, n_taps):
    row8 = lax.broadcasted_iota(jnp.int32, prev8.shape, 0)
    out = None
    for k in range(n_taps):
        back = n_taps - 1 - k
        if back == 0:
            sh = xin
        else:
            r = pltpu.roll(xin, back, axis=0)
            first = jnp.where(row8 >= back, r[0:SUBLANES], pltpu.roll(prev8, back, axis=0))
            sh = jnp.concatenate([first, r[SUBLANES:]], axis=0)
        term = sh * w_ref[k:k + 1, :]
        out = term if out is None else out + term
    return out


def _causal_taps_seqs(buf_ref, s0, n_seq, w_ref, n_taps, width):
    out = None
    for k in range(n_taps):
        back = n_taps - 1 - k
        sh = buf_ref[s0:s0 + n_seq, SUBLANES - back:2 * SUBLANES - back, :].reshape(n_seq * DEC_SEQ, width)
        term = sh * w_ref[k:k + 1, :]
        out = term if out is None else out + term
    return out


def _advance(gen):
    try:
        next(gen)
        return True
    except StopIteration:
        return False


def _mix_kernel(*refs, prompt, rows, n_in, n_carry):
    refs = refs[:n_in] + refs[n_in + n_carry:]
    if prompt:
        (x_ref, xn_ref, g1_ref, win_ref, wa_ref, ba_ref, cw_ref, scw_ref, scb_ref, dtb_ref, alog_ref,
         dsk_ref, snorm_ref, wout_ref,
         xo_ref, vrow_ref, convo_ref, sconvo_ref, ssmo_ref,
         tailb_ref, tailc_ref, st_ref, proja_ref, projb_ref, hn_ref, mixa_ref, mixb_ref) = refs
    else:
        (x_ref, g1_ref, win_ref, wa_ref, ba_ref, cw_ref, scw_ref, scb_ref, dtb_ref, alog_ref,
         dsk_ref, snorm_ref, wout_ref, conv0_ref, sconv0_ref, ssm0_ref,
         xo_ref, vrow_ref, convo_ref, sconvo_ref, ssmo_ref,
         cbufb_ref, cbufc_ref) = refs
    n_sub = rows // CHUNK
    seqs_per_sub = CHUNK // DEC_SEQ

    def in_proj(x_rows):
        return _dot(_bf(_rmsnorm(x_rows, g1_ref[...])), win_ref[...])

    if prompt:
        half = rows // 2
        subs_per_half = n_sub // 2
        n_fill = -(-D_IN_PAD // FILL_COLS)
        j = pl.program_id(1)

        @pl.when(j == 0)
        def _():
            tailb_ref[...] = jnp.zeros(tailb_ref.shape, F32)
            tailc_ref[...] = jnp.zeros(tailc_ref.shape, F32)
            st_ref[...] = jnp.zeros(st_ref.shape, F32)

        @pl.when((pl.program_id(0) == 0) & (j == 0))
        def _():
            proja_ref[...] = in_proj(x_ref[0:half, :])
    else:
        cbufb_ref[:, SUBLANES - (CONV_B - 1):SUBLANES, :] = conv0_ref[...]
        cbufc_ref[:, SUBLANES - (CONV_C - 1):SUBLANES, :] = sconv0_ref[...]

    row = lax.broadcasted_iota(jnp.int32, (CHUNK, CHUNK), 0)
    col = lax.broadcasted_iota(jnp.int32, (CHUNK, CHUNK), 1)
    if prompt:
        causal = col <= row
    else:
        causal = ((row >> 3) == (col >> 3)) & (col <= row)
    causal01 = _bf(jnp.where(causal, 1.0, 0.0))
    erow = lax.broadcasted_iota(jnp.int32, (LANES, C_WIDTH), 0)
    ecol = lax.broadcasted_iota(jnp.int32, (LANES, C_WIDTH), 1)
    expand01 = _bf(jnp.where((ecol >> 6) == erow, 1.0, 0.0))

    def expand(a):
        a1 = _bf(a)
        a2 = _bf(a - a1.astype(F32))
        return _dot(a1, expand01) + _dot(a2, expand01)

    a_neg = -jnp.exp(alog_ref[...])
    if prompt:
        upper01 = _bf(jnp.where(row <= col, 1.0, 0.0))
        xrow = lax.broadcasted_iota(jnp.int32, (LANES, 2 * C_WIDTH), 0)
        xcol = lax.broadcasted_iota(jnp.int32, (LANES, 2 * C_WIDTH), 1)
        expand2 = _bf(jnp.where((xrow < 4 * SSM_HEADS) & ((xcol >> 6) == (xrow & (2 * SSM_HEADS - 1))),
                                1.0, 0.0))
        arow = lax.broadcasted_iota(jnp.int32, (CHUNK, A_HEADS * CHUNK), 0)
        acol = lax.broadcasted_iota(jnp.int32, (CHUNK, A_HEADS * CHUNK), 1)
        wa_b = _bf(jnp.where((acol & (CHUNK - 1)) <= arow, wa_ref[...], 0.0))
        hist = {"b": tailb_ref[...], "c": tailc_ref[...]}
        mix_refs = (mixa_ref, mixb_ref)

    fills = []
    points_left = [0]

    def fill_cols(dst, k):
        cs = slice(k * FILL_COLS, min((k + 1) * FILL_COLS, D_IN_PAD))
        dst[:, cs] = _dot(hn_ref[...], win_ref[:, cs])

    def fill():
        n = -(-len(fills) // max(points_left[0], 1))
        points_left[0] -= 1
        for _ in range(min(n, len(fills))):
            fills.pop(0)()

    def sub_block(c):
        sl = slice(c * CHUNK, (c + 1) * CHUNK)
        x = x_ref[sl, :]
        if prompt:
            if c % subs_per_half == 0:
                nxt = x_ref[half:rows, :] if c == 0 else xn_ref[...]
                dst = projb_ref if c == 0 else proja_ref
                hn_ref[...] = _bf(_rmsnorm(nxt, g1_ref[...]))
                fills.extend(functools.partial(fill_cols, dst, k) for k in range(n_fill))
            pref = proja_ref if c < subs_per_half else projb_ref
            lr = slice((c % subs_per_half) * CHUNK, (c % subs_per_half + 1) * CHUNK)
            P = lambda c0, w, pref=pref, lr=lr: pref[lr, c0:c0 + w]
        else:
            proj = in_proj(x)
            P = lambda c0, w, proj=proj: proj[:, c0:c0 + w]

        if prompt:
            dt_t = _softplus(P(COL_DT, LANES).T[0:SSM_HEADS, :] + dtb_ref[...])
            a = dt_t * a_neg
            a1 = _bf(a).astype(F32)
            a2 = _bf(a - a1).astype(F32)
            r = _dot(_bf(jnp.concatenate([a1, a2, a - a1 - a2, jnp.zeros_like(a)], axis=0)), upper01)
            cum_t = r[0:8] + r[8:16] + r[16:24]
            expcum_t = jnp.exp(cum_t)
            wst_t = jnp.exp(cum_t[:, CHUNK - 1:CHUNK] - cum_t) * dt_t
            w_hi = _bf(wst_t).astype(F32)
            e_hi = _bf(expcum_t).astype(F32)
            pack = jnp.concatenate([w_hi, e_hi, wst_t - w_hi, expcum_t - e_hi, cum_t,
                                    jnp.zeros((CHUNK - 5 * SSM_HEADS, CHUNK), F32)], axis=0)
            cols = pack.T
            ex = _dot(_bf(cols), expand2)
            w_state_x, expcum_x = ex[:, 0:C_WIDTH], ex[:, C_WIDTH:]
            cum, cum_lane0 = cols, 4 * SSM_HEADS
        else:
            cum_lane0 = 0
            dt = _softplus(P(COL_DT, LANES) + dtb_ref[...])
            cum = _sel_left(causal01, dt * a_neg)
            cum3 = cum.reshape(seqs_per_sub, DEC_SEQ, LANES)
            tot = jnp.broadcast_to(cum3[:, DEC_SEQ - 1:DEC_SEQ, :], cum3.shape).reshape(CHUNK, LANES)
            expcum = jnp.exp(cum)
            w_state = jnp.exp(tot - cum) * dt
            ex = expand(jnp.concatenate([w_state, expcum], axis=0))
            w_state_x, expcum_x = ex[0:CHUNK], ex[CHUNK:]
            cum_t = cum.T
            dt_t = dt.T

        u = _gelu(P(COL_U, A_WIDTH))
        v = _gelu(P(COL_V, A_WIDTH))
        if prompt:
            pairs = []
            for p in range(A_HEADS // 2):
                rhs = _block_diag_pair(v[:, p * LANES:(p + 1) * LANES])
                pairs.append(_dot(wa_b[:, p * 2 * CHUNK:(p + 1) * 2 * CHUNK], rhs))
            s = jnp.concatenate(pairs, axis=1) + ba_ref[...]
            if c == n_sub - 1:
                vrow_ref[...] = v
        else:
            s0 = c * seqs_per_sub
            v3 = v.reshape(seqs_per_sub, DEC_SEQ, A_WIDTH)
            trow = lax.broadcasted_iota(jnp.int32, (DEC_SEQ, A_WIDTH), 0)
            s3 = jnp.zeros((seqs_per_sub, DEC_SEQ, A_WIDTH), F32)
            for jj in range(DEC_SEQ):
                coef = jnp.where(trow >= jj, wa_ref[jj], 0.0)
                s3 = s3 + coef[None] * v3[:, jj:jj + 1, :]
            s = (s3 + ba_ref[0:DEC_SEQ, :][None]).reshape(CHUNK, A_WIDTH)
            vrow_ref[s0:s0 + seqs_per_sub] = v3
        ya = u * s
        fill()
        yield

        cg_in = P(COL_CG, B_WIDTH) * P(COL_HB, B_WIDTH)
        xbc_in = P(COL_XBC, SSM_CONV_DIM)
        if prompt:
            conv = _causal_taps_rows(cg_in, hist["b"], cw_ref, CONV_B)
            xbc = _causal_taps_rows(xbc_in, hist["c"], scw_ref, CONV_C)
            hist["b"] = cg_in[CHUNK - SUBLANES:, :]
            hist["c"] = xbc_in[CHUNK - SUBLANES:, :]
        else:
            cbufb_ref[s0:s0 + seqs_per_sub, SUBLANES:, :] = cg_in.reshape(seqs_per_sub, DEC_SEQ, B_WIDTH)
            cbufc_ref[s0:s0 + seqs_per_sub, SUBLANES:, :] = xbc_in.reshape(seqs_per_sub, DEC_SEQ, SSM_CONV_DIM)
            conv = _causal_taps_seqs(cbufb_ref, s0, seqs_per_sub, cw_ref, CONV_B, B_WIDTH)
            xbc = _causal_taps_seqs(cbufc_ref, s0, seqs_per_sub, scw_ref, CONV_C, SSM_CONV_DIM)
        yb = P(COL_BG, B_WIDTH) * conv
        fill()
        yield

        xbc = _silu(xbc + scb_ref[...])
        xs = xbc[:, 0:C_WIDTH]
        bm = xbc[:, C_WIDTH:C_WIDTH + SSM_GROUPS * D_STATE]
        cm = xbc[:, C_WIDTH + SSM_GROUPS * D_STATE:]
        xw = xs * w_state_x
        bm_b = _bf(bm)
        cm_b = _bf(cm)
        fill()
        yield

        y_pairs = []
        for g in range(SSM_GROUPS):
            gs = slice(g * D_STATE, (g + 1) * D_STATE)
            cb = _dot_nt(cm_b[:, gs], bm_b[:, gs])
            wts = []
            for kk in range(HEADS_PER_GROUP):
                k = g * HEADS_PER_GROUP + kk
                seg = cum[:, cum_lane0 + k:cum_lane0 + k + 1] - cum_t[k:k + 1, :]
                decay = jnp.exp(jnp.where(causal, seg, -jnp.inf))
                wts.append(_bf(cb * decay * dt_t[k:k + 1, :]))
            for p in range(HEADS_PER_GROUP // 2):
                k0 = g * HEADS_PER_GROUP + 2 * p
                lhs = jnp.concatenate([wts[2 * p], wts[2 * p + 1]], axis=1)
                rhs = _block_diag_pair(xs[:, k0 * SSM_HEAD_DIM:(k0 + 2) * SSM_HEAD_DIM])
                y_pairs.append(_dot(lhs, rhs))
            fill()
            yield
        y = jnp.concatenate(y_pairs, axis=1)

        if prompt:
            xw_b = _bf(xw)
            y_off = []
            for g in range(SSM_GROUPS):
                gs = slice(g * D_STATE, (g + 1) * D_STATE)
                gw = slice(g * GROUP_W, (g + 1) * GROUP_W)
                st_g = st_ref[:, gw]
                y_off.append(_dot(cm_b[:, gs], _bf(st_g)))
                st_ref[:, gw] = expcum_x[CHUNK - 1:CHUNK, gw] * st_g + _dot(_bf(bm[:, gs].T), xw_b[:, gw])
            y = y + jnp.concatenate(y_off, axis=1) * expcum_x
        else:
            exptot = jnp.exp(tot)
            xw_t = [_bf(xw[:, g * GROUP_W:(g + 1) * GROUP_W].T) for g in range(SSM_GROUPS)]
            y_off = []
            for i in range(seqs_per_sub):
                rs = slice(i * DEC_SEQ, (i + 1) * DEC_SEQ)
                own_rows = (row >> 3) == i
                parts = []
                for g in range(SSM_GROUPS):
                    gs = slice(g * D_STATE, (g + 1) * D_STATE)
                    gw = slice(g * GROUP_W, (g + 1) * GROUP_W)
                    h0 = ssm0_ref[s0 + i, gw, :]
                    parts.append(_dot_nt(_bf(cm[rs, gs]), _bf(h0)))
                    upd = _dot(xw_t[g], _bf(jnp.where(own_rows, bm[:, gs], 0.0)))
                    for kk in range(HEADS_PER_GROUP):
                        k = g * HEADS_PER_GROUP + kk
                        hs = slice(kk * SSM_HEAD_DIM, (kk + 1) * SSM_HEAD_DIM)
                        dec = jnp.broadcast_to(exptot[i * DEC_SEQ:i * DEC_SEQ + 1, k:k + 1],
                                               (SSM_HEAD_DIM, D_STATE))
                        ssmo_ref[s0 + i, k * SSM_HEAD_DIM:(k + 1) * SSM_HEAD_DIM, :] = (
                            dec * h0[hs, :] + upd[hs, :])
                y_off.append(jnp.concatenate(parts, axis=1))
            y = y + jnp.concatenate(y_off, axis=0) * expcum_x

        y = y + dsk_ref[...] * xs
        yc = _rmsnorm(y * _silu(P(COL_Z, C_WIDTH)), snorm_ref[...])
        fill()
        yield
        mix = jnp.concatenate([_bf(ya), _bf(yb), _bf(yc)], axis=1)
        if prompt:
            mix_refs[c // subs_per_half][lr, :] = mix
        else:
            xo_ref[sl, :] = x + _dot(mix, wout_ref[...])

    def out_cols(h, k):
        rs = slice(h * half, (h + 1) * half)
        cs = slice(k * FILL_COLS, (k + 1) * FILL_COLS)
        xo_ref[rs, cs] = x_ref[rs, cs] + _dot(mix_refs[h][...], wout_ref[:, cs])

    group = subs_per_half if prompt else 1
    for c0 in range(0, n_sub, group):
        if prompt:
            points_left[0] = FILL_POINTS * group
        gens = [sub_block(c) for c in range(c0, c0 + group)]
        while gens:
            gens = [g for g in gens if _advance(g)]
        while fills:
            fills.pop(0)()
        if prompt:
            outs = [functools.partial(out_cols, c0 // group, k) for k in range(D_MODEL // FILL_COLS)]
            if c0 + group < n_sub:
                fills.extend(outs)
            else:
                for task in outs:
                    task()

    if prompt:
        tailb_ref[...] = hist["b"]
        tailc_ref[...] = hist["c"]
        convo_ref[...] = hist["b"][SUBLANES - (CONV_B - 1):, :]
        sconvo_ref[...] = hist["c"][SUBLANES - (CONV_C - 1):, :]

        @pl.when(j == pl.num_programs(1) - 1)
        def _():
            ssmo_ref[...] = st_ref[...].T
    else:
        convo_ref[...] = cbufb_ref[:, 2 * SUBLANES - (CONV_B - 1):, :]
        sconvo_ref[...] = cbufc_ref[:, 2 * SUBLANES - (CONV_C - 1):, :]


def _ff_kernel(x_ref, g2_ref, w1_ref, w2_ref, gf_ref, o_ref, f_ref, *, final):
    x = x_ref[...]
    h = _bf(_rmsnorm(x, g2_ref[...]))
    for c in range(D_FF // FF_COLS):
        cs = slice(c * FF_COLS, (c + 1) * FF_COLS)
        f = jnp.maximum(_dot(h, w1_ref[:, cs]), 0.0)
        f_ref[:, cs] = _bf(f * f)
    y = x + _dot(f_ref[...], w2_ref[...])
    if final:
        y = _rmsnorm(y, gf_ref[...])
    o_ref[...] = y


def _layer_spec(shape, layer, n_grid):
    zeros = (0,) * len(shape)
    if n_grid == 1:
        index_map = lambda i: (layer,) + zeros
    else:
        index_map = lambda i, j: (layer,) + zeros
    return pl.BlockSpec((None,) + tuple(shape), index_map, pipeline_mode=pl.Buffered(1))


def _mix_call(prompt, layer, x2d, params, states=None, carry=None):
    n_rows = x2d.shape[0]
    if prompt:
        rows = PROMPT_ROWS
        seq = 2048
        n_batch = n_rows // seq
        steps = seq // rows
        grid = (n_batch, steps)
        ng = 2
        xmap = lambda b, j: (b * steps + j, 0)
        smap = lambda b, j: (layer, b, 0, 0)
        sblk = (None, None)
        state_rows = (CHUNK, CONV_B - 1, CONV_C - 1, C_WIDTH)
    else:
        rows = SAMPLE_SEQS * DEC_SEQ
        n_batch = n_rows // DEC_SEQ
        grid = (n_batch // SAMPLE_SEQS,)
        ng = 1
        xmap = lambda i: (i, 0)
        smap = lambda i: (layer, i, 0, 0)
        sblk = (None, SAMPLE_SEQS)
        state_rows = (DEC_SEQ, CONV_B - 1, CONV_C - 1, C_WIDTH)
    state_cols = (A_WIDTH, B_WIDTH, SSM_CONV_DIM, D_STATE)

    par_specs = [_layer_spec(p.shape[1:], layer, ng) for p in params]
    in_specs = [pl.BlockSpec((rows, D_MODEL), xmap)] + par_specs
    args = [x2d] + list(params)
    if prompt:
        half = rows // 2
        last_half = n_rows // half - 1
        nmap = lambda b, j: (jnp.minimum(2 * (b * steps + j) + 2, last_half), 0)
        in_specs.insert(1, pl.BlockSpec((half, D_MODEL), nmap))
        args.insert(1, x2d)
        scratch = [
            pltpu.VMEM((SUBLANES, B_WIDTH), F32),
            pltpu.VMEM((SUBLANES, SSM_CONV_DIM), F32),
            pltpu.VMEM((D_STATE, C_WIDTH), F32),
            pltpu.VMEM((half, D_IN_PAD), F32),
            pltpu.VMEM((half, D_IN_PAD), F32),
            pltpu.VMEM((half, D_MODEL), BF16),
            pltpu.VMEM((half, D_MODEL), BF16),
            pltpu.VMEM((half, D_MODEL), BF16),
        ]
        sem = ("arbitrary", "arbitrary")
    else:
        in_specs += [pl.BlockSpec(sblk + (r, w), smap)
                     for r, w in zip(state_rows[1:], state_cols[1:])]
        args += list(states)
        scratch = [
            pltpu.VMEM((SAMPLE_SEQS, 2 * SUBLANES, B_WIDTH), F32),
            pltpu.VMEM((SAMPLE_SEQS, 2 * SUBLANES, SSM_CONV_DIM), F32),
        ]
        sem = ("parallel",)

    out_shape = [jax.ShapeDtypeStruct((n_rows, D_MODEL), F32)]
    out_specs = [pl.BlockSpec((rows, D_MODEL), xmap)]
    for r, w in zip(state_rows, state_cols):
        out_shape.append(jax.ShapeDtypeStruct((DEPTH, n_batch, r, w), F32))
        out_specs.append(pl.BlockSpec(sblk + (r, w), smap))
    n_in = len(args)
    aliases = {}
    if carry is not None:
        in_specs += [pl.BlockSpec(memory_space=pl.ANY)] * len(carry)
        args += list(carry)
        aliases = {n_in + k: 1 + k for k in range(len(carry))}

    return pl.pallas_call(
        functools.partial(_mix_kernel, prompt=prompt, rows=rows, n_in=n_in, n_carry=len(aliases)),
        grid=grid,
        in_specs=in_specs,
        out_specs=out_specs,
        out_shape=out_shape,
        scratch_shapes=scratch,
        input_output_aliases=aliases,
        compiler_params=pltpu.CompilerParams(dimension_semantics=sem, vmem_limit_bytes=VMEM_LIMIT),
        name="mix_prompt" if prompt else "mix_sample",
    )(*args)


def _ff_call(layer, x2d, g2, w1, w2, gf, final):
    n_rows = x2d.shape[0]
    return pl.pallas_call(
        functools.partial(_ff_kernel, final=final),
        grid=(n_rows // FF_ROWS,),
        in_specs=[
            pl.BlockSpec((FF_ROWS, D_MODEL), lambda i: (i, 0)),
            _layer_spec(g2.shape[1:], layer, 1),
            _layer_spec(w1.shape[1:], layer, 1),
            _layer_spec(w2.shape[1:], layer, 1),
            pl.BlockSpec(gf.shape, lambda i: (0, 0), pipeline_mode=pl.Buffered(1)),
        ],
        out_specs=pl.BlockSpec((FF_ROWS, D_MODEL), lambda i: (i, 0)),
        out_shape=jax.ShapeDtypeStruct((n_rows, D_MODEL), F32),
        scratch_shapes=[pltpu.VMEM((FF_ROWS, D_FF), BF16)],
        compiler_params=pltpu.CompilerParams(dimension_semantics=("parallel",),
                                             vmem_limit_bytes=VMEM_LIMIT),
        name="ffn",
    )(x2d, g2, w1, w2, gf)


def kernel(x_prompt, x_sample, state_conv, state_ssm_conv, state_ssm, norm1, w_in, w_s, b_s, conv_w,
           ssm_conv_w, ssm_conv_b, dt_bias, a_log, d_skip, ssm_norm, w_out, norm2, w_ff1, w_ff2,
           final_norm):
    bp, seq, _ = x_prompt.shape
    bs, dseq, _ = x_sample.shape
    assert seq % PROMPT_ROWS == 0 and dseq == DEC_SEQ and bs % SAMPLE_SEQS == 0

    win_b = _bf(jnp.pad(w_in, ((0, 0), (0, 0), (0, D_IN_PAD - D_IN))))
    wout_b = _bf(w_out)
    w1_b = _bf(w_ff1)
    w2_b = _bf(w_ff2)
    g1 = norm1.reshape(DEPTH, 1, D_MODEL)
    g2 = norm2.reshape(DEPTH, 1, D_MODEL)
    gf = final_norm.reshape(1, D_MODEL)
    wa_prompt = jnp.transpose(w_s, (0, 2, 1, 3)).reshape(DEPTH, CHUNK, A_HEADS * CHUNK)
    wa_sample = jnp.repeat(jnp.transpose(w_s[:, :, :DEC_SEQ, :DEC_SEQ], (0, 3, 2, 1)),
                           A_WIDTH // A_HEADS, axis=-1)
    ba = jnp.repeat(jnp.transpose(b_s, (0, 2, 1)), A_WIDTH // A_HEADS, axis=-1)
    scb = ssm_conv_b.reshape(DEPTH, 1, SSM_CONV_DIM)
    pad_heads = lambda p: jnp.pad(p, ((0, 0), (0, LANES - SSM_HEADS))).reshape(DEPTH, 1, LANES)
    dtb = pad_heads(dt_bias)
    alog = pad_heads(a_log)
    dsk = jnp.repeat(d_skip, SSM_HEAD_DIM, axis=-1).reshape(DEPTH, 1, C_WIDTH)
    snorm = ssm_norm.reshape(DEPTH, 1, C_WIDTH)
    over_time = lambda p: jnp.broadcast_to(p[:, :, None], (DEPTH, SSM_HEADS, CHUNK))
    par_prompt = (g1, win_b, wa_prompt, ba, conv_w, ssm_conv_w, scb, over_time(dt_bias), over_time(a_log),
                  dsk, snorm, wout_b)
    par_sample = (g1, win_b, wa_sample, ba, conv_w, ssm_conv_w, scb, dtb, alog, dsk, snorm, wout_b)

    xp = x_prompt.reshape(bp * seq, D_MODEL)
    xs = x_sample.reshape(bs * dseq, D_MODEL)
    ssm_in = state_ssm.reshape(DEPTH, bs, C_WIDTH, D_STATE)
    st_p = st_s = None
    for l in range(DEPTH):
        final = l == DEPTH - 1
        xp, *st_p = _mix_call(True, l, xp, par_prompt, carry=st_p)
        xp = _ff_call(l, xp, g2, w1_b, w2_b, gf, final)
        xs, *st_s = _mix_call(False, l, xs, par_sample, (state_conv, state_ssm_conv, ssm_in), carry=st_s)
        xs = _ff_call(l, xs, g2, w1_b, w2_b, gf, final)

    y_prompt = xp.reshape(bp, seq, D_MODEL)
    y_sample = xs.reshape(bs, dseq, D_MODEL)
    ssm_shape = lambda b: (DEPTH, b, SSM_HEADS, SSM_HEAD_DIM, D_STATE)
    return (y_prompt, y_sample,
            st_p[0], st_p[1], st_p[2], st_p[3].reshape(ssm_shape(bp)),
            st_s[0], st_s[1], st_s[2], st_s[3].reshape(ssm_shape(bs)))
```

```python
import functools

import jax
import jax.numpy as jnp
from jax import lax
from jax.experimental import pallas as pl
from jax.experimental.pallas import tpu as pltpu

D_MODEL = 1024
DEPTH = 4
A_WIDTH = 256
A_HEADS = 4
B_WIDTH = 256
C_WIDTH = 512
CONV_B = 3
CONV_C = 4
SSM_HEADS = 8
SSM_HEAD_DIM = 64
SSM_GROUPS = 2
HEADS_PER_GROUP = SSM_HEADS // SSM_GROUPS
D_STATE = 128
SSM_CONV_DIM = 1024
CHUNK = 128
D_FF = 4096
EPS = 1e-5
DEC_SEQ = 8

COL_U, COL_V, COL_BG, COL_CG, COL_HB, COL_Z, COL_XBC, COL_DT = 0, 256, 512, 768, 1024, 1280, 1792, 2816
D_IN = 2824
LANES = 128
SUBLANES = 8
D_IN_PAD = 2944
GROUP_W = C_WIDTH // SSM_GROUPS

PROMPT_ROWS = 512
SAMPLE_SEQS = 16
FILL_COLS = 256
FILL_POINTS = 6
FF_ROWS = 512
FF_COLS = 1024
VMEM_LIMIT = 56 * 1024 * 1024

F32 = jnp.float32
BF16 = jnp.bfloat16


def _bf(x):
    return x.astype(BF16)


def _dot(a, b):
    return jnp.dot(a, b, preferred_element_type=F32)


def _dot_nt(a, b):
    return lax.dot_general(a, b, (((1,), (1,)), ((), ())), preferred_element_type=F32)


def _split3(a):
    a1 = _bf(a)
    r1 = a - a1.astype(F32)
    a2 = _bf(r1)
    r2 = r1 - a2.astype(F32)
    return a1, a2, _bf(r2)


def _sel_left(m01, a):
    a1, a2, a3 = _split3(a)
    return _dot(m01, a1) + _dot(m01, a2) + _dot(m01, a3)


def _rmsnorm(x, g):
    ms = jnp.mean(x * x, axis=-1, keepdims=True)
    return (x * lax.rsqrt(ms + EPS)) * g


def _gelu(x):
    return 0.5 * x * (1.0 + lax.erf(x * (0.5 ** 0.5)))


def _silu(x):
    return x * jax.nn.sigmoid(x)


def _softplus(x):
    return jnp.maximum(x, 0.0) + jnp.log1p(jnp.exp(-jnp.abs(x)))


def _block_diag_pair(t):
    lane = lax.broadcasted_iota(jnp.int32, t.shape, 1)
    lo = jnp.where(lane < SSM_HEAD_DIM, t, 0.0)
    hi = jnp.where(lane >= SSM_HEAD_DIM, t, 0.0)
    return _bf(jnp.concatenate([lo, hi], axis=0))


def _causal_taps_rows(xin, prev8, w_ref, n_taps):
    row8 = lax.broadcasted_iota(jnp.int32, prev8.shape, 0)
    out = None
    for k in range(n_taps):
        back = n_taps - 1 - k
        if back == 0:
            sh = xin
        else:
            r = pltpu.roll(xin, back, axis=0)
            first = jnp.where(row8 >= back, r[0:SUBLANES], pltpu.roll(prev8, back, axis=0))
            sh = jnp.concatenate([first, r[SUBLANES:]], axis=0)
        term = sh * w_ref[k:k + 1, :]
        out = term if out is None else out + term
    return out


def _causal_taps_seqs(buf_ref, s0, n_seq, w_ref, n_taps, width):
    out = None
    for k in range(n_taps):
        back = n_taps - 1 - k
        sh = buf_ref[s0:s0 + n_seq, SUBLANES - back:2 * SUBLANES - back, :].reshape(n_seq * DEC_SEQ, width)
        term = sh * w_ref[k:k + 1, :]
        out = term if out is None else out + term
    return out


def _advance(gen):
    try:
        next(gen)
        return True
    except StopIteration:
        return False


def _mix_kernel(*refs, prompt, rows, n_in, n_carry, steps=None, final=False):
    refs = refs[:n_in] + refs[n_in + n_carry:]
    if prompt:
        (x_ref, xn_ref, g1_ref, win_ref, wa_ref, ba_ref, cw_ref, scw_ref, scb_ref, dtb_ref, alog_ref,
         dsk_ref, snorm_ref, wout_ref, g2_ref, w1_ref, w2_ref, gf_ref,
         xo_ref, vrow_ref, convo_ref, sconvo_ref, ssmo_ref,
         tailb_ref, tailc_ref, st_ref, proja_ref, projb_ref, hn_ref, mixa_ref, mixb_ref,
         xm_ref, hff_ref, f_ref) = refs
    else:
        (x_ref, g1_ref, win_ref, wa_ref, ba_ref, cw_ref, scw_ref, scb_ref, dtb_ref, alog_ref,
         dsk_ref, snorm_ref, wout_ref, conv0_ref, sconv0_ref, ssm0_ref,
         xo_ref, vrow_ref, convo_ref, sconvo_ref, ssmo_ref,
         cbufb_ref, cbufc_ref) = refs
    n_sub = rows // CHUNK
    seqs_per_sub = CHUNK // DEC_SEQ

    def in_proj(x_rows):
        return _dot(_bf(_rmsnorm(x_rows, g1_ref[...])), win_ref[...])

    if prompt:
        half = rows // 2
        subs_per_half = n_sub // 2
        n_fill = -(-D_IN_PAD // FILL_COLS)
        t = pl.program_id(0)
        j = t % steps
        live = t < pl.num_programs(0) - 1

        @pl.when(j == 0)
        def _():
            tailb_ref[...] = jnp.zeros(tailb_ref.shape, F32)
            tailc_ref[...] = jnp.zeros(tailc_ref.shape, F32)
            st_ref[...] = jnp.zeros(st_ref.shape, F32)

        @pl.when(t == 0)
        def _():
            proja_ref[...] = in_proj(x_ref[0:half, :])
            xm_ref[...] = jnp.zeros(xm_ref.shape, F32)

        xm = xm_ref[...]
        xo_ref[...] = xm
        hff_ref[...] = _bf(_rmsnorm(xm, g2_ref[...]))
    else:
        cbufb_ref[:, SUBLANES - (CONV_B - 1):SUBLANES, :] = conv0_ref[...]
        cbufc_ref[:, SUBLANES - (CONV_C - 1):SUBLANES, :] = sconv0_ref[...]

    row = lax.broadcasted_iota(jnp.int32, (CHUNK, CHUNK), 0)
    col = lax.broadcasted_iota(jnp.int32, (CHUNK, CHUNK), 1)
    if prompt:
        causal = col <= row
    else:
        causal = ((row >> 3) == (col >> 3)) & (col <= row)
    causal01 = _bf(jnp.where(causal, 1.0, 0.0))
    erow = lax.broadcasted_iota(jnp.int32, (LANES, C_WIDTH), 0)
    ecol = lax.broadcasted_iota(jnp.int32, (LANES, C_WIDTH), 1)
    expand01 = _bf(jnp.where((ecol >> 6) == erow, 1.0, 0.0))

    def expand(a):
        a1 = _bf(a)
        a2 = _bf(a - a1.astype(F32))
        return _dot(a1, expand01) + _dot(a2, expand01)

    a_neg = -jnp.exp(alog_ref[...])
    if prompt:
        upper01 = _bf(jnp.where(row <= col, 1.0, 0.0))
        xrow = lax.broadcasted_iota(jnp.int32, (LANES, 2 * C_WIDTH), 0)
        xcol = lax.broadcasted_iota(jnp.int32, (LANES, 2 * C_WIDTH), 1)
        expand2 = _bf(jnp.where((xrow < 4 * SSM_HEADS) & ((xcol >> 6) == (xrow & (2 * SSM_HEADS - 1))),
                                1.0, 0.0))
        arow = lax.broadcasted_iota(jnp.int32, (CHUNK, A_HEADS * CHUNK), 0)
        acol = lax.broadcasted_iota(jnp.int32, (CHUNK, A_HEADS * CHUNK), 1)
        wa_b = _bf(jnp.where((acol & (CHUNK - 1)) <= arow, wa_ref[...], 0.0))
        hist = {"b": tailb_ref[...], "c": tailc_ref[...]}
        mix_refs = (mixa_ref, mixb_ref)

    fills = []
    points_left = [0]
    ffn_q = []
    step_points_left = [FILL_POINTS * n_sub]

    def fill_cols(dst, k):
        cs = slice(k * FILL_COLS, min((k + 1) * FILL_COLS, D_IN_PAD))
        dst[:, cs] = _dot(hn_ref[...], win_ref[:, cs])

    def ff_up(k):
        cs = slice(k * FILL_COLS, (k + 1) * FILL_COLS)
        f = jnp.maximum(_dot(hff_ref[...], w1_ref[:, cs]), 0.0)
        f_ref[:, cs] = _bf(f * f)

    def ff_down(kc, nc):
        ks = slice(kc * FF_COLS, (kc + 1) * FF_COLS)
        cs = slice(nc * FILL_COLS, (nc + 1) * FILL_COLS)
        xo_ref[:, cs] += _dot(f_ref[:, ks], w2_ref[ks, cs])

    def ff_final_norm():
        xo_ref[...] = _rmsnorm(xo_ref[...], gf_ref[...])

    if prompt:
        ups_per_k = FF_COLS // FILL_COLS
        for kc in range(D_FF // FF_COLS):
            ffn_q.extend(functools.partial(ff_up, kc * ups_per_k + i) for i in range(ups_per_k))
            ffn_q.extend(functools.partial(ff_down, kc, nc) for nc in range(D_MODEL // FILL_COLS))
        if final:
            ffn_q.append(ff_final_norm)

    def fill():
        for queue, left in ((fills, points_left), (ffn_q, step_points_left)):
            n = -(-len(queue) // max(left[0], 1))
            left[0] -= 1
            for _ in range(min(n, len(queue))):
                queue.pop(0)()

    def sub_block(c):
        sl = slice(c * CHUNK, (c + 1) * CHUNK)
        x = x_ref[sl, :]
        if prompt:
            if c % subs_per_half == 0:
                nxt = x_ref[half:rows, :] if c == 0 else xn_ref[...]
                dst = projb_ref if c == 0 else proja_ref
                hn_ref[...] = _bf(_rmsnorm(nxt, g1_ref[...]))
                fills.extend(functools.partial(fill_cols, dst, k) for k in range(n_fill))
            pref = proja_ref if c < subs_per_half else projb_ref
            lr = slice((c % subs_per_half) * CHUNK, (c % subs_per_half + 1) * CHUNK)
            P = lambda c0, w, pref=pref, lr=lr: pref[lr, c0:c0 + w]
        else:
            proj = in_proj(x)
            P = lambda c0, w, proj=proj: proj[:, c0:c0 + w]

        if prompt:
            dt_t = _softplus(P(COL_DT, LANES).T[0:SSM_HEADS, :] + dtb_ref[...])
            a = dt_t * a_neg
            a1 = _bf(a).astype(F32)
            a2 = _bf(a - a1).astype(F32)
            r = _dot(_bf(jnp.concatenate([a1, a2, a - a1 - a2, jnp.zeros_like(a)], axis=0)), upper01)
            cum_t = r[0:8] + r[8:16] + r[16:24]
            expcum_t = jnp.exp(cum_t)
            wst_t = jnp.exp(cum_t[:, CHUNK - 1:CHUNK] - cum_t) * dt_t
            w_hi = _bf(wst_t).astype(F32)
            e_hi = _bf(expcum_t).astype(F32)
            pack = jnp.concatenate([w_hi, e_hi, wst_t - w_hi, expcum_t - e_hi, cum_t,
                                    jnp.zeros((CHUNK - 5 * SSM_HEADS, CHUNK), F32)], axis=0)
            cols = pack.T
            ex = _dot(_bf(cols), expand2)
            w_state_x, expcum_x = ex[:, 0:C_WIDTH], ex[:, C_WIDTH:]
            cum, cum_lane0 = cols, 4 * SSM_HEADS
        else:
            cum_lane0 = 0
            dt = _softplus(P(COL_DT, LANES) + dtb_ref[...])
            cum = _sel_left(causal01, dt * a_neg)
            cum3 = cum.reshape(seqs_per_sub, DEC_SEQ, LANES)
            tot = jnp.broadcast_to(cum3[:, DEC_SEQ - 1:DEC_SEQ, :], cum3.shape).reshape(CHUNK, LANES)
            expcum = jnp.exp(cum)
            w_state = jnp.exp(tot - cum) * dt
            ex = expand(jnp.concatenate([w_state, expcum], axis=0))
            w_state_x, expcum_x = ex[0:CHUNK], ex[CHUNK:]
            cum_t = cum.T
            dt_t = dt.T

        u = _gelu(P(COL_U, A_WIDTH))
        v = _gelu(P(COL_V, A_WIDTH))
        if prompt:
            pairs = []
            for p in range(A_HEADS // 2):
                rhs = _block_diag_pair(v[:, p * LANES:(p + 1) * LANES])
                pairs.append(_dot(wa_b[:, p * 2 * CHUNK:(p + 1) * 2 * CHUNK], rhs))
            s = jnp.concatenate(pairs, axis=1) + ba_ref[...]
            if c == n_sub - 1:
                hist["v"] = v
        else:
            s0 = c * seqs_per_sub
            v3 = v.reshape(seqs_per_sub, DEC_SEQ, A_WIDTH)
            trow = lax.broadcasted_iota(jnp.int32, (DEC_SEQ, A_WIDTH), 0)
            s3 = jnp.zeros((seqs_per_sub, DEC_SEQ, A_WIDTH), F32)
            for jj in range(DEC_SEQ):
                coef = jnp.where(trow >= jj, wa_ref[jj], 0.0)
                s3 = s3 + coef[None] * v3[:, jj:jj + 1, :]
            s = (s3 + ba_ref[0:DEC_SEQ, :][None]).reshape(CHUNK, A_WIDTH)
            vrow_ref[s0:s0 + seqs_per_sub] = v3
        ya = u * s
        fill()
        yield

        cg_in = P(COL_CG, B_WIDTH) * P(COL_HB, B_WIDTH)
        xbc_in = P(COL_XBC, SSM_CONV_DIM)
        if prompt:
            conv = _causal_taps_rows(cg_in, hist["b"], cw_ref, CONV_B)
            xbc = _causal_taps_rows(xbc_in, hist["c"], scw_ref, CONV_C)
            hist["b"] = cg_in[CHUNK - SUBLANES:, :]
            hist["c"] = xbc_in[CHUNK - SUBLANES:, :]
        else:
            cbufb_ref[s0:s0 + seqs_per_sub, SUBLANES:, :] = cg_in.reshape(seqs_per_sub, DEC_SEQ, B_WIDTH)
            cbufc_ref[s0:s0 + seqs_per_sub, SUBLANES:, :] = xbc_in.reshape(seqs_per_sub, DEC_SEQ, SSM_CONV_DIM)
            conv = _causal_taps_seqs(cbufb_ref, s0, seqs_per_sub, cw_ref, CONV_B, B_WIDTH)
            xbc = _causal_taps_seqs(cbufc_ref, s0, seqs_per_sub, scw_ref, CONV_C, SSM_CONV_DIM)
        yb = P(COL_BG, B_WIDTH) * conv
        fill()
        yield

        xbc = _silu(xbc + scb_ref[...])
        xs = xbc[:, 0:C_WIDTH]
        bm = xbc[:, C_WIDTH:C_WIDTH + SSM_GROUPS * D_STATE]
        cm = xbc[:, C_WIDTH + SSM_GROUPS * D_STATE:]
        xw = xs * w_state_x
        bm_b = _bf(bm)
        cm_b = _bf(cm)
        fill()
        yield

        y_pairs = []
        for g in range(SSM_GROUPS):
            gs = slice(g * D_STATE, (g + 1) * D_STATE)
            cb = _dot_nt(cm_b[:, gs], bm_b[:, gs])
            wts = []
            for kk in range(HEADS_PER_GROUP):
                k = g * HEADS_PER_GROUP + kk
                seg = cum[:, cum_lane0 + k:cum_lane0 + k + 1] - cum_t[k:k + 1, :]
                decay = jnp.exp(jnp.where(causal, seg, -jnp.inf))
                wts.append(_bf(cb * decay * dt_t[k:k + 1, :]))
            for p in range(HEADS_PER_GROUP // 2):
                k0 = g * HEADS_PER_GROUP + 2 * p
                lhs = jnp.concatenate([wts[2 * p], wts[2 * p + 1]], axis=1)
                rhs = _block_diag_pair(xs[:, k0 * SSM_HEAD_DIM:(k0 + 2) * SSM_HEAD_DIM])
                y_pairs.append(_dot(lhs, rhs))
            fill()
            yield
        y = jnp.concatenate(y_pairs, axis=1)

        if prompt:
            xw_b = _bf(xw)
            y_off = []
            for g in range(SSM_GROUPS):
                gs = slice(g * D_STATE, (g + 1) * D_STATE)
                gw = slice(g * GROUP_W, (g + 1) * GROUP_W)
                st_g = st_ref[:, gw]
                y_off.append(_dot(cm_b[:, gs], _bf(st_g)))
                st_ref[:, gw] = expcum_x[CHUNK - 1:CHUNK, gw] * st_g + _dot(_bf(bm[:, gs].T), xw_b[:, gw])
            y = y + jnp.concatenate(y_off, axis=1) * expcum_x
        else:
            exptot = jnp.exp(tot)
            xw_t = [_bf(xw[:, g * GROUP_W:(g + 1) * GROUP_W].T) for g in range(SSM_GROUPS)]
            y_off = []
            for i in range(seqs_per_sub):
                rs = slice(i * DEC_SEQ, (i + 1) * DEC_SEQ)
                own_rows = (row >> 3) == i
                parts = []
                for g in range(SSM_GROUPS):
                    gs = slice(g * D_STATE, (g + 1) * D_STATE)
                    gw = slice(g * GROUP_W, (g + 1) * GROUP_W)
                    h0 = ssm0_ref[s0 + i, gw, :]
                    parts.append(_dot_nt(_bf(cm[rs, gs]), _bf(h0)))
                    upd = _dot(xw_t[g], _bf(jnp.where(own_rows, bm[:, gs], 0.0)))
                    for kk in range(HEADS_PER_GROUP):
                        k = g * HEADS_PER_GROUP + kk
                        hs = slice(kk * SSM_HEAD_DIM, (kk + 1) * SSM_HEAD_DIM)
                        dec = jnp.broadcast_to(exptot[i * DEC_SEQ:i * DEC_SEQ + 1, k:k + 1],
                                               (SSM_HEAD_DIM, D_STATE))
                        ssmo_ref[s0 + i, k * SSM_HEAD_DIM:(k + 1) * SSM_HEAD_DIM, :] = (
                            dec * h0[hs, :] + upd[hs, :])
                y_off.append(jnp.concatenate(parts, axis=1))
            y = y + jnp.concatenate(y_off, axis=0) * expcum_x

        y = y + dsk_ref[...] * xs
        yc = _rmsnorm(y * _silu(P(COL_Z, C_WIDTH)), snorm_ref[...])
        fill()
        yield
        mix = jnp.concatenate([_bf(ya), _bf(yb), _bf(yc)], axis=1)
        if prompt:
            mix_refs[c // subs_per_half][lr, :] = mix
        else:
            xo_ref[sl, :] = x + _dot(mix, wout_ref[...])

    def out_cols(h, k):
        rs = slice(h * half, (h + 1) * half)
        cs = slice(k * FILL_COLS, (k + 1) * FILL_COLS)
        xm_ref[rs, cs] = x_ref[rs, cs] + _dot(mix_refs[h][...], wout_ref[:, cs])

    group = subs_per_half if prompt else 1
    for c0 in range(0, n_sub, group):
        if prompt:
            points_left[0] = FILL_POINTS * group
        gens = [sub_block(c) for c in range(c0, c0 + group)]
        while gens:
            gens = [g for g in gens if _advance(g)]
        while fills:
            fills.pop(0)()
        if prompt:
            outs = [functools.partial(out_cols, c0 // group, k) for k in range(D_MODEL // FILL_COLS)]
            if c0 + group < n_sub:
                fills.extend(outs)
            else:
                for task in outs:
                    task()
    while ffn_q:
        ffn_q.pop(0)()

    if prompt:
        tailb_ref[...] = hist["b"]
        tailc_ref[...] = hist["c"]

        @pl.when(live)
        def _():
            vrow_ref[...] = hist["v"]
            convo_ref[...] = hist["b"][SUBLANES - (CONV_B - 1):, :]
            sconvo_ref[...] = hist["c"][SUBLANES - (CONV_C - 1):, :]

        @pl.when(live & (j == steps - 1))
        def _():
            ssmo_ref[...] = st_ref[...].T
    else:
        convo_ref[...] = cbufb_ref[:, 2 * SUBLANES - (CONV_B - 1):, :]
        sconvo_ref[...] = cbufc_ref[:, 2 * SUBLANES - (CONV_C - 1):, :]


def _ff_kernel(x_ref, g2_ref, w1_ref, w2_ref, gf_ref, o_ref, f_ref, *, final):
    x = x_ref[...]
    h = _bf(_rmsnorm(x, g2_ref[...]))
    for c in range(D_FF // FF_COLS):
        cs = slice(c * FF_COLS, (c + 1) * FF_COLS)
        f = jnp.maximum(_dot(h, w1_ref[:, cs]), 0.0)
        f_ref[:, cs] = _bf(f * f)
    y = x + _dot(f_ref[...], w2_ref[...])
    if final:
        y = _rmsnorm(y, gf_ref[...])
    o_ref[...] = y


def _layer_spec(shape, layer, n_grid):
    zeros = (0,) * len(shape)
    if n_grid == 1:
        index_map = lambda i: (layer,) + zeros
    else:
        index_map = lambda i, j: (layer,) + zeros
    return pl.BlockSpec((None,) + tuple(shape), index_map, pipeline_mode=pl.Buffered(1))


def _mix_call(prompt, layer, x2d, params, states=None, carry=None, seq=None, final=False):
    n_rows = x2d.shape[0]
    steps = None
    if prompt:
        rows = PROMPT_ROWS
        n_batch = n_rows // seq
        steps = seq // rows
        n_blocks = n_rows // rows
        grid = (n_blocks + 1,)
        xmap = lambda t: (jnp.minimum(t, n_blocks - 1), 0)
        omap = lambda t: (jnp.maximum(t - 1, 0), 0)
        smap = lambda t: (layer, jnp.minimum(t // steps, n_batch - 1), 0, 0)
        sblk = (None, None)
        state_rows = (CHUNK, CONV_B - 1, CONV_C - 1, C_WIDTH)
    else:
        rows = SAMPLE_SEQS * DEC_SEQ
        n_batch = n_rows // DEC_SEQ
        grid = (n_batch // SAMPLE_SEQS,)
        xmap = omap = lambda i: (i, 0)
        smap = lambda i: (layer, i, 0, 0)
        sblk = (None, SAMPLE_SEQS)
        state_rows = (DEC_SEQ, CONV_B - 1, CONV_C - 1, C_WIDTH)
    state_cols = (A_WIDTH, B_WIDTH, SSM_CONV_DIM, D_STATE)

    par_specs = [_layer_spec(p.shape[1:], layer, 1) if p.ndim > 2 else
                 pl.BlockSpec(p.shape, lambda i: (0, 0), pipeline_mode=pl.Buffered(1)) for p in params]
    in_specs = [pl.BlockSpec((rows, D_MODEL), xmap)] + par_specs
    args = [x2d] + list(params)
    if prompt:
        half = rows // 2
        last_half = n_rows // half - 1
        nmap = lambda t: (jnp.minimum(2 * t + 2, last_half), 0)
        in_specs.insert(1, pl.BlockSpec((half, D_MODEL), nmap))
        args.insert(1, x2d)
        scratch = [
            pltpu.VMEM((SUBLANES, B_WIDTH), F32),
            pltpu.VMEM((SUBLANES, SSM_CONV_DIM), F32),
            pltpu.VMEM((D_STATE, C_WIDTH), F32),
            pltpu.VMEM((half, D_IN_PAD), F32),
            pltpu.VMEM((half, D_IN_PAD), F32),
            pltpu.VMEM((half, D_MODEL), BF16),
            pltpu.VMEM((half, D_MODEL), BF16),
            pltpu.VMEM((half, D_MODEL), BF16),
            pltpu.VMEM((rows, D_MODEL), F32),
            pltpu.VMEM((rows, D_MODEL), BF16),
            pltpu.VMEM((rows, D_FF), BF16),
        ]
        sem = ("arbitrary",)
    else:
        in_specs += [pl.BlockSpec(sblk + (r, w), smap)
                     for r, w in zip(state_rows[1:], state_cols[1:])]
        args += list(states)
        scratch = [
            pltpu.VMEM((SAMPLE_SEQS, 2 * SUBLANES, B_WIDTH), F32),
            pltpu.VMEM((SAMPLE_SEQS, 2 * SUBLANES, SSM_CONV_DIM), F32),
        ]
        sem = ("parallel",)

    out_shape = [jax.ShapeDtypeStruct((n_rows, D_MODEL), F32)]
    out_specs = [pl.BlockSpec((rows, D_MODEL), omap)]
    for r, w in zip(state_rows, state_cols):
        out_shape.append(jax.ShapeDtypeStruct((DEPTH, n_batch, r, w), F32))
        out_specs.append(pl.BlockSpec(sblk + (r, w), smap))
    n_in = len(args)
    aliases = {}
    if carry is not None:
        in_specs += [pl.BlockSpec(memory_space=pl.ANY)] * len(carry)
        args += list(carry)
        aliases = {n_in + k: 1 + k for k in range(len(carry))}

    return pl.pallas_call(
        functools.partial(_mix_kernel, prompt=prompt, rows=rows, n_in=n_in, n_carry=len(aliases),
                          steps=steps, final=final),
        grid=grid,
        in_specs=in_specs,
        out_specs=out_specs,
        out_shape=out_shape,
        scratch_shapes=scratch,
        input_output_aliases=aliases,
        compiler_params=pltpu.CompilerParams(dimension_semantics=sem, vmem_limit_bytes=VMEM_LIMIT),
        name="mix_prompt" if prompt else "mix_sample",
    )(*args)


def _ff_call(layer, x2d, g2, w1, w2, gf, final):
    n_rows = x2d.shape[0]
    return pl.pallas_call(
        functools.partial(_ff_kernel, final=final),
        grid=(n_rows // FF_ROWS,),
        in_specs=[
            pl.BlockSpec((FF_ROWS, D_MODEL), lambda i: (i, 0)),
            _layer_spec(g2.shape[1:], layer, 1),
            _layer_spec(w1.shape[1:], layer, 1),
            _layer_spec(w2.shape[1:], layer, 1),
            pl.BlockSpec(gf.shape, lambda i: (0, 0), pipeline_mode=pl.Buffered(1)),
        ],
        out_specs=pl.BlockSpec((FF_ROWS, D_MODEL), lambda i: (i, 0)),
        out_shape=jax.ShapeDtypeStruct((n_rows, D_MODEL), F32),
        scratch_shapes=[pltpu.VMEM((FF_ROWS, D_FF), BF16)],
        compiler_params=pltpu.CompilerParams(dimension_semantics=("parallel",),
                                             vmem_limit_bytes=VMEM_LIMIT),
        name="ffn",
    )(x2d, g2, w1, w2, gf)


def kernel(x_prompt, x_sample, state_conv, state_ssm_conv, state_ssm, norm1, w_in, w_s, b_s, conv_w,
           ssm_conv_w, ssm_conv_b, dt_bias, a_log, d_skip, ssm_norm, w_out, norm2, w_ff1, w_ff2,
           final_norm):
    bp, seq, _ = x_prompt.shape
    bs, dseq, _ = x_sample.shape
    assert seq % PROMPT_ROWS == 0 and dseq == DEC_SEQ and bs % SAMPLE_SEQS == 0

    win_b = _bf(jnp.pad(w_in, ((0, 0), (0, 0), (0, D_IN_PAD - D_IN))))
    wout_b = _bf(w_out)
    w1_b = _bf(w_ff1)
    w2_b = _bf(w_ff2)
    g1 = norm1.reshape(DEPTH, 1, D_MODEL)
    g2 = norm2.reshape(DEPTH, 1, D_MODEL)
    gf = final_norm.reshape(1, D_MODEL)
    wa_prompt = jnp.transpose(w_s, (0, 2, 1, 3)).reshape(DEPTH, CHUNK, A_HEADS * CHUNK)
    wa_sample = jnp.repeat(jnp.transpose(w_s[:, :, :DEC_SEQ, :DEC_SEQ], (0, 3, 2, 1)),
                           A_WIDTH // A_HEADS, axis=-1)
    ba = jnp.repeat(jnp.transpose(b_s, (0, 2, 1)), A_WIDTH // A_HEADS, axis=-1)
    scb = ssm_conv_b.reshape(DEPTH, 1, SSM_CONV_DIM)
    pad_heads = lambda p: jnp.pad(p, ((0, 0), (0, LANES - SSM_HEADS))).reshape(DEPTH, 1, LANES)
    dtb = pad_heads(dt_bias)
    alog = pad_heads(a_log)
    dsk = jnp.repeat(d_skip, SSM_HEAD_DIM, axis=-1).reshape(DEPTH, 1, C_WIDTH)
    snorm = ssm_norm.reshape(DEPTH, 1, C_WIDTH)
    over_time = lambda p: jnp.broadcast_to(p[:, :, None], (DEPTH, SSM_HEADS, CHUNK))
    par_prompt = (g1, win_b, wa_prompt, ba, conv_w, ssm_conv_w, scb, over_time(dt_bias), over_time(a_log),
                  dsk, snorm, wout_b)
    par_sample = (g1, win_b, wa_sample, ba, conv_w, ssm_conv_w, scb, dtb, alog, dsk, snorm, wout_b)

    xp = x_prompt.reshape(bp * seq, D_MODEL)
    xs = x_sample.reshape(bs * dseq, D_MODEL)
    ssm_in = state_ssm.reshape(DEPTH, bs, C_WIDTH, D_STATE)
    st_p = st_s = None
    for l in range(DEPTH):
        final = l == DEPTH - 1
        xp, *st_p = _mix_call(True, l, xp, par_prompt + (g2, w1_b, w2_b, gf), carry=st_p, seq=seq, final=final)
        xs, *st_s = _mix_call(False, l, xs, par_sample, (state_conv, state_ssm_conv, ssm_in), carry=st_s)
        xs = _ff_call(l, xs, g2, w1_b, w2_b, gf, final)

    y_prompt = xp.reshape(bp, seq, D_MODEL)
    y_sample = xs.reshape(bs, dseq, D_MODEL)
    ssm_shape = lambda b: (DEPTH, b, SSM_HEADS, SSM_HEAD_DIM, D_STATE)
    return (y_prompt, y_sample,
            st_p[0], st_p[1], st_p[2], st_p[3].reshape(ssm_shape(bp)),
            st_s[0], st_s[1], st_s[2], st_s[3].reshape(ssm_shape(bs)))
```

```python
import functools

import jax
import jax.numpy as jnp
from jax import lax
from jax.experimental import pallas as pl
from jax.experimental.pallas import tpu as pltpu

D_MODEL = 1024
DEPTH = 4
A_WIDTH = 256
A_HEADS = 4
B_WIDTH = 256
C_WIDTH = 512
CONV_B = 3
CONV_C = 4
SSM_HEADS = 8
SSM_HEAD_DIM = 64
SSM_GROUPS = 2
HEADS_PER_GROUP = SSM_HEADS // SSM_GROUPS
D_STATE = 128
SSM_CONV_DIM = 1024
CHUNK = 128
D_FF = 4096
EPS = 1e-5
DEC_SEQ = 8

COL_U, COL_V, COL_BG, COL_CG, COL_HB, COL_Z, COL_XBC, COL_DT = 0, 256, 512, 768, 1024, 1280, 1792, 2816
D_IN = 2824
LANES = 128
SUBLANES = 8
D_IN_PAD = 2944
GROUP_W = C_WIDTH // SSM_GROUPS

PROMPT_ROWS = 512
SAMPLE_SEQS = 16
FILL_COLS = 256
FILL_POINTS = 6
FFN_HEAD_START = 2
FFN_TAIL = 6
FF_ROWS = 1024
FF_COLS = 1024
VMEM_LIMIT = 56 * 1024 * 1024

F32 = jnp.float32
BF16 = jnp.bfloat16


def _bf(x):
    return x.astype(BF16)


def _dot(a, b):
    return jnp.dot(a, b, preferred_element_type=F32)


def _dot_nt(a, b):
    return lax.dot_general(a, b, (((1,), (1,)), ((), ())), preferred_element_type=F32)


def _split3(a):
    a1 = _bf(a)
    r1 = a - a1.astype(F32)
    a2 = _bf(r1)
    r2 = r1 - a2.astype(F32)
    return a1, a2, _bf(r2)


def _sel_left(m01, a):
    a1, a2, a3 = _split3(a)
    return _dot(m01, a1) + _dot(m01, a2) + _dot(m01, a3)


def _rmsnorm(x, g):
    ms = jnp.mean(x * x, axis=-1, keepdims=True)
    return (x * lax.rsqrt(ms + EPS)) * g


def _gelu(x):
    return 0.5 * x * (1.0 + lax.erf(x * (0.5 ** 0.5)))


def _silu(x):
    return x * jax.nn.sigmoid(x)


def _softplus(x):
    return jnp.maximum(x, 0.0) + jnp.log1p(jnp.exp(-jnp.abs(x)))


def _block_diag_pair(t):
    lane = lax.broadcasted_iota(jnp.int32, t.shape, 1)
    lo = jnp.where(lane < SSM_HEAD_DIM, t, 0.0)
    hi = jnp.where(lane >= SSM_HEAD_DIM, t, 0.0)
    return _bf(jnp.concatenate([lo, hi], axis=0))


def _causal_taps_rows(xin, prev8, w_ref, n_taps):
    row8 = lax.broadcasted_iota(jnp.int32, prev8.shape, 0)
    out = None
    for k in range(n_taps):
        back = n_taps - 1 - k
        if back == 0:
            sh = xin
        else:
            r = pltpu.roll(xin, back, axis=0)
            first = jnp.where(row8 >= back, r[0:SUBLANES], pltpu.roll(prev8, back, axis=0))
            sh = jnp.concatenate([first, r[SUBLANES:]], axis=0)
        term = sh * w_ref[k:k + 1, :]
        out = term if out is None else out + term
    return out


def _causal_taps_seqs(buf_ref, s0, n_seq, w_ref, n_taps, width):
    out = None
    for k in range(n_taps):
        back = n_taps - 1 - k
        sh = buf_ref[s0:s0 + n_seq, SUBLANES - back:2 * SUBLANES - back, :].reshape(n_seq * DEC_SEQ, width)
        term = sh * w_ref[k:k + 1, :]
        out = term if out is None else out + term
    return out


def _advance(gen):
    try:
        next(gen)
        return True
    except StopIteration:
        return False


def _mix_kernel(*refs, prompt, rows, n_in, n_carry, steps=None, final=False, n_cast=0):
    refs = refs[:n_in] + refs[n_in + n_carry:]
    if prompt:
        cast_in, cast_out = refs[19:19 + n_cast], refs[24 + n_cast:24 + 2 * n_cast]
        refs = refs[:19] + refs[19 + n_cast:24 + n_cast] + refs[24 + 2 * n_cast:]
        (x_ref, xn_ref, g1_ref, win_ref, wdt_ref, wa_ref, ba_ref, cw_ref, scw_ref, scb_ref, dtb_ref, alog_ref,
         dsk_ref, snorm_ref, wout_ref, g2_ref, w1_ref, w2_ref, gf_ref,
         xo_ref, vrow_ref, convo_ref, sconvo_ref, ssmo_ref,
         tailb_ref, tailc_ref, st_ref, proja_ref, projb_ref, hna_ref, hnb_ref, mixa_ref, mixb_ref,
         xm_ref, hff_ref, f_ref) = refs
        for src, dst in zip(cast_in, cast_out):
            dst[...] = _bf(src[...])
    else:
        (x_ref, g1_ref, win_ref, wdt_ref, wa_ref, ba_ref, cw_ref, scw_ref, scb_ref, dtb_ref, alog_ref,
         dsk_ref, snorm_ref, wout_ref, conv0_ref, sconv0_ref, ssm0_ref,
         xo_ref, vrow_ref, convo_ref, sconvo_ref, ssmo_ref,
         cbufb_ref, cbufc_ref) = refs
    n_sub = rows // CHUNK
    seqs_per_sub = CHUNK // DEC_SEQ

    def in_proj(x_rows):
        h = _bf(_rmsnorm(x_rows, g1_ref[...]))
        return jnp.concatenate([_dot(h, win_ref[...]), _dot(h, wdt_ref[...])], axis=1)

    ups_left = [D_FF // FILL_COLS]

    def ff_up(k):
        ups_left[0] -= 1
        cs = slice(k * FILL_COLS, (k + 1) * FILL_COLS)
        f = jnp.maximum(_dot(hff_ref[...], w1_ref[:, cs]), 0.0)
        f_ref[:, cs] = _bf(f * f)

    if prompt:
        half = rows // 2
        subs_per_half = n_sub // 2
        n_fill = -(-D_IN_PAD // FILL_COLS)
        t = pl.program_id(0)
        j = t % steps
        live = t < pl.num_programs(0) - 1

        @pl.when(j == 0)
        def _():
            tailb_ref[...] = jnp.zeros(tailb_ref.shape, F32)
            tailc_ref[...] = jnp.zeros(tailc_ref.shape, F32)
            st_ref[...] = jnp.zeros(st_ref.shape, F32)

        @pl.when(t == 0)
        def _():
            proja_ref[...] = in_proj(x_ref[0:half, :])
            xm_ref[...] = jnp.zeros(xm_ref.shape, F32)
            hff_ref[...] = jnp.zeros(hff_ref.shape, BF16)

        for k in range(FFN_HEAD_START):
            ff_up(k)
        xo_ref[...] = xm_ref[...]
        hn_refs = (hna_ref, hnb_ref)
        hna_ref[...] = _bf(_rmsnorm(x_ref[half:rows, :], g1_ref[...]))
    else:
        cbufb_ref[:, SUBLANES - (CONV_B - 1):SUBLANES, :] = conv0_ref[...]
        cbufc_ref[:, SUBLANES - (CONV_C - 1):SUBLANES, :] = sconv0_ref[...]

    row = lax.broadcasted_iota(jnp.int32, (CHUNK, CHUNK), 0)
    col = lax.broadcasted_iota(jnp.int32, (CHUNK, CHUNK), 1)
    if prompt:
        causal = col <= row
    else:
        causal = ((row >> 3) == (col >> 3)) & (col <= row)
    causal01 = _bf(jnp.where(causal, 1.0, 0.0))
    erow = lax.broadcasted_iota(jnp.int32, (LANES, C_WIDTH), 0)
    ecol = lax.broadcasted_iota(jnp.int32, (LANES, C_WIDTH), 1)
    expand01 = _bf(jnp.where((ecol >> 6) == erow, 1.0, 0.0))

    def expand(a):
        a1 = _bf(a)
        a2 = _bf(a - a1.astype(F32))
        return _dot(a1, expand01) + _dot(a2, expand01)

    a_neg = -jnp.exp(alog_ref[...])
    if prompt:
        upper01 = _bf(jnp.where(row <= col, 1.0, 0.0))
        xrow = lax.broadcasted_iota(jnp.int32, (LANES, 2 * C_WIDTH), 0)
        xcol = lax.broadcasted_iota(jnp.int32, (LANES, 2 * C_WIDTH), 1)
        expand2 = _bf(jnp.where((xrow < 4 * SSM_HEADS) & ((xcol >> 6) == (xrow & (2 * SSM_HEADS - 1))),
                                1.0, 0.0))
        arow = lax.broadcasted_iota(jnp.int32, (CHUNK, A_HEADS * CHUNK), 0)
        acol = lax.broadcasted_iota(jnp.int32, (CHUNK, A_HEADS * CHUNK), 1)
        wa_b = _bf(jnp.where((acol & (CHUNK - 1)) <= arow, wa_ref[...], 0.0))
        hist = {"b": tailb_ref[...], "c": tailc_ref[...]}
        mix_refs = (mixa_ref, mixb_ref)

    fills = []
    points_left = [0]
    ffn_q = []
    step_points_left = [FILL_POINTS * n_sub]

    def fill_cols(dst, src, k):
        if k * FILL_COLS < COL_DT:
            cs = slice(k * FILL_COLS, (k + 1) * FILL_COLS)
            dst[:, cs] = _dot(src[...], win_ref[:, cs])
        else:
            dst[:, COL_DT:] = _dot(src[...], wdt_ref[...])

    def norm_next_step():
        hnb_ref[...] = _bf(_rmsnorm(xn_ref[...], g1_ref[...]))

    def norm_ffn(h):
        assert ups_left[0] == 0, "hff_ref is still being read by this step's up-projections"
        rs = slice(h * half, (h + 1) * half)
        hff_ref[rs, :] = _bf(_rmsnorm(xm_ref[rs, :], g2_ref[...]))

    def ff_down(kc, nc):
        ks = slice(kc * FF_COLS, (kc + 1) * FF_COLS)
        cs = slice(nc * FILL_COLS, (nc + 1) * FILL_COLS)
        xo_ref[:, cs] += _dot(f_ref[:, ks], w2_ref[ks, cs])

    def ff_final_norm():
        xo_ref[...] = _rmsnorm(xo_ref[...], gf_ref[...])

    if prompt:
        ffn_q.extend(functools.partial(ff_up, k) for k in range(FFN_HEAD_START, D_FF // FILL_COLS))
        for kc in range(D_FF // FF_COLS):
            ffn_q.extend(functools.partial(ff_down, kc, nc) for nc in range(D_MODEL // FILL_COLS))
        ffn_tail = ffn_q[-FFN_TAIL:]
        del ffn_q[-FFN_TAIL:]

    def fill():
        for queue, left in ((fills, points_left), (ffn_q, step_points_left)):
            n = -(-len(queue) // max(left[0], 1))
            left[0] -= 1
            for _ in range(min(n, len(queue))):
                queue.pop(0)()

    def sub_block(c):
        sl = slice(c * CHUNK, (c + 1) * CHUNK)
        x = x_ref[sl, :]
        if prompt:
            if c % subs_per_half == 0:
                h = c // subs_per_half
                dst = projb_ref if h == 0 else proja_ref
                fills.extend(functools.partial(fill_cols, dst, hn_refs[h], k) for k in range(n_fill))
                if h == 0:
                    fills.append(norm_next_step)
            pref = proja_ref if c < subs_per_half else projb_ref
            lr = slice((c % subs_per_half) * CHUNK, (c % subs_per_half + 1) * CHUNK)
            P = lambda c0, w, pref=pref, lr=lr: pref[lr, c0:c0 + w]
        else:
            proj = in_proj(x)
            P = lambda c0, w, proj=proj: proj[:, c0:c0 + w]

        if prompt:
            dt_t = _softplus(P(COL_DT, LANES).T[0:SSM_HEADS, :] + dtb_ref[...])
            a = dt_t * a_neg
            a1 = _bf(a).astype(F32)
            a2 = _bf(a - a1).astype(F32)
            r = _dot(_bf(jnp.concatenate([a1, a2, a - a1 - a2, jnp.zeros_like(a)], axis=0)), upper01)
            cum_t = r[0:8] + r[8:16] + r[16:24]
            expcum_t = jnp.exp(cum_t)
            wst_t = jnp.exp(cum_t[:, CHUNK - 1:CHUNK] - cum_t) * dt_t
            w_hi = _bf(wst_t).astype(F32)
            e_hi = _bf(expcum_t).astype(F32)
            pack = jnp.concatenate([w_hi, e_hi, wst_t - w_hi, expcum_t - e_hi, cum_t,
                                    jnp.zeros((CHUNK - 5 * SSM_HEADS, CHUNK), F32)], axis=0)
            cols = pack.T
            ex = _dot(_bf(cols), expand2)
            w_state_x, expcum_x = ex[:, 0:C_WIDTH], ex[:, C_WIDTH:]
            cum, cum_lane0 = cols, 4 * SSM_HEADS
        else:
            cum_lane0 = 0
            dt = _softplus(P(COL_DT, LANES) + dtb_ref[...])
            cum = _sel_left(causal01, dt * a_neg)
            cum3 = cum.reshape(seqs_per_sub, DEC_SEQ, LANES)
            tot = jnp.broadcast_to(cum3[:, DEC_SEQ - 1:DEC_SEQ, :], cum3.shape).reshape(CHUNK, LANES)
            expcum = jnp.exp(cum)
            w_state = jnp.exp(tot - cum) * dt
            ex = expand(jnp.concatenate([w_state, expcum], axis=0))
            w_state_x, expcum_x = ex[0:CHUNK], ex[CHUNK:]
            cum_t = cum.T
            dt_t = dt.T

        u = _gelu(P(COL_U, A_WIDTH))
        v = _gelu(P(COL_V, A_WIDTH))
        if prompt:
            pairs = []
            for p in range(A_HEADS // 2):
                rhs = _block_diag_pair(v[:, p * LANES:(p + 1) * LANES])
                pairs.append(_dot(wa_b[:, p * 2 * CHUNK:(p + 1) * 2 * CHUNK], rhs))
            s = jnp.concatenate(pairs, axis=1) + ba_ref[...]
            if c == n_sub - 1:
                hist["v"] = v
        else:
            s0 = c * seqs_per_sub
            v3 = v.reshape(seqs_per_sub, DEC_SEQ, A_WIDTH)
            trow = lax.broadcasted_iota(jnp.int32, (DEC_SEQ, A_WIDTH), 0)
            s3 = jnp.zeros((seqs_per_sub, DEC_SEQ, A_WIDTH), F32)
            for jj in range(DEC_SEQ):
                coef = jnp.where(trow >= jj, wa_ref[jj], 0.0)
                s3 = s3 + coef[None] * v3[:, jj:jj + 1, :]
            s = (s3 + ba_ref[0:DEC_SEQ, :][None]).reshape(CHUNK, A_WIDTH)
            vrow_ref[s0:s0 + seqs_per_sub] = v3
        ya = u * s
        fill()
        yield

        cg_in = P(COL_CG, B_WIDTH) * P(COL_HB, B_WIDTH)
        xbc_in = P(COL_XBC, SSM_CONV_DIM)
        if prompt:
            conv = _causal_taps_rows(cg_in, hist["b"], cw_ref, CONV_B)
            xbc = _causal_taps_rows(xbc_in, hist["c"], scw_ref, CONV_C)
            hist["b"] = cg_in[CHUNK - SUBLANES:, :]
            hist["c"] = xbc_in[CHUNK - SUBLANES:, :]
        else:
            cbufb_ref[s0:s0 + seqs_per_sub, SUBLANES:, :] = cg_in.reshape(seqs_per_sub, DEC_SEQ, B_WIDTH)
            cbufc_ref[s0:s0 + seqs_per_sub, SUBLANES:, :] = xbc_in.reshape(seqs_per_sub, DEC_SEQ, SSM_CONV_DIM)
            conv = _causal_taps_seqs(cbufb_ref, s0, seqs_per_sub, cw_ref, CONV_B, B_WIDTH)
            xbc = _causal_taps_seqs(cbufc_ref, s0, seqs_per_sub, scw_ref, CONV_C, SSM_CONV_DIM)
        yb = P(COL_BG, B_WIDTH) * conv
        fill()
        yield

        xbc = _silu(xbc + scb_ref[...])
        xs = xbc[:, 0:C_WIDTH]
        bm = xbc[:, C_WIDTH:C_WIDTH + SSM_GROUPS * D_STATE]
        cm = xbc[:, C_WIDTH + SSM_GROUPS * D_STATE:]
        xw = xs * w_state_x
        bm_b = _bf(bm)
        cm_b = _bf(cm)
        fill()
        yield

        y_pairs = []
        for g in range(SSM_GROUPS):
            gs = slice(g * D_STATE, (g + 1) * D_STATE)
            cb = _dot_nt(cm_b[:, gs], bm_b[:, gs])
            wts = []
            for kk in range(HEADS_PER_GROUP):
                k = g * HEADS_PER_GROUP + kk
                seg = cum[:, cum_lane0 + k:cum_lane0 + k + 1] - cum_t[k:k + 1, :]
                decay = jnp.exp(jnp.where(causal, seg, -jnp.inf))
                wts.append(_bf(cb * decay * dt_t[k:k + 1, :]))
            for p in range(HEADS_PER_GROUP // 2):
                k0 = g * HEADS_PER_GROUP + 2 * p
                lhs = jnp.concatenate([wts[2 * p], wts[2 * p + 1]], axis=1)
                rhs = _block_diag_pair(xs[:, k0 * SSM_HEAD_DIM:(k0 + 2) * SSM_HEAD_DIM])
                y_pairs.append(_dot(lhs, rhs))
            fill()
            yield
        y = jnp.concatenate(y_pairs, axis=1)

        if prompt:
            xw_b = _bf(xw)
            y_off = []
            for g in range(SSM_GROUPS):
                gs = slice(g * D_STATE, (g + 1) * D_STATE)
                gw = slice(g * GROUP_W, (g + 1) * GROUP_W)
                st_g = st_ref[:, gw]
                y_off.append(_dot(cm_b[:, gs], _bf(st_g)))
                st_ref[:, gw] = expcum_x[CHUNK - 1:CHUNK, gw] * st_g + _dot(_bf(bm[:, gs].T), xw_b[:, gw])
            y = y + jnp.concatenate(y_off, axis=1) * expcum_x
        else:
            exptot = jnp.exp(tot)
            xw_t = [_bf(xw[:, g * GROUP_W:(g + 1) * GROUP_W].T) for g in range(SSM_GROUPS)]
            y_off = []
            for i in range(seqs_per_sub):
                rs = slice(i * DEC_SEQ, (i + 1) * DEC_SEQ)
                own_rows = (row >> 3) == i
                parts = []
                for g in range(SSM_GROUPS):
                    gs = slice(g * D_STATE, (g + 1) * D_STATE)
                    gw = slice(g * GROUP_W, (g + 1) * GROUP_W)
                    h0 = ssm0_ref[s0 + i, gw, :]
                    parts.append(_dot_nt(_bf(cm[rs, gs]), _bf(h0)))
                    upd = _dot(xw_t[g], _bf(jnp.where(own_rows, bm[:, gs], 0.0)))
                    for kk in range(HEADS_PER_GROUP):
                        k = g * HEADS_PER_GROUP + kk
                        hs = slice(kk * SSM_HEAD_DIM, (kk + 1) * SSM_HEAD_DIM)
                        dec = jnp.broadcast_to(exptot[i * DEC_SEQ:i * DEC_SEQ + 1, k:k + 1],
                                               (SSM_HEAD_DIM, D_STATE))
                        ssmo_ref[s0 + i, k * SSM_HEAD_DIM:(k + 1) * SSM_HEAD_DIM, :] = (
                            dec * h0[hs, :] + upd[hs, :])
                y_off.append(jnp.concatenate(parts, axis=1))
            y = y + jnp.concatenate(y_off, axis=0) * expcum_x

        y = y + dsk_ref[...] * xs
        yc = _rmsnorm(y * _silu(P(COL_Z, C_WIDTH)), snorm_ref[...])
        fill()
        yield
        mix = jnp.concatenate([_bf(ya), _bf(yb), _bf(yc)], axis=1)
        if prompt:
            mix_refs[c // subs_per_half][lr, :] = mix
        else:
            xo_ref[sl, :] = x + _dot(mix, wout_ref[...])

    def out_cols(h, k):
        rs = slice(h * half, (h + 1) * half)
        cs = slice(k * FILL_COLS, (k + 1) * FILL_COLS)
        xm_ref[rs, cs] = x_ref[rs, cs] + _dot(mix_refs[h][...], wout_ref[:, cs])

    group = subs_per_half if prompt else 1
    for c0 in range(0, n_sub, group):
        if prompt:
            points_left[0] = FILL_POINTS * group
        gens = [sub_block(c) for c in range(c0, c0 + group)]
        while gens:
            gens = [g for g in gens if _advance(g)]
        while fills:
            fills.pop(0)()
        if prompt:
            h = c0 // group
            outs = [functools.partial(out_cols, h, k) for k in range(D_MODEL // FILL_COLS)]
            outs.append(functools.partial(norm_ffn, h))
            if c0 + group < n_sub:
                fills.extend(outs)
            else:
                while ffn_q:
                    ffn_q.pop(0)()
                for task in outs:
                    task()
                    if ffn_tail:
                        ffn_tail.pop(0)()
                while ffn_tail:
                    ffn_tail.pop(0)()
                if final:
                    ff_final_norm()

    if prompt:
        tailb_ref[...] = hist["b"]
        tailc_ref[...] = hist["c"]

        @pl.when(live)
        def _():
            vrow_ref[...] = hist["v"]
            convo_ref[...] = hist["b"][SUBLANES - (CONV_B - 1):, :]
            sconvo_ref[...] = hist["c"][SUBLANES - (CONV_C - 1):, :]

        @pl.when(live & (j == steps - 1))
        def _():
            ssmo_ref[...] = st_ref[...].T
    else:
        convo_ref[...] = cbufb_ref[:, 2 * SUBLANES - (CONV_B - 1):, :]
        sconvo_ref[...] = cbufc_ref[:, 2 * SUBLANES - (CONV_C - 1):, :]


def _ff_kernel(x_ref, g2_ref, w1_ref, w2_ref, gf_ref, o_ref, f_ref, *, final):
    x = x_ref[...]
    h = _bf(_rmsnorm(x, g2_ref[...]))
    for c in range(D_FF // FF_COLS):
        cs = slice(c * FF_COLS, (c + 1) * FF_COLS)
        f = jnp.maximum(_dot(h, w1_ref[:, cs]), 0.0)
        f_ref[:, cs] = _bf(f * f)
    y = x + _dot(f_ref[...], w2_ref[...])
    if final:
        y = _rmsnorm(y, gf_ref[...])
    o_ref[...] = y


def _layer_spec(shape, layer, n_grid):
    zeros = (0,) * len(shape)
    if n_grid == 1:
        index_map = lambda i: (layer,) + zeros
    else:
        index_map = lambda i, j: (layer,) + zeros
    return pl.BlockSpec((None,) + tuple(shape), index_map, pipeline_mode=pl.Buffered(1))


def _mix_call(prompt, layer, x2d, params, states=None, carry=None, seq=None, final=False, cast_next=()):
    n_rows = x2d.shape[0]
    steps = None
    if prompt:
        rows = PROMPT_ROWS
        n_batch = n_rows // seq
        steps = seq // rows
        n_blocks = n_rows // rows
        grid = (n_blocks + 1,)
        xmap = lambda t: (jnp.minimum(t, n_blocks - 1), 0)
        omap = lambda t: (jnp.maximum(t - 1, 0), 0)
        smap = lambda t: (layer, jnp.minimum(t // steps, n_batch - 1), 0, 0)
        sblk = (None, None)
        state_rows = (CHUNK, CONV_B - 1, CONV_C - 1, C_WIDTH)
    else:
        rows = SAMPLE_SEQS * DEC_SEQ
        n_batch = n_rows // DEC_SEQ
        grid = (n_batch // SAMPLE_SEQS,)
        xmap = omap = lambda i: (i, 0)
        smap = lambda i: (layer, i, 0, 0)
        sblk = (None, SAMPLE_SEQS)
        state_rows = (DEC_SEQ, CONV_B - 1, CONV_C - 1, C_WIDTH)
    state_cols = (A_WIDTH, B_WIDTH, SSM_CONV_DIM, D_STATE)

    par_specs = [_layer_spec(p.shape[1:], layer, 1) if p.ndim > 2 else
                 pl.BlockSpec(p.shape, lambda i: (0, 0), pipeline_mode=pl.Buffered(1)) for p in params]
    in_specs = [pl.BlockSpec((rows, D_MODEL), xmap)] + par_specs
    args = [x2d] + list(params)
    if prompt:
        half = rows // 2
        last_half = n_rows // half - 1
        nmap = lambda t: (jnp.minimum(2 * t + 2, last_half), 0)
        in_specs.insert(1, pl.BlockSpec((half, D_MODEL), nmap))
        args.insert(1, x2d)
        scratch = [
            pltpu.VMEM((SUBLANES, B_WIDTH), F32),
            pltpu.VMEM((SUBLANES, SSM_CONV_DIM), F32),
            pltpu.VMEM((D_STATE, C_WIDTH), F32),
            pltpu.VMEM((half, D_IN_PAD), F32),
            pltpu.VMEM((half, D_IN_PAD), F32),
            pltpu.VMEM((half, D_MODEL), BF16),
            pltpu.VMEM((half, D_MODEL), BF16),
            pltpu.VMEM((half, D_MODEL), BF16),
            pltpu.VMEM((half, D_MODEL), BF16),
            pltpu.VMEM((rows, D_MODEL), F32),
            pltpu.VMEM((rows, D_MODEL), BF16),
            pltpu.VMEM((rows, D_FF), BF16),
        ]
        sem = ("arbitrary",)
    else:
        in_specs += [pl.BlockSpec(sblk + (r, w), smap)
                     for r, w in zip(state_rows[1:], state_cols[1:])]
        args += list(states)
        scratch = [
            pltpu.VMEM((SAMPLE_SEQS, 2 * SUBLANES, B_WIDTH), F32),
            pltpu.VMEM((SAMPLE_SEQS, 2 * SUBLANES, SSM_CONV_DIM), F32),
        ]
        sem = ("parallel",)

    out_shape = [jax.ShapeDtypeStruct((n_rows, D_MODEL), F32)]
    out_specs = [pl.BlockSpec((rows, D_MODEL), omap)]
    for r, w in zip(state_rows, state_cols):
        out_shape.append(jax.ShapeDtypeStruct((DEPTH, n_batch, r, w), F32))
        out_specs.append(pl.BlockSpec(sblk + (r, w), smap))
    for w, cols in cast_next:
        blk = w.shape[1] // n_blocks
        cmap = lambda t: (jnp.minimum(t, n_blocks - 1), 0)
        in_specs.append(pl.BlockSpec((None, blk, cols), lambda t: (layer + 1, jnp.minimum(t, n_blocks - 1), 0)))
        args.append(w)
        out_shape.append(jax.ShapeDtypeStruct((w.shape[1], cols), BF16))
        out_specs.append(pl.BlockSpec((blk, cols), cmap))
    n_in = len(args)
    aliases = {}
    if carry is not None:
        in_specs += [pl.BlockSpec(memory_space=pl.ANY)] * len(carry)
        args += list(carry)
        aliases = {n_in + k: 1 + k for k in range(len(carry))}

    return pl.pallas_call(
        functools.partial(_mix_kernel, prompt=prompt, rows=rows, n_in=n_in, n_carry=len(aliases),
                          steps=steps, final=final, n_cast=len(cast_next)),
        grid=grid,
        in_specs=in_specs,
        out_specs=out_specs,
        out_shape=out_shape,
        scratch_shapes=scratch,
        input_output_aliases=aliases,
        compiler_params=pltpu.CompilerParams(dimension_semantics=sem, vmem_limit_bytes=VMEM_LIMIT),
        name="mix_prompt" if prompt else "mix_sample",
    )(*args)


def _ff_call(layer, x2d, g2, w1, w2, gf, final):
    n_rows = x2d.shape[0]
    return pl.pallas_call(
        functools.partial(_ff_kernel, final=final),
        grid=(n_rows // FF_ROWS,),
        in_specs=[
            pl.BlockSpec((FF_ROWS, D_MODEL), lambda i: (i, 0)),
            _layer_spec(g2.shape[1:], layer, 1),
            pl.BlockSpec(w1.shape, lambda i: (0, 0), pipeline_mode=pl.Buffered(1)),
            pl.BlockSpec(w2.shape, lambda i: (0, 0), pipeline_mode=pl.Buffered(1)),
            pl.BlockSpec(gf.shape, lambda i: (0, 0), pipeline_mode=pl.Buffered(1)),
        ],
        out_specs=pl.BlockSpec((FF_ROWS, D_MODEL), lambda i: (i, 0)),
        out_shape=jax.ShapeDtypeStruct((n_rows, D_MODEL), F32),
        scratch_shapes=[pltpu.VMEM((FF_ROWS, D_FF), BF16)],
        compiler_params=pltpu.CompilerParams(dimension_semantics=("parallel",),
                                             vmem_limit_bytes=VMEM_LIMIT),
        name="ffn",
    )(x2d, g2, w1, w2, gf)


def kernel(x_prompt, x_sample, state_conv, state_ssm_conv, state_ssm, norm1, w_in, w_s, b_s, conv_w,
           ssm_conv_w, ssm_conv_b, dt_bias, a_log, d_skip, ssm_norm, w_out, norm2, w_ff1, w_ff2,
           final_norm):
    bp, seq, _ = x_prompt.shape
    bs, dseq, _ = x_sample.shape
    assert seq % PROMPT_ROWS == 0 and dseq == DEC_SEQ and bs % SAMPLE_SEQS == 0

    wdt_b = _bf(jnp.pad(w_in[:, :, COL_DT:], ((0, 0), (0, 0), (0, D_IN_PAD - D_IN))))
    big_weights = ((w_in, COL_DT), (w_out, D_MODEL), (w_ff1, D_FF), (w_ff2, D_MODEL))
    wts = tuple(_bf(w[0, :, :cols]) for w, cols in big_weights)
    g1 = norm1.reshape(DEPTH, 1, D_MODEL)
    g2 = norm2.reshape(DEPTH, 1, D_MODEL)
    gf = final_norm.reshape(1, D_MODEL)
    wa_prompt = jnp.transpose(w_s, (0, 2, 1, 3)).reshape(DEPTH, CHUNK, A_HEADS * CHUNK)
    wa_sample = jnp.repeat(jnp.transpose(w_s[:, :, :DEC_SEQ, :DEC_SEQ], (0, 3, 2, 1)),
                           A_WIDTH // A_HEADS, axis=-1)
    ba = jnp.repeat(jnp.transpose(b_s, (0, 2, 1)), A_WIDTH // A_HEADS, axis=-1)
    scb = ssm_conv_b.reshape(DEPTH, 1, SSM_CONV_DIM)
    pad_heads = lambda p: jnp.pad(p, ((0, 0), (0, LANES - SSM_HEADS))).reshape(DEPTH, 1, LANES)
    dtb = pad_heads(dt_bias)
    alog = pad_heads(a_log)
    dsk = jnp.repeat(d_skip, SSM_HEAD_DIM, axis=-1).reshape(DEPTH, 1, C_WIDTH)
    snorm = ssm_norm.reshape(DEPTH, 1, C_WIDTH)
    over_time = lambda p: jnp.broadcast_to(p[:, :, None], (DEPTH, SSM_HEADS, CHUNK))
    mid_prompt = (wa_prompt, ba, conv_w, ssm_conv_w, scb, over_time(dt_bias), over_time(a_log), dsk, snorm)
    mid_sample = (wa_sample, ba, conv_w, ssm_conv_w, scb, dtb, alog, dsk, snorm)

    xp = x_prompt.reshape(bp * seq, D_MODEL)
    xs = x_sample.reshape(bs * dseq, D_MODEL)
    ssm_in = state_ssm.reshape(DEPTH, bs, C_WIDTH, D_STATE)
    st_p = st_s = None
    for l in range(DEPTH):
        final = l == DEPTH - 1
        win_b, wout_b, w1_b, w2_b = wts
        outs = _mix_call(True, l, xp, (g1, win_b, wdt_b) + mid_prompt + (wout_b, g2, w1_b, w2_b, gf),
                         carry=st_p, seq=seq, final=final, cast_next=() if final else big_weights)
        xp, st_p, wts = outs[0], outs[1:5], tuple(outs[5:])
        xs, *st_s = _mix_call(False, l, xs, (g1, win_b, wdt_b) + mid_sample + (wout_b,),
                              (state_conv, state_ssm_conv, ssm_in), carry=st_s)
        xs = _ff_call(l, xs, g2, w1_b, w2_b, gf, final)

    y_prompt = xp.reshape(bp, seq, D_MODEL)
    y_sample = xs.reshape(bs, dseq, D_MODEL)
    ssm_shape = lambda b: (DEPTH, b, SSM_HEADS, SSM_HEAD_DIM, D_STATE)
    return (y_prompt, y_sample,
            st_p[0], st_p[1], st_p[2], st_p[3].reshape(ssm_shape(bp)),
            st_s[0], st_s[1], st_s[2], st_s[3].reshape(ssm_shape(bs)))
```

```python
import functools

import jax
import jax.numpy as jnp
from jax import lax
from jax.experimental import pallas as pl
from jax.experimental.pallas import tpu as pltpu

D_MODEL = 1024
DEPTH = 4
A_WIDTH = 256
A_HEADS = 4
B_WIDTH = 256
C_WIDTH = 512
CONV_B = 3
CONV_C = 4
SSM_HEADS = 8
SSM_HEAD_DIM = 64
SSM_GROUPS = 2
HEADS_PER_GROUP = SSM_HEADS // SSM_GROUPS
D_STATE = 128
SSM_CONV_DIM = 1024
CHUNK = 128
D_FF = 4096
EPS = 1e-5
DEC_SEQ = 8

COL_U, COL_V, COL_BG, COL_CG, COL_HB, COL_Z, COL_XBC, COL_DT = 0, 256, 512, 768, 1024, 1280, 1792, 2816
D_IN = 2824
LANES = 128
SUBLANES = 8
D_IN_PAD = 2944
GROUP_W = C_WIDTH // SSM_GROUPS

PROMPT_ROWS = 512
SAMPLE_SEQS = 16
FILL_COLS = 256
FILL_POINTS = 6
FFN_HEAD_START = 2
FFN_TAIL = 6
FF_ROWS = 1024
FF_COLS = 1024
VMEM_LIMIT = 56 * 1024 * 1024

F32 = jnp.float32
BF16 = jnp.bfloat16


def _bf(x):
    return x.astype(BF16)


def _dot(a, b):
    return jnp.dot(a, b, preferred_element_type=F32)


def _dot_nt(a, b):
    return lax.dot_general(a, b, (((1,), (1,)), ((), ())), preferred_element_type=F32)


def _split3(a):
    a1 = _bf(a)
    r1 = a - a1.astype(F32)
    a2 = _bf(r1)
    r2 = r1 - a2.astype(F32)
    return a1, a2, _bf(r2)


def _sel_left(m01, a):
    a1, a2, a3 = _split3(a)
    return _dot(m01, a1) + _dot(m01, a2) + _dot(m01, a3)


def _rmsnorm(x, g):
    ms = jnp.mean(x * x, axis=-1, keepdims=True)
    return (x * lax.rsqrt(ms + EPS)) * g


def _gelu(x):
    return 0.5 * x * (1.0 + lax.erf(x * (0.5 ** 0.5)))


def _silu(x):
    return x * jax.nn.sigmoid(x)


def _softplus(x):
    return jnp.maximum(x, 0.0) + jnp.log1p(jnp.exp(-jnp.abs(x)))


def _block_diag_pair(t):
    lane = lax.broadcasted_iota(jnp.int32, t.shape, 1)
    lo = jnp.where(lane < SSM_HEAD_DIM, t, 0.0)
    hi = jnp.where(lane >= SSM_HEAD_DIM, t, 0.0)
    return _bf(jnp.concatenate([lo, hi], axis=0))


def _causal_taps_rows(xin, prev8, w_ref, n_taps):
    row8 = lax.broadcasted_iota(jnp.int32, prev8.shape, 0)
    out = None
    for k in range(n_taps):
        back = n_taps - 1 - k
        if back == 0:
            sh = xin
        else:
            r = pltpu.roll(xin, back, axis=0)
            first = jnp.where(row8 >= back, r[0:SUBLANES], pltpu.roll(prev8, back, axis=0))
            sh = jnp.concatenate([first, r[SUBLANES:]], axis=0)
        term = sh * w_ref[k:k + 1, :]
        out = term if out is None else out + term
    return out


def _causal_taps_seqs(buf_ref, s0, n_seq, w_ref, n_taps, width):
    out = None
    for k in range(n_taps):
        back = n_taps - 1 - k
        sh = buf_ref[s0:s0 + n_seq, SUBLANES - back:2 * SUBLANES - back, :].reshape(n_seq * DEC_SEQ, width)
        term = sh * w_ref[k:k + 1, :]
        out = term if out is None else out + term
    return out


def _advance(gen):
    try:
        next(gen)
        return True
    except StopIteration:
        return False


def _mix_kernel(*refs, prompt, rows, n_in, n_carry, steps=None, final=False):
    refs = refs[:n_in] + refs[n_in + n_carry:]
    if prompt:
        (x_ref, xn_ref, g1_ref, win_ref, wdt_ref, wa_ref, ba_ref, cw_ref, scw_ref, scb_ref, dtb_ref, alog_ref,
         dsk_ref, snorm_ref, wout_ref, g2_ref, w1_ref, w2_ref, gf_ref,
         xo_ref, vrow_ref, convo_ref, sconvo_ref, ssmo_ref,
         tailb_ref, tailc_ref, st_ref, proja_ref, projb_ref, hna_ref, hnb_ref, mixa_ref, mixb_ref,
         xm_ref, hff_ref, f_ref) = refs
    else:
        (x_ref, g1_ref, win_ref, wdt_ref, wa_ref, ba_ref, cw_ref, scw_ref, scb_ref, dtb_ref, alog_ref,
         dsk_ref, snorm_ref, wout_ref, conv0_ref, sconv0_ref, ssm0_ref,
         xo_ref, vrow_ref, convo_ref, sconvo_ref, ssmo_ref,
         cbufb_ref, cbufc_ref) = refs
    n_sub = rows // CHUNK
    seqs_per_sub = CHUNK // DEC_SEQ

    def in_proj(x_rows):
        h = _bf(_rmsnorm(x_rows, g1_ref[...]))
        return jnp.concatenate([_dot(h, win_ref[...]), _dot(h, wdt_ref[...])], axis=1)

    ups_left = [D_FF // FILL_COLS]

    def ff_up(k):
        ups_left[0] -= 1
        cs = slice(k * FILL_COLS, (k + 1) * FILL_COLS)
        f = jnp.maximum(_dot(hff_ref[...], w1_ref[:, cs]), 0.0)
        f_ref[:, cs] = _bf(f * f)

    if prompt:
        half = rows // 2
        subs_per_half = n_sub // 2
        n_fill = -(-D_IN_PAD // FILL_COLS)
        t = pl.program_id(0)
        j = t % steps
        live = t < pl.num_programs(0) - 1

        @pl.when(j == 0)
        def _():
            tailb_ref[...] = jnp.zeros(tailb_ref.shape, F32)
            tailc_ref[...] = jnp.zeros(tailc_ref.shape, F32)
            st_ref[...] = jnp.zeros(st_ref.shape, F32)

        @pl.when(t == 0)
        def _():
            proja_ref[...] = in_proj(x_ref[0:half, :])
            xm_ref[...] = jnp.zeros(xm_ref.shape, F32)
            hff_ref[...] = jnp.zeros(hff_ref.shape, BF16)

        for k in range(FFN_HEAD_START):
            ff_up(k)
        xo_ref[...] = xm_ref[...]
        hn_refs = (hna_ref, hnb_ref)
        hna_ref[...] = _bf(_rmsnorm(x_ref[half:rows, :], g1_ref[...]))
    else:
        cbufb_ref[:, SUBLANES - (CONV_B - 1):SUBLANES, :] = conv0_ref[...]
        cbufc_ref[:, SUBLANES - (CONV_C - 1):SUBLANES, :] = sconv0_ref[...]

    row = lax.broadcasted_iota(jnp.int32, (CHUNK, CHUNK), 0)
    col = lax.broadcasted_iota(jnp.int32, (CHUNK, CHUNK), 1)
    if prompt:
        causal = col <= row
    else:
        causal = ((row >> 3) == (col >> 3)) & (col <= row)
    causal01 = _bf(jnp.where(causal, 1.0, 0.0))
    erow = lax.broadcasted_iota(jnp.int32, (LANES, C_WIDTH), 0)
    ecol = lax.broadcasted_iota(jnp.int32, (LANES, C_WIDTH), 1)
    expand01 = _bf(jnp.where((ecol >> 6) == erow, 1.0, 0.0))

    def expand(a):
        a1 = _bf(a)
        a2 = _bf(a - a1.astype(F32))
        return _dot(a1, expand01) + _dot(a2, expand01)

    a_neg = -jnp.exp(alog_ref[...])
    if prompt:
        upper01 = _bf(jnp.where(row <= col, 1.0, 0.0))
        arow = lax.broadcasted_iota(jnp.int32, (CHUNK, A_HEADS * CHUNK), 0)
        acol = lax.broadcasted_iota(jnp.int32, (CHUNK, A_HEADS * CHUNK), 1)
        wa_b = _bf(jnp.where((acol & (CHUNK - 1)) <= arow, wa_ref[...], 0.0))
        hist = {"b": tailb_ref[...], "c": tailc_ref[...]}
        mix_refs = (mixa_ref, mixb_ref)

    fills = []
    points_left = [0]
    ffn_q = []
    step_points_left = [FILL_POINTS * n_sub]

    def fill_cols(dst, src, k):
        if k * FILL_COLS < COL_DT:
            cs = slice(k * FILL_COLS, (k + 1) * FILL_COLS)
            dst[:, cs] = _dot(src[...], win_ref[:, cs])
        else:
            dst[:, COL_DT:] = _dot(src[...], wdt_ref[...])

    def norm_next_step():
        hnb_ref[...] = _bf(_rmsnorm(xn_ref[...], g1_ref[...]))

    def norm_ffn(h):
        assert ups_left[0] == 0, "hff_ref is still being read by this step's up-projections"
        rs = slice(h * half, (h + 1) * half)
        hff_ref[rs, :] = _bf(_rmsnorm(xm_ref[rs, :], g2_ref[...]))

    def ff_down(kc, nc):
        ks = slice(kc * FF_COLS, (kc + 1) * FF_COLS)
        cs = slice(nc * FILL_COLS, (nc + 1) * FILL_COLS)
        xo_ref[:, cs] += _dot(f_ref[:, ks], w2_ref[ks, cs])

    def ff_final_norm():
        xo_ref[...] = _rmsnorm(xo_ref[...], gf_ref[...])

    if prompt:
        ffn_q.extend(functools.partial(ff_up, k) for k in range(FFN_HEAD_START, D_FF // FILL_COLS))
        for kc in range(D_FF // FF_COLS):
            ffn_q.extend(functools.partial(ff_down, kc, nc) for nc in range(D_MODEL // FILL_COLS))
        ffn_tail = ffn_q[-FFN_TAIL:]
        del ffn_q[-FFN_TAIL:]

    def fill():
        for queue, left in ((fills, points_left), (ffn_q, step_points_left)):
            n = -(-len(queue) // max(left[0], 1))
            left[0] -= 1
            for _ in range(min(n, len(queue))):
                queue.pop(0)()

    def sub_block(c):
        sl = slice(c * CHUNK, (c + 1) * CHUNK)
        x = x_ref[sl, :]
        if prompt:
            if c % subs_per_half == 0:
                h = c // subs_per_half
                dst = projb_ref if h == 0 else proja_ref
                fills.extend(functools.partial(fill_cols, dst, hn_refs[h], k) for k in range(n_fill))
                if h == 0:
                    fills.append(norm_next_step)
            pref = proja_ref if c < subs_per_half else projb_ref
            lr = slice((c % subs_per_half) * CHUNK, (c % subs_per_half + 1) * CHUNK)
            P = lambda c0, w, pref=pref, lr=lr: pref[lr, c0:c0 + w]
        else:
            proj = in_proj(x)
            P = lambda c0, w, proj=proj: proj[:, c0:c0 + w]

        if prompt:
            dt_t = _softplus(P(COL_DT, LANES).T[0:SSM_HEADS, :] + dtb_ref[...])
            a = dt_t * a_neg
            a1 = _bf(a).astype(F32)
            a2 = _bf(a - a1).astype(F32)
            r = _dot(_bf(jnp.concatenate([a1, a2, a - a1 - a2, jnp.zeros_like(a)], axis=0)), upper01)
            cum_t = r[0:8] + r[8:16] + r[16:24]
            expcum_t = jnp.exp(cum_t)
            wst_t = jnp.exp(cum_t[:, CHUNK - 1:CHUNK] - cum_t) * dt_t
            pack = jnp.concatenate([wst_t, expcum_t, cum_t,
                                    jnp.zeros((CHUNK - 3 * SSM_HEADS, CHUNK), F32)], axis=0)
            cols = pack.T

            def per_channel(lane0):
                tiles = []
                for p in range(SSM_HEADS // 2):
                    even = jnp.broadcast_to(cols[:, lane0 + 2 * p:lane0 + 2 * p + 1], (CHUNK, LANES))
                    odd = jnp.broadcast_to(cols[:, lane0 + 2 * p + 1:lane0 + 2 * p + 2], (CHUNK, LANES))
                    tiles.append(jnp.where(col < SSM_HEAD_DIM, even, odd))
                return jnp.concatenate(tiles, axis=1)

            w_state_x, expcum_x = per_channel(0), per_channel(SSM_HEADS)
            cum, cum_lane0 = cols, 2 * SSM_HEADS
        else:
            cum_lane0 = 0
            dt = _softplus(P(COL_DT, LANES) + dtb_ref[...])
            cum = _sel_left(causal01, dt * a_neg)
            cum3 = cum.reshape(seqs_per_sub, DEC_SEQ, LANES)
            tot = jnp.broadcast_to(cum3[:, DEC_SEQ - 1:DEC_SEQ, :], cum3.shape).reshape(CHUNK, LANES)
            expcum = jnp.exp(cum)
            w_state = jnp.exp(tot - cum) * dt
            ex = expand(jnp.concatenate([w_state, expcum], axis=0))
            w_state_x, expcum_x = ex[0:CHUNK], ex[CHUNK:]
            cum_t = cum.T
            dt_t = dt.T

        u = _gelu(P(COL_U, A_WIDTH))
        v = _gelu(P(COL_V, A_WIDTH))
        if prompt:
            pairs = []
            for p in range(A_HEADS // 2):
                rhs = _block_diag_pair(v[:, p * LANES:(p + 1) * LANES])
                pairs.append(_dot(wa_b[:, p * 2 * CHUNK:(p + 1) * 2 * CHUNK], rhs))
            s = jnp.concatenate(pairs, axis=1) + ba_ref[...]
            if c == n_sub - 1:
                hist["v"] = v
        else:
            s0 = c * seqs_per_sub
            v3 = v.reshape(seqs_per_sub, DEC_SEQ, A_WIDTH)
            trow = lax.broadcasted_iota(jnp.int32, (DEC_SEQ, A_WIDTH), 0)
            s3 = jnp.zeros((seqs_per_sub, DEC_SEQ, A_WIDTH), F32)
            for jj in range(DEC_SEQ):
                coef = jnp.where(trow >= jj, wa_ref[jj], 0.0)
                s3 = s3 + coef[None] * v3[:, jj:jj + 1, :]
            s = (s3 + ba_ref[0:DEC_SEQ, :][None]).reshape(CHUNK, A_WIDTH)
            vrow_ref[s0:s0 + seqs_per_sub] = v3
        ya = u * s
        fill()
        yield

        cg_in = P(COL_CG, B_WIDTH) * P(COL_HB, B_WIDTH)
        xbc_in = P(COL_XBC, SSM_CONV_DIM)
        if prompt:
            conv = _causal_taps_rows(cg_in, hist["b"], cw_ref, CONV_B)
            xbc = _causal_taps_rows(xbc_in, hist["c"], scw_ref, CONV_C)
            hist["b"] = cg_in[CHUNK - SUBLANES:, :]
            hist["c"] = xbc_in[CHUNK - SUBLANES:, :]
        else:
            cbufb_ref[s0:s0 + seqs_per_sub, SUBLANES:, :] = cg_in.reshape(seqs_per_sub, DEC_SEQ, B_WIDTH)
            cbufc_ref[s0:s0 + seqs_per_sub, SUBLANES:, :] = xbc_in.reshape(seqs_per_sub, DEC_SEQ, SSM_CONV_DIM)
            conv = _causal_taps_seqs(cbufb_ref, s0, seqs_per_sub, cw_ref, CONV_B, B_WIDTH)
            xbc = _causal_taps_seqs(cbufc_ref, s0, seqs_per_sub, scw_ref, CONV_C, SSM_CONV_DIM)
        yb = P(COL_BG, B_WIDTH) * conv
        fill()
        yield

        xbc = _silu(xbc + scb_ref[...])
        xs = xbc[:, 0:C_WIDTH]
        bm = xbc[:, C_WIDTH:C_WIDTH + SSM_GROUPS * D_STATE]
        cm = xbc[:, C_WIDTH + SSM_GROUPS * D_STATE:]
        xw = xs * w_state_x
        bm_b = _bf(bm)
        cm_b = _bf(cm)
        fill()
        yield

        zero_blk = jnp.zeros((CHUNK, D_STATE), BF16)
        b_diag = jnp.concatenate([jnp.concatenate([bm_b[:, 0:D_STATE], zero_blk], axis=1),
                                  jnp.concatenate([zero_blk, bm_b[:, D_STATE:]], axis=1)], axis=0)
        cb_all = _dot_nt(cm_b, b_diag)
        y_pairs = []
        for g in range(SSM_GROUPS):
            gs = slice(g * D_STATE, (g + 1) * D_STATE)
            cb = cb_all[:, gs]
            wts = []
            for kk in range(HEADS_PER_GROUP):
                k = g * HEADS_PER_GROUP + kk
                seg = cum[:, cum_lane0 + k:cum_lane0 + k + 1] - cum_t[k:k + 1, :]
                decay = jnp.exp(jnp.where(causal, seg, -jnp.inf))
                wts.append(_bf(cb * decay * dt_t[k:k + 1, :]))
            for p in range(HEADS_PER_GROUP // 2):
                k0 = g * HEADS_PER_GROUP + 2 * p
                lhs = jnp.concatenate([wts[2 * p], wts[2 * p + 1]], axis=1)
                rhs = _block_diag_pair(xs[:, k0 * SSM_HEAD_DIM:(k0 + 2) * SSM_HEAD_DIM])
                y_pairs.append(_dot(lhs, rhs))
            fill()
            yield
        y = jnp.concatenate(y_pairs, axis=1)

        if prompt:
            xw_b = _bf(xw)
            y_off = []
            for g in range(SSM_GROUPS):
                gs = slice(g * D_STATE, (g + 1) * D_STATE)
                gw = slice(g * GROUP_W, (g + 1) * GROUP_W)
                st_g = st_ref[:, gw]
                y_off.append(_dot(cm_b[:, gs], _bf(st_g)))
                st_ref[:, gw] = expcum_x[CHUNK - 1:CHUNK, gw] * st_g + _dot(_bf(bm[:, gs].T), xw_b[:, gw])
            y = y + jnp.concatenate(y_off, axis=1) * expcum_x
        else:
            exptot = jnp.exp(tot)
            xw_t = [_bf(xw[:, g * GROUP_W:(g + 1) * GROUP_W].T) for g in range(SSM_GROUPS)]
            y_off = []
            for i in range(seqs_per_sub):
                rs = slice(i * DEC_SEQ, (i + 1) * DEC_SEQ)
                own_rows = (row >> 3) == i
                parts = []
                for g in range(SSM_GROUPS):
                    gs = slice(g * D_STATE, (g + 1) * D_STATE)
                    gw = slice(g * GROUP_W, (g + 1) * GROUP_W)
                    h0 = ssm0_ref[s0 + i, gw, :]
                    parts.append(_dot_nt(_bf(cm[rs, gs]), _bf(h0)))
                    upd = _dot(xw_t[g], _bf(jnp.where(own_rows, bm[:, gs], 0.0)))
                    for kk in range(HEADS_PER_GROUP):
                        k = g * HEADS_PER_GROUP + kk
                        hs = slice(kk * SSM_HEAD_DIM, (kk + 1) * SSM_HEAD_DIM)
                        dec = jnp.broadcast_to(exptot[i * DEC_SEQ:i * DEC_SEQ + 1, k:k + 1],
                                               (SSM_HEAD_DIM, D_STATE))
                        ssmo_ref[s0 + i, k * SSM_HEAD_DIM:(k + 1) * SSM_HEAD_DIM, :] = (
                            dec * h0[hs, :] + upd[hs, :])
                y_off.append(jnp.concatenate(parts, axis=1))
            y = y + jnp.concatenate(y_off, axis=0) * expcum_x

        y = y + dsk_ref[...] * xs
        yc = _rmsnorm(y * _silu(P(COL_Z, C_WIDTH)), snorm_ref[...])
        fill()
        yield
        mix = jnp.concatenate([_bf(ya), _bf(yb), _bf(yc)], axis=1)
        if prompt:
            mix_refs[c // subs_per_half][lr, :] = mix
        else:
            xo_ref[sl, :] = x + _dot(mix, wout_ref[...])

    def out_cols(h, k):
        rs = slice(h * half, (h + 1) * half)
        cs = slice(k * FILL_COLS, (k + 1) * FILL_COLS)
        xm_ref[rs, cs] = x_ref[rs, cs] + _dot(mix_refs[h][...], wout_ref[:, cs])

    group = subs_per_half if prompt else 1
    for c0 in range(0, n_sub, group):
        if prompt:
            points_left[0] = FILL_POINTS * group
        gens = [sub_block(c) for c in range(c0, c0 + group)]
        while gens:
            gens = [g for g in gens if _advance(g)]
        while fills:
            fills.pop(0)()
        if prompt:
            h = c0 // group
            outs = [functools.partial(out_cols, h, k) for k in range(D_MODEL // FILL_COLS)]
            outs.append(functools.partial(norm_ffn, h))
            if c0 + group < n_sub:
                fills.extend(outs)
            else:
                while ffn_q:
                    ffn_q.pop(0)()
                for task in outs:
                    task()
                    if ffn_tail:
                        ffn_tail.pop(0)()
                while ffn_tail:
                    ffn_tail.pop(0)()
                if final:
                    ff_final_norm()

    if prompt:
        tailb_ref[...] = hist["b"]
        tailc_ref[...] = hist["c"]

        @pl.when(live)
        def _():
            vrow_ref[...] = hist["v"]
            convo_ref[...] = hist["b"][SUBLANES - (CONV_B - 1):, :]
            sconvo_ref[...] = hist["c"][SUBLANES - (CONV_C - 1):, :]

        @pl.when(live & (j == steps - 1))
        def _():
            ssmo_ref[...] = st_ref[...].T
    else:
        convo_ref[...] = cbufb_ref[:, 2 * SUBLANES - (CONV_B - 1):, :]
        sconvo_ref[...] = cbufc_ref[:, 2 * SUBLANES - (CONV_C - 1):, :]


def _ff_kernel(x_ref, g2_ref, w1_ref, w2_ref, gf_ref, o_ref, f_ref, *, final):
    x = x_ref[...]
    h = _bf(_rmsnorm(x, g2_ref[...]))
    for c in range(D_FF // FF_COLS):
        cs = slice(c * FF_COLS, (c + 1) * FF_COLS)
        f = jnp.maximum(_dot(h, w1_ref[:, cs]), 0.0)
        f_ref[:, cs] = _bf(f * f)
    y = x + _dot(f_ref[...], w2_ref[...])
    if final:
        y = _rmsnorm(y, gf_ref[...])
    o_ref[...] = y


def _layer_spec(shape, layer, n_grid):
    zeros = (0,) * len(shape)
    if n_grid == 1:
        index_map = lambda i: (layer,) + zeros
    else:
        index_map = lambda i, j: (layer,) + zeros
    return pl.BlockSpec((None,) + tuple(shape), index_map, pipeline_mode=pl.Buffered(1))


def _mix_call(prompt, layer, x2d, params, states=None, carry=None, seq=None, final=False):
    n_rows = x2d.shape[0]
    steps = None
    if prompt:
        rows = PROMPT_ROWS
        n_batch = n_rows // seq
        steps = seq // rows
        n_blocks = n_rows // rows
        grid = (n_blocks + 1,)
        xmap = lambda t: (jnp.minimum(t, n_blocks - 1), 0)
        omap = lambda t: (jnp.maximum(t - 1, 0), 0)
        smap = lambda t: (layer, jnp.minimum(t // steps, n_batch - 1), 0, 0)
        sblk = (None, None)
        state_rows = (CHUNK, CONV_B - 1, CONV_C - 1, C_WIDTH)
    else:
        rows = SAMPLE_SEQS * DEC_SEQ
        n_batch = n_rows // DEC_SEQ
        grid = (n_batch // SAMPLE_SEQS,)
        xmap = omap = lambda i: (i, 0)
        smap = lambda i: (layer, i, 0, 0)
        sblk = (None, SAMPLE_SEQS)
        state_rows = (DEC_SEQ, CONV_B - 1, CONV_C - 1, C_WIDTH)
    state_cols = (A_WIDTH, B_WIDTH, SSM_CONV_DIM, D_STATE)

    par_specs = [_layer_spec(p.shape[1:], layer, 1) if p.ndim > 2 else
                 pl.BlockSpec(p.shape, lambda i: (0, 0), pipeline_mode=pl.Buffered(1)) for p in params]
    in_specs = [pl.BlockSpec((rows, D_MODEL), xmap)] + par_specs
    args = [x2d] + list(params)
    if prompt:
        half = rows // 2
        last_half = n_rows // half - 1
        nmap = lambda t: (jnp.minimum(2 * t + 2, last_half), 0)
        in_specs.insert(1, pl.BlockSpec((half, D_MODEL), nmap))
        args.insert(1, x2d)
        scratch = [
            pltpu.VMEM((SUBLANES, B_WIDTH), F32),
            pltpu.VMEM((SUBLANES, SSM_CONV_DIM), F32),
            pltpu.VMEM((D_STATE, C_WIDTH), F32),
            pltpu.VMEM((half, D_IN_PAD), F32),
            pltpu.VMEM((half, D_IN_PAD), F32),
            pltpu.VMEM((half, D_MODEL), BF16),
            pltpu.VMEM((half, D_MODEL), BF16),
            pltpu.VMEM((half, D_MODEL), BF16),
            pltpu.VMEM((half, D_MODEL), BF16),
            pltpu.VMEM((rows, D_MODEL), F32),
            pltpu.VMEM((rows, D_MODEL), BF16),
            pltpu.VMEM((rows, D_FF), BF16),
        ]
        sem = ("arbitrary",)
    else:
        in_specs += [pl.BlockSpec(sblk + (r, w), smap)
                     for r, w in zip(state_rows[1:], state_cols[1:])]
        args += list(states)
        scratch = [
            pltpu.VMEM((SAMPLE_SEQS, 2 * SUBLANES, B_WIDTH), F32),
            pltpu.VMEM((SAMPLE_SEQS, 2 * SUBLANES, SSM_CONV_DIM), F32),
        ]
        sem = ("parallel",)

    out_shape = [jax.ShapeDtypeStruct((n_rows, D_MODEL), F32)]
    out_specs = [pl.BlockSpec((rows, D_MODEL), omap)]
    for r, w in zip(state_rows, state_cols):
        out_shape.append(jax.ShapeDtypeStruct((DEPTH, n_batch, r, w), F32))
        out_specs.append(pl.BlockSpec(sblk + (r, w), smap))
    n_in = len(args)
    aliases = {}
    if carry is not None:
        in_specs += [pl.BlockSpec(memory_space=pl.ANY)] * len(carry)
        args += list(carry)
        aliases = {n_in + k: 1 + k for k in range(len(carry))}

    return pl.pallas_call(
        functools.partial(_mix_kernel, prompt=prompt, rows=rows, n_in=n_in, n_carry=len(aliases),
                          steps=steps, final=final),
        grid=grid,
        in_specs=in_specs,
        out_specs=out_specs,
        out_shape=out_shape,
        scratch_shapes=scratch,
        input_output_aliases=aliases,
        compiler_params=pltpu.CompilerParams(dimension_semantics=sem, vmem_limit_bytes=VMEM_LIMIT),
        name="mix_prompt" if prompt else "mix_sample",
    )(*args)


def _ff_call(layer, x2d, g2, w1, w2, gf, final):
    n_rows = x2d.shape[0]
    return pl.pallas_call(
        functools.partial(_ff_kernel, final=final),
        grid=(n_rows // FF_ROWS,),
        in_specs=[
            pl.BlockSpec((FF_ROWS, D_MODEL), lambda i: (i, 0)),
            _layer_spec(g2.shape[1:], layer, 1),
            _layer_spec(w1.shape[1:], layer, 1),
            _layer_spec(w2.shape[1:], layer, 1),
            pl.BlockSpec(gf.shape, lambda i: (0, 0), pipeline_mode=pl.Buffered(1)),
        ],
        out_specs=pl.BlockSpec((FF_ROWS, D_MODEL), lambda i: (i, 0)),
        out_shape=jax.ShapeDtypeStruct((n_rows, D_MODEL), F32),
        scratch_shapes=[pltpu.VMEM((FF_ROWS, D_FF), BF16)],
        compiler_params=pltpu.CompilerParams(dimension_semantics=("parallel",),
                                             vmem_limit_bytes=VMEM_LIMIT),
        name="ffn",
    )(x2d, g2, w1, w2, gf)


def kernel(x_prompt, x_sample, state_conv, state_ssm_conv, state_ssm, norm1, w_in, w_s, b_s, conv_w,
           ssm_conv_w, ssm_conv_b, dt_bias, a_log, d_skip, ssm_norm, w_out, norm2, w_ff1, w_ff2,
           final_norm):
    bp, seq, _ = x_prompt.shape
    bs, dseq, _ = x_sample.shape
    assert seq % PROMPT_ROWS == 0 and dseq == DEC_SEQ and bs % SAMPLE_SEQS == 0

    wdt_b = _bf(jnp.pad(w_in[:, :, COL_DT:], ((0, 0), (0, 0), (0, D_IN_PAD - D_IN))))
    win_all, wout_all, w1_all, w2_all = _bf(w_in[:, :, :COL_DT]), _bf(w_out), _bf(w_ff1), _bf(w_ff2)
    g1 = norm1.reshape(DEPTH, 1, D_MODEL)
    g2 = norm2.reshape(DEPTH, 1, D_MODEL)
    gf = final_norm.reshape(1, D_MODEL)
    wa_prompt = jnp.transpose(w_s, (0, 2, 1, 3)).reshape(DEPTH, CHUNK, A_HEADS * CHUNK)
    wa_sample = jnp.repeat(jnp.transpose(w_s[:, :, :DEC_SEQ, :DEC_SEQ], (0, 3, 2, 1)),
                           A_WIDTH // A_HEADS, axis=-1)
    ba = jnp.repeat(jnp.transpose(b_s, (0, 2, 1)), A_WIDTH // A_HEADS, axis=-1)
    scb = ssm_conv_b.reshape(DEPTH, 1, SSM_CONV_DIM)
    pad_heads = lambda p: jnp.pad(p, ((0, 0), (0, LANES - SSM_HEADS))).reshape(DEPTH, 1, LANES)
    dtb = pad_heads(dt_bias)
    alog = pad_heads(a_log)
    dsk = jnp.repeat(d_skip, SSM_HEAD_DIM, axis=-1).reshape(DEPTH, 1, C_WIDTH)
    snorm = ssm_norm.reshape(DEPTH, 1, C_WIDTH)
    over_time = lambda p: jnp.broadcast_to(p[:, :, None], (DEPTH, SSM_HEADS, CHUNK))
    mid_prompt = (wa_prompt, ba, conv_w, ssm_conv_w, scb, over_time(dt_bias), over_time(a_log), dsk, snorm)
    mid_sample = (wa_sample, ba, conv_w, ssm_conv_w, scb, dtb, alog, dsk, snorm)

    xp = x_prompt.reshape(bp * seq, D_MODEL)
    xs = x_sample.reshape(bs * dseq, D_MODEL)
    ssm_in = state_ssm.reshape(DEPTH, bs, C_WIDTH, D_STATE)
    st_p = st_s = None
    for l in range(DEPTH):
        final = l == DEPTH - 1
        win_b, wout_b, w1_b, w2_b = win_all, wout_all, w1_all, w2_all
        xp, *st_p = _mix_call(True, l, xp, (g1, win_b, wdt_b) + mid_prompt + (wout_b, g2, w1_b, w2_b, gf),
                              carry=st_p, seq=seq, final=final)
        xs, *st_s = _mix_call(False, l, xs, (g1, win_b, wdt_b) + mid_sample + (wout_b,),
                              (state_conv, state_ssm_conv, ssm_in), carry=st_s)
        xs = _ff_call(l, xs, g2, w1_b, w2_b, gf, final)

    y_prompt = xp.reshape(bp, seq, D_MODEL)
    y_sample = xs.reshape(bs, dseq, D_MODEL)
    ssm_shape = lambda b: (DEPTH, b, SSM_HEADS, SSM_HEAD_DIM, D_STATE)
    return (y_prompt, y_sample,
            st_p[0], st_p[1], st_p[2], st_p[3].reshape(ssm_shape(bp)),
            st_s[0], st_s[1], st_s[2], st_s[3].reshape(ssm_shape(bs)))
```

```python
import functools

import jax
import jax.numpy as jnp
from jax import lax
from jax.experimental import pallas as pl
from jax.experimental.pallas import tpu as pltpu

D_MODEL = 1024
DEPTH = 4
A_WIDTH = 256
A_HEADS = 4
B_WIDTH = 256
C_WIDTH = 512
CONV_B = 3
CONV_C = 4
SSM_HEADS = 8
SSM_HEAD_DIM = 64
SSM_GROUPS = 2
HEADS_PER_GROUP = SSM_HEADS // SSM_GROUPS
D_STATE = 128
SSM_CONV_DIM = 1024
CHUNK = 128
D_FF = 4096
EPS = 1e-5
DEC_SEQ = 8

COL_U, COL_V, COL_BG, COL_CG, COL_HB, COL_Z, COL_XBC, COL_DT = 0, 256, 512, 768, 1024, 1280, 1792, 2816
D_IN = 2824
LANES = 128
SUBLANES = 8
D_IN_PAD = 2944
GROUP_W = C_WIDTH // SSM_GROUPS

PROMPT_ROWS = 512
SAMPLE_SEQS = 16
FILL_COLS = 256
FILL_POINTS = 6
FFN_HEAD_START = 2
FFN_TAIL = 6
FF_ROWS = 1024
FF_COLS = 1024
VMEM_LIMIT = 56 * 1024 * 1024

F32 = jnp.float32
BF16 = jnp.bfloat16


def _bf(x):
    return x.astype(BF16)


def _dot(a, b):
    return jnp.dot(a, b, preferred_element_type=F32)


def _dot_nt(a, b):
    return lax.dot_general(a, b, (((1,), (1,)), ((), ())), preferred_element_type=F32)


def _split3(a):
    a1 = _bf(a)
    r1 = a - a1.astype(F32)
    a2 = _bf(r1)
    r2 = r1 - a2.astype(F32)
    return a1, a2, _bf(r2)


def _sel_left(m01, a):
    a1, a2, a3 = _split3(a)
    return _dot(m01, a1) + _dot(m01, a2) + _dot(m01, a3)


def _rmsnorm(x, g):
    ms = jnp.mean(x * x, axis=-1, keepdims=True)
    return (x * lax.rsqrt(ms + EPS)) * g


def _gelu(x):
    return 0.5 * x * (1.0 + lax.erf(x * (0.5 ** 0.5)))


def _silu(x):
    return x * jax.nn.sigmoid(x)


def _softplus(x):
    return jnp.maximum(x, 0.0) + jnp.log1p(jnp.exp(-jnp.abs(x)))


def _block_diag_pair(t):
    lane = lax.broadcasted_iota(jnp.int32, t.shape, 1)
    lo = jnp.where(lane < SSM_HEAD_DIM, t, 0.0)
    hi = jnp.where(lane >= SSM_HEAD_DIM, t, 0.0)
    return _bf(jnp.concatenate([lo, hi], axis=0))


def _causal_taps_rows(xin, prev8, w_ref, n_taps):
    row8 = lax.broadcasted_iota(jnp.int32, prev8.shape, 0)
    out = None
    for k in range(n_taps):
        back = n_taps - 1 - k
        if back == 0:
            sh = xin
        else:
            r = pltpu.roll(xin, back, axis=0)
            first = jnp.where(row8 >= back, r[0:SUBLANES], pltpu.roll(prev8, back, axis=0))
            sh = jnp.concatenate([first, r[SUBLANES:]], axis=0)
        term = sh * w_ref[k:k + 1, :]
        out = term if out is None else out + term
    return out


def _causal_taps_seqs(buf_ref, s0, n_seq, w_ref, n_taps, width):
    out = None
    for k in range(n_taps):
        back = n_taps - 1 - k
        sh = buf_ref[s0:s0 + n_seq, SUBLANES - back:2 * SUBLANES - back, :].reshape(n_seq * DEC_SEQ, width)
        term = sh * w_ref[k:k + 1, :]
        out = term if out is None else out + term
    return out


def _advance(gen):
    try:
        next(gen)
        return True
    except StopIteration:
        return False


def _mix_kernel(*refs, prompt, rows, n_in, n_carry, steps=None, final=False):
    refs = refs[:n_in] + refs[n_in + n_carry:]
    if prompt:
        (x_ref, xn_ref, g1_ref, win_ref, wdt_ref, wa_ref, ba_ref, cw_ref, scw_ref, scb_ref, dtb_ref, alog_ref,
         dsk_ref, snorm_ref, wout_ref, g2_ref, w1_ref, w2_ref, gf_ref,
         xo_ref, vrow_ref, convo_ref, sconvo_ref, ssmo_ref,
         tailb_ref, tailc_ref, st_ref, proja_ref, projb_ref, dtta_ref, dttb_ref, hna_ref, hnb_ref,
         mixa_ref, mixb_ref,
         xm_ref, hff_ref, f_ref) = refs
    else:
        (x_ref, g1_ref, win_ref, wdt_ref, wa_ref, ba_ref, cw_ref, scw_ref, scb_ref, dtb_ref, alog_ref,
         dsk_ref, snorm_ref, wout_ref, conv0_ref, sconv0_ref, ssm0_ref,
         xo_ref, vrow_ref, convo_ref, sconvo_ref, ssmo_ref,
         cbufb_ref, cbufc_ref) = refs
    n_sub = rows // CHUNK
    seqs_per_sub = CHUNK // DEC_SEQ

    def in_proj(x_rows):
        h = _bf(_rmsnorm(x_rows, g1_ref[...]))
        return jnp.concatenate([_dot(h, win_ref[:, 0:COL_DT]), _dot(h, wdt_ref[...])], axis=1)

    ups_left = [D_FF // FILL_COLS]

    def ff_up(k):
        ups_left[0] -= 1
        cs = slice(k * FILL_COLS, (k + 1) * FILL_COLS)
        f = jnp.maximum(_dot(hff_ref[...], w1_ref[:, cs]), 0.0)
        f_ref[:, cs] = _bf(f * f)

    if prompt:
        half = rows // 2
        subs_per_half = n_sub // 2
        n_fill = -(-D_IN_PAD // FILL_COLS)
        t = pl.program_id(0)
        j = t % steps
        live = t < pl.num_programs(0) - 1

        @pl.when(j == 0)
        def _():
            tailb_ref[...] = jnp.zeros(tailb_ref.shape, F32)
            tailc_ref[...] = jnp.zeros(tailc_ref.shape, F32)
            st_ref[...] = jnp.zeros(st_ref.shape, F32)

        @pl.when(t == 0)
        def _():
            h0 = _bf(_rmsnorm(x_ref[0:half, :], g1_ref[...]))
            proja_ref[...] = _dot(h0, win_ref[:, 0:COL_DT])
            dtta_ref[...] = _dot_nt(wdt_ref[...], h0)
            xm_ref[...] = jnp.zeros(xm_ref.shape, F32)
            hff_ref[...] = jnp.zeros(hff_ref.shape, BF16)

        for k in range(FFN_HEAD_START):
            ff_up(k)
        xo_ref[...] = xm_ref[...]
        hn_refs = (hna_ref, hnb_ref)
        hna_ref[...] = _bf(_rmsnorm(x_ref[half:rows, :], g1_ref[...]))
    else:
        cbufb_ref[:, SUBLANES - (CONV_B - 1):SUBLANES, :] = conv0_ref[...]
        cbufc_ref[:, SUBLANES - (CONV_C - 1):SUBLANES, :] = sconv0_ref[...]

    row = lax.broadcasted_iota(jnp.int32, (CHUNK, CHUNK), 0)
    col = lax.broadcasted_iota(jnp.int32, (CHUNK, CHUNK), 1)
    if prompt:
        causal = col <= row
    else:
        causal = ((row >> 3) == (col >> 3)) & (col <= row)
    causal01 = _bf(jnp.where(causal, 1.0, 0.0))
    erow = lax.broadcasted_iota(jnp.int32, (LANES, C_WIDTH), 0)
    ecol = lax.broadcasted_iota(jnp.int32, (LANES, C_WIDTH), 1)
    expand01 = _bf(jnp.where((ecol >> 6) == erow, 1.0, 0.0))

    def expand(a):
        a1 = _bf(a)
        a2 = _bf(a - a1.astype(F32))
        return _dot(a1, expand01) + _dot(a2, expand01)

    a_neg = -jnp.exp(alog_ref[...])
    if prompt:
        upper01 = _bf(jnp.where(row <= col, 1.0, 0.0))
        arow = lax.broadcasted_iota(jnp.int32, (CHUNK, A_HEADS * CHUNK), 0)
        acol = lax.broadcasted_iota(jnp.int32, (CHUNK, A_HEADS * CHUNK), 1)
        wa_b = _bf(jnp.where((acol & (CHUNK - 1)) <= arow, wa_ref[...], 0.0))
        hist = {"b": tailb_ref[...], "c": tailc_ref[...]}
        mix_refs = (mixa_ref, mixb_ref)

    fills = []
    points_left = [0]
    ffn_q = []
    step_points_left = [FILL_POINTS * n_sub]

    def fill_cols(dst, src, k):
        if k * FILL_COLS < COL_DT:
            cs = slice(k * FILL_COLS, (k + 1) * FILL_COLS)
            dst[:, cs] = _dot(src[...], win_ref[:, cs])
        else:
            dtt = dttb_ref if dst is projb_ref else dtta_ref
            dtt[...] = _dot_nt(wdt_ref[...], src[...])

    def norm_next_step():
        hnb_ref[...] = _bf(_rmsnorm(xn_ref[...], g1_ref[...]))

    def norm_ffn(h):
        assert ups_left[0] == 0, "hff_ref is still being read by this step's up-projections"
        rs = slice(h * half, (h + 1) * half)
        hff_ref[rs, :] = _bf(_rmsnorm(xm_ref[rs, :], g2_ref[...]))

    def ff_down(kc, nc):
        ks = slice(kc * FF_COLS, (kc + 1) * FF_COLS)
        cs = slice(nc * FILL_COLS, (nc + 1) * FILL_COLS)
        xo_ref[:, cs] += _dot(f_ref[:, ks], w2_ref[ks, cs])

    def ff_final_norm():
        xo_ref[...] = _rmsnorm(xo_ref[...], gf_ref[...])

    if prompt:
        ffn_q.extend(functools.partial(ff_up, k) for k in range(FFN_HEAD_START, D_FF // FILL_COLS))
        for kc in range(D_FF // FF_COLS):
            ffn_q.extend(functools.partial(ff_down, kc, nc) for nc in range(D_MODEL // FILL_COLS))
        ffn_tail = ffn_q[-FFN_TAIL:]
        del ffn_q[-FFN_TAIL:]

    def fill():
        for queue, left in ((fills, points_left), (ffn_q, step_points_left)):
            n = -(-len(queue) // max(left[0], 1))
            left[0] -= 1
            for _ in range(min(n, len(queue))):
                queue.pop(0)()

    def sub_block(c):
        sl = slice(c * CHUNK, (c + 1) * CHUNK)
        x = x_ref[sl, :]
        if prompt:
            if c % subs_per_half == 0:
                h = c // subs_per_half
                dst = projb_ref if h == 0 else proja_ref
                fills.extend(functools.partial(fill_cols, dst, hn_refs[h], k) for k in range(n_fill))
                if h == 0:
                    fills.append(norm_next_step)
            pref = proja_ref if c < subs_per_half else projb_ref
            lr = slice((c % subs_per_half) * CHUNK, (c % subs_per_half + 1) * CHUNK)
            P = lambda c0, w, pref=pref, lr=lr: pref[lr, c0:c0 + w]
        else:
            proj = in_proj(x)
            P = lambda c0, w, proj=proj: proj[:, c0:c0 + w]

        if prompt:
            dtt = dtta_ref if c < subs_per_half else dttb_ref
            dt_t = _softplus(dtt[:, lr] + dtb_ref[...])
            a = dt_t * a_neg
            a1 = _bf(a).astype(F32)
            a2 = _bf(a - a1).astype(F32)
            r = _dot(_bf(jnp.concatenate([a1, a2, a - a1 - a2, jnp.zeros_like(a)], axis=0)), upper01)
            cum_t = r[0:8] + r[8:16] + r[16:24]
            expcum_t = jnp.exp(cum_t)
            wst_t = jnp.exp(cum_t[:, CHUNK - 1:CHUNK] - cum_t) * dt_t
            pack = jnp.concatenate([wst_t, expcum_t, cum_t,
                                    jnp.zeros((CHUNK - 3 * SSM_HEADS, CHUNK), F32)], axis=0)
            cols = pack.T

            def per_channel(lane0):
                tiles = []
                for p in range(SSM_HEADS // 2):
                    even = jnp.broadcast_to(cols[:, lane0 + 2 * p:lane0 + 2 * p + 1], (CHUNK, LANES))
                    odd = jnp.broadcast_to(cols[:, lane0 + 2 * p + 1:lane0 + 2 * p + 2], (CHUNK, LANES))
                    tiles.append(jnp.where(col < SSM_HEAD_DIM, even, odd))
                return jnp.concatenate(tiles, axis=1)

            w_state_x, expcum_x = per_channel(0), per_channel(SSM_HEADS)
            cum, cum_lane0 = cols, 2 * SSM_HEADS
        else:
            cum_lane0 = 0
            dt = _softplus(P(COL_DT, LANES) + dtb_ref[...])
            cum = _sel_left(causal01, dt * a_neg)
            cum3 = cum.reshape(seqs_per_sub, DEC_SEQ, LANES)
            tot = jnp.broadcast_to(cum3[:, DEC_SEQ - 1:DEC_SEQ, :], cum3.shape).reshape(CHUNK, LANES)
            expcum = jnp.exp(cum)
            w_state = jnp.exp(tot - cum) * dt
            ex = expand(jnp.concatenate([w_state, expcum], axis=0))
            w_state_x, expcum_x = ex[0:CHUNK], ex[CHUNK:]
            cum_t = cum.T
            dt_t = dt.T

        u = _gelu(P(COL_U, A_WIDTH))
        v = _gelu(P(COL_V, A_WIDTH))
        if prompt:
            pairs = []
            for p in range(A_HEADS // 2):
                rhs = _block_diag_pair(v[:, p * LANES:(p + 1) * LANES])
                pairs.append(_dot(wa_b[:, p * 2 * CHUNK:(p + 1) * 2 * CHUNK], rhs))
            s = jnp.concatenate(pairs, axis=1) + ba_ref[...]
            if c == n_sub - 1:
                hist["v"] = v
        else:
            s0 = c * seqs_per_sub
            v3 = v.reshape(seqs_per_sub, DEC_SEQ, A_WIDTH)
            trow = lax.broadcasted_iota(jnp.int32, (DEC_SEQ, A_WIDTH), 0)
            s3 = jnp.zeros((seqs_per_sub, DEC_SEQ, A_WIDTH), F32)
            for jj in range(DEC_SEQ):
                coef = jnp.where(trow >= jj, wa_ref[jj], 0.0)
                s3 = s3 + coef[None] * v3[:, jj:jj + 1, :]
            s = (s3 + ba_ref[0:DEC_SEQ, :][None]).reshape(CHUNK, A_WIDTH)
            vrow_ref[s0:s0 + seqs_per_sub] = v3
        ya = u * s
        fill()
        yield

        cg_in = P(COL_CG, B_WIDTH) * P(COL_HB, B_WIDTH)
        xbc_in = P(COL_XBC, SSM_CONV_DIM)
        if prompt:
            conv = _causal_taps_rows(cg_in, hist["b"], cw_ref, CONV_B)
            xbc = _causal_taps_rows(xbc_in, hist["c"], scw_ref, CONV_C)
            hist["b"] = cg_in[CHUNK - SUBLANES:, :]
            hist["c"] = xbc_in[CHUNK - SUBLANES:, :]
        else:
            cbufb_ref[s0:s0 + seqs_per_sub, SUBLANES:, :] = cg_in.reshape(seqs_per_sub, DEC_SEQ, B_WIDTH)
            cbufc_ref[s0:s0 + seqs_per_sub, SUBLANES:, :] = xbc_in.reshape(seqs_per_sub, DEC_SEQ, SSM_CONV_DIM)
            conv = _causal_taps_seqs(cbufb_ref, s0, seqs_per_sub, cw_ref, CONV_B, B_WIDTH)
            xbc = _causal_taps_seqs(cbufc_ref, s0, seqs_per_sub, scw_ref, CONV_C, SSM_CONV_DIM)
        yb = P(COL_BG, B_WIDTH) * conv
        fill()
        yield

        xbc = _silu(xbc + scb_ref[...])
        xs = xbc[:, 0:C_WIDTH]
        bm = xbc[:, C_WIDTH:C_WIDTH + SSM_GROUPS * D_STATE]
        cm = xbc[:, C_WIDTH + SSM_GROUPS * D_STATE:]
        xw = xs * w_state_x
        bm_b = _bf(bm)
        cm_b = _bf(cm)
        fill()
        yield

        zero_blk = jnp.zeros((CHUNK, D_STATE), BF16)
        b_diag = jnp.concatenate([jnp.concatenate([bm_b[:, 0:D_STATE], zero_blk], axis=1),
                                  jnp.concatenate([zero_blk, bm_b[:, D_STATE:]], axis=1)], axis=0)
        cb_all = _dot_nt(cm_b, b_diag)
        y_pairs = []
        for g in range(SSM_GROUPS):
            gs = slice(g * D_STATE, (g + 1) * D_STATE)
            cb = cb_all[:, gs]
            wts = []
            for kk in range(HEADS_PER_GROUP):
                k = g * HEADS_PER_GROUP + kk
                seg = cum[:, cum_lane0 + k:cum_lane0 + k + 1] - cum_t[k:k + 1, :]
                decay = jnp.exp(jnp.where(causal, seg, -jnp.inf))
                wts.append(_bf(cb * decay * dt_t[k:k + 1, :]))
            for p in range(HEADS_PER_GROUP // 2):
                k0 = g * HEADS_PER_GROUP + 2 * p
                lhs = jnp.concatenate([wts[2 * p], wts[2 * p + 1]], axis=1)
                rhs = _block_diag_pair(xs[:, k0 * SSM_HEAD_DIM:(k0 + 2) * SSM_HEAD_DIM])
                y_pairs.append(_dot(lhs, rhs))
            fill()
            yield
        y = jnp.concatenate(y_pairs, axis=1)

        if prompt:
            xw_b = _bf(xw)
            y_off = []
            for g in range(SSM_GROUPS):
                gs = slice(g * D_STATE, (g + 1) * D_STATE)
                gw = slice(g * GROUP_W, (g + 1) * GROUP_W)
                st_g = st_ref[:, gw]
                y_off.append(_dot(cm_b[:, gs], _bf(st_g)))
                st_ref[:, gw] = expcum_x[CHUNK - 1:CHUNK, gw] * st_g + _dot(_bf(bm[:, gs].T), xw_b[:, gw])
            y = y + jnp.concatenate(y_off, axis=1) * expcum_x
        else:
            exptot = jnp.exp(tot)
            xw_t = [_bf(xw[:, g * GROUP_W:(g + 1) * GROUP_W].T) for g in range(SSM_GROUPS)]
            y_off = []
            for i in range(seqs_per_sub):
                rs = slice(i * DEC_SEQ, (i + 1) * DEC_SEQ)
                own_rows = (row >> 3) == i
                parts = []
                for g in range(SSM_GROUPS):
                    gs = slice(g * D_STATE, (g + 1) * D_STATE)
                    gw = slice(g * GROUP_W, (g + 1) * GROUP_W)
                    h0 = ssm0_ref[s0 + i, gw, :]
                    parts.append(_dot_nt(_bf(cm[rs, gs]), _bf(h0)))
                    upd = _dot(xw_t[g], _bf(jnp.where(own_rows, bm[:, gs], 0.0)))
                    for kk in range(HEADS_PER_GROUP):
                        k = g * HEADS_PER_GROUP + kk
                        hs = slice(kk * SSM_HEAD_DIM, (kk + 1) * SSM_HEAD_DIM)
                        dec = jnp.broadcast_to(exptot[i * DEC_SEQ:i * DEC_SEQ + 1, k:k + 1],
                                               (SSM_HEAD_DIM, D_STATE))
                        ssmo_ref[s0 + i, k * SSM_HEAD_DIM:(k + 1) * SSM_HEAD_DIM, :] = (
                            dec * h0[hs, :] + upd[hs, :])
                y_off.append(jnp.concatenate(parts, axis=1))
            y = y + jnp.concatenate(y_off, axis=0) * expcum_x

        y = y + dsk_ref[...] * xs
        yc = _rmsnorm(y * _silu(P(COL_Z, C_WIDTH)), snorm_ref[...])
        fill()
        yield
        mix = jnp.concatenate([_bf(ya), _bf(yb), _bf(yc)], axis=1)
        if prompt:
            mix_refs[c // subs_per_half][lr, :] = mix
        else:
            xo_ref[sl, :] = x + _dot(mix, wout_ref[...])

    def out_cols(h, k):
        rs = slice(h * half, (h + 1) * half)
        cs = slice(k * FILL_COLS, (k + 1) * FILL_COLS)
        xm_ref[rs, cs] = x_ref[rs, cs] + _dot(mix_refs[h][...], wout_ref[:, cs])

    group = subs_per_half if prompt else 1
    for c0 in range(0, n_sub, group):
        if prompt:
            points_left[0] = FILL_POINTS * group
        gens = [sub_block(c) for c in range(c0, c0 + group)]
        while gens:
            gens = [g for g in gens if _advance(g)]
        while fills:
            fills.pop(0)()
        if prompt:
            h = c0 // group
            outs = [functools.partial(out_cols, h, k) for k in range(D_MODEL // FILL_COLS)]
            outs.append(functools.partial(norm_ffn, h))
            if c0 + group < n_sub:
                fills.extend(outs)
            else:
                while ffn_q:
                    ffn_q.pop(0)()
                for task in outs:
                    task()
                    if ffn_tail:
                        ffn_tail.pop(0)()
                while ffn_tail:
                    ffn_tail.pop(0)()
                if final:
                    ff_final_norm()

    if prompt:
        tailb_ref[...] = hist["b"]
        tailc_ref[...] = hist["c"]

        @pl.when(live)
        def _():
            vrow_ref[...] = hist["v"]
            convo_ref[...] = hist["b"][SUBLANES - (CONV_B - 1):, :]
            sconvo_ref[...] = hist["c"][SUBLANES - (CONV_C - 1):, :]

        @pl.when(live & (j == steps - 1))
        def _():
            ssmo_ref[...] = st_ref[...].T
    else:
        convo_ref[...] = cbufb_ref[:, 2 * SUBLANES - (CONV_B - 1):, :]
        sconvo_ref[...] = cbufc_ref[:, 2 * SUBLANES - (CONV_C - 1):, :]


def _ff_kernel(x_ref, g2_ref, w1_ref, w2_ref, gf_ref, o_ref, f_ref, *, final):
    x = x_ref[...]
    h = _bf(_rmsnorm(x, g2_ref[...]))
    for c in range(D_FF // FF_COLS):
        cs = slice(c * FF_COLS, (c + 1) * FF_COLS)
        f = jnp.maximum(_dot(h, w1_ref[:, cs]), 0.0)
        f_ref[:, cs] = _bf(f * f)
    y = x + _dot(f_ref[...], w2_ref[...])
    if final:
        y = _rmsnorm(y, gf_ref[...])
    o_ref[...] = y


def _layer_spec(shape, layer, n_grid):
    zeros = (0,) * len(shape)
    if n_grid == 1:
        index_map = lambda i: (layer,) + zeros
    else:
        index_map = lambda i, j: (layer,) + zeros
    return pl.BlockSpec((None,) + tuple(shape), index_map, pipeline_mode=pl.Buffered(1))


def _mix_call(prompt, layer, x2d, params, states=None, carry=None, seq=None, final=False):
    n_rows = x2d.shape[0]
    steps = None
    if prompt:
        rows = PROMPT_ROWS
        n_batch = n_rows // seq
        steps = seq // rows
        n_blocks = n_rows // rows
        grid = (n_blocks + 1,)
        xmap = lambda t: (jnp.minimum(t, n_blocks - 1), 0)
        omap = lambda t: (jnp.maximum(t - 1, 0), 0)
        smap = lambda t: (layer, jnp.minimum(t // steps, n_batch - 1), 0, 0)
        sblk = (None, None)
        state_rows = (CHUNK, CONV_B - 1, CONV_C - 1, C_WIDTH)
    else:
        rows = SAMPLE_SEQS * DEC_SEQ
        n_batch = n_rows // DEC_SEQ
        grid = (n_batch // SAMPLE_SEQS,)
        xmap = omap = lambda i: (i, 0)
        smap = lambda i: (layer, i, 0, 0)
        sblk = (None, SAMPLE_SEQS)
        state_rows = (DEC_SEQ, CONV_B - 1, CONV_C - 1, C_WIDTH)
    state_cols = (A_WIDTH, B_WIDTH, SSM_CONV_DIM, D_STATE)

    par_specs = [_layer_spec(p.shape[1:], layer, 1) if p.ndim > 2 else
                 pl.BlockSpec(p.shape, lambda i: (0, 0), pipeline_mode=pl.Buffered(1)) for p in params]
    in_specs = [pl.BlockSpec((rows, D_MODEL), xmap)] + par_specs
    args = [x2d] + list(params)
    if prompt:
        half = rows // 2
        last_half = n_rows // half - 1
        nmap = lambda t: (jnp.minimum(2 * t + 2, last_half), 0)
        in_specs.insert(1, pl.BlockSpec((half, D_MODEL), nmap))
        args.insert(1, x2d)
        scratch = [
            pltpu.VMEM((SUBLANES, B_WIDTH), F32),
            pltpu.VMEM((SUBLANES, SSM_CONV_DIM), F32),
            pltpu.VMEM((D_STATE, C_WIDTH), F32),
            pltpu.VMEM((half, COL_DT), F32),
            pltpu.VMEM((half, COL_DT), F32),
            pltpu.VMEM((SSM_HEADS, half), F32),
            pltpu.VMEM((SSM_HEADS, half), F32),
            pltpu.VMEM((half, D_MODEL), BF16),
            pltpu.VMEM((half, D_MODEL), BF16),
            pltpu.VMEM((half, D_MODEL), BF16),
            pltpu.VMEM((half, D_MODEL), BF16),
            pltpu.VMEM((rows, D_MODEL), F32),
            pltpu.VMEM((rows, D_MODEL), BF16),
            pltpu.VMEM((rows, D_FF), BF16),
        ]
        sem = ("arbitrary",)
    else:
        in_specs += [pl.BlockSpec(sblk + (r, w), smap)
                     for r, w in zip(state_rows[1:], state_cols[1:])]
        args += list(states)
        scratch = [
            pltpu.VMEM((SAMPLE_SEQS, 2 * SUBLANES, B_WIDTH), F32),
            pltpu.VMEM((SAMPLE_SEQS, 2 * SUBLANES, SSM_CONV_DIM), F32),
        ]
        sem = ("parallel",)

    out_shape = [jax.ShapeDtypeStruct((n_rows, D_MODEL), F32)]
    out_specs = [pl.BlockSpec((rows, D_MODEL), omap)]
    for r, w in zip(state_rows, state_cols):
        out_shape.append(jax.ShapeDtypeStruct((DEPTH, n_batch, r, w), F32))
        out_specs.append(pl.BlockSpec(sblk + (r, w), smap))
    n_in = len(args)
    aliases = {}
    if carry is not None:
        in_specs += [pl.BlockSpec(memory_space=pl.ANY)] * len(carry)
        args += list(carry)
        aliases = {n_in + k: 1 + k for k in range(len(carry))}

    return pl.pallas_call(
        functools.partial(_mix_kernel, prompt=prompt, rows=rows, n_in=n_in, n_carry=len(aliases),
                          steps=steps, final=final),
        grid=grid,
        in_specs=in_specs,
        out_specs=out_specs,
        out_shape=out_shape,
        scratch_shapes=scratch,
        input_output_aliases=aliases,
        compiler_params=pltpu.CompilerParams(dimension_semantics=sem, vmem_limit_bytes=VMEM_LIMIT),
        name="mix_prompt" if prompt else "mix_sample",
    )(*args)


def _ff_call(layer, x2d, g2, w1, w2, gf, final):
    n_rows = x2d.shape[0]
    return pl.pallas_call(
        functools.partial(_ff_kernel, final=final),
        grid=(n_rows // FF_ROWS,),
        in_specs=[
            pl.BlockSpec((FF_ROWS, D_MODEL), lambda i: (i, 0)),
            _layer_spec(g2.shape[1:], layer, 1),
            _layer_spec(w1.shape[1:], layer, 1),
            _layer_spec(w2.shape[1:], layer, 1),
            pl.BlockSpec(gf.shape, lambda i: (0, 0), pipeline_mode=pl.Buffered(1)),
        ],
        out_specs=pl.BlockSpec((FF_ROWS, D_MODEL), lambda i: (i, 0)),
        out_shape=jax.ShapeDtypeStruct((n_rows, D_MODEL), F32),
        scratch_shapes=[pltpu.VMEM((FF_ROWS, D_FF), BF16)],
        compiler_params=pltpu.CompilerParams(dimension_semantics=("parallel",),
                                             vmem_limit_bytes=VMEM_LIMIT),
        name="ffn",
    )(x2d, g2, w1, w2, gf)


def kernel(x_prompt, x_sample, state_conv, state_ssm_conv, state_ssm, norm1, w_in, w_s, b_s, conv_w,
           ssm_conv_w, ssm_conv_b, dt_bias, a_log, d_skip, ssm_norm, w_out, norm2, w_ff1, w_ff2,
           final_norm):
    bp, seq, _ = x_prompt.shape
    bs, dseq, _ = x_sample.shape
    assert seq % PROMPT_ROWS == 0 and dseq == DEC_SEQ and bs % SAMPLE_SEQS == 0

    wdt_b = _bf(jnp.pad(w_in[:, :, COL_DT:], ((0, 0), (0, 0), (0, D_IN_PAD - D_IN))))
    wdt_t = _bf(jnp.transpose(w_in[:, :, COL_DT:], (0, 2, 1)))
    win_all, wout_all, w1_all, w2_all = _bf(w_in), _bf(w_out), _bf(w_ff1), _bf(w_ff2)
    g1 = norm1.reshape(DEPTH, 1, D_MODEL)
    g2 = norm2.reshape(DEPTH, 1, D_MODEL)
    gf = final_norm.reshape(1, D_MODEL)
    wa_prompt = jnp.transpose(w_s, (0, 2, 1, 3)).reshape(DEPTH, CHUNK, A_HEADS * CHUNK)
    wa_sample = jnp.repeat(jnp.transpose(w_s[:, :, :DEC_SEQ, :DEC_SEQ], (0, 3, 2, 1)),
                           A_WIDTH // A_HEADS, axis=-1)
    ba = jnp.repeat(jnp.transpose(b_s, (0, 2, 1)), A_WIDTH // A_HEADS, axis=-1)
    scb = ssm_conv_b.reshape(DEPTH, 1, SSM_CONV_DIM)
    pad_heads = lambda p: jnp.pad(p, ((0, 0), (0, LANES - SSM_HEADS))).reshape(DEPTH, 1, LANES)
    dtb = pad_heads(dt_bias)
    alog = pad_heads(a_log)
    dsk = jnp.repeat(d_skip, SSM_HEAD_DIM, axis=-1).reshape(DEPTH, 1, C_WIDTH)
    snorm = ssm_norm.reshape(DEPTH, 1, C_WIDTH)
    over_time = lambda p: jnp.broadcast_to(p[:, :, None], (DEPTH, SSM_HEADS, CHUNK))
    mid_prompt = (wa_prompt, ba, conv_w, ssm_conv_w, scb, over_time(dt_bias), over_time(a_log), dsk, snorm)
    mid_sample = (wa_sample, ba, conv_w, ssm_conv_w, scb, dtb, alog, dsk, snorm)

    xp = x_prompt.reshape(bp * seq, D_MODEL)
    xs = x_sample.reshape(bs * dseq, D_MODEL)
    ssm_in = state_ssm.reshape(DEPTH, bs, C_WIDTH, D_STATE)
    st_p = st_s = None
    for l in range(DEPTH):
        final = l == DEPTH - 1
        win_b, wout_b, w1_b, w2_b = win_all, wout_all, w1_all, w2_all
        xp, *st_p = _mix_call(True, l, xp, (g1, win_b, wdt_t) + mid_prompt + (wout_b, g2, w1_b, w2_b, gf),
                              carry=st_p, seq=seq, final=final)
        xs, *st_s = _mix_call(False, l, xs, (g1, win_b, wdt_b) + mid_sample + (wout_b,),
                              (state_conv, state_ssm_conv, ssm_in), carry=st_s)
        xs = _ff_call(l, xs, g2, w1_b, w2_b, gf, final)

    y_prompt = xp.reshape(bp, seq, D_MODEL)
    y_sample = xs.reshape(bs, dseq, D_MODEL)
    ssm_shape = lambda b: (DEPTH, b, SSM_HEADS, SSM_HEAD_DIM, D_STATE)
    return (y_prompt, y_sample,
            st_p[0], st_p[1], st_p[2], st_p[3].reshape(ssm_shape(bp)),
            st_s[0], st_s[1], st_s[2], st_s[3].reshape(ssm_shape(bs)))
```

```python
import functools

import jax
import jax.numpy as jnp
from jax import lax
from jax.experimental import pallas as pl
from jax.experimental.pallas import tpu as pltpu

D_MODEL = 1024
DEPTH = 4
A_WIDTH = 256
A_HEADS = 4
B_WIDTH = 256
C_WIDTH = 512
CONV_B = 3
CONV_C = 4
SSM_HEADS = 8
SSM_HEAD_DIM = 64
SSM_GROUPS = 2
HEADS_PER_GROUP = SSM_HEADS // SSM_GROUPS
D_STATE = 128
SSM_CONV_DIM = 1024
CHUNK = 128
D_FF = 4096
EPS = 1e-5
DEC_SEQ = 8

COL_U, COL_V, COL_BG, COL_CG, COL_HB, COL_Z, COL_XBC, COL_DT = 0, 256, 512, 768, 1024, 1280, 1792, 2816
D_IN = 2824
LANES = 128
SUBLANES = 8
D_IN_PAD = 2944
GROUP_W = C_WIDTH // SSM_GROUPS

PROMPT_ROWS = 512
SAMPLE_SEQS = 16
FILL_COLS = 256
FILL_POINTS = 6
FFN_HEAD_START = 2
FF_DOWN_K = 1024
FFN_TAIL = 6
FF_ROWS = 1024
FF_COLS = 1024
VMEM_LIMIT = 56 * 1024 * 1024

F32 = jnp.float32
BF16 = jnp.bfloat16


def _bf(x):
    return x.astype(BF16)


def _dot(a, b):
    return jnp.dot(a, b, preferred_element_type=F32)


def _dot_nt(a, b):
    return lax.dot_general(a, b, (((1,), (1,)), ((), ())), preferred_element_type=F32)


def _split3(a):
    a1 = _bf(a)
    r1 = a - a1.astype(F32)
    a2 = _bf(r1)
    r2 = r1 - a2.astype(F32)
    return a1, a2, _bf(r2)


def _sel_left(m01, a):
    a1, a2, a3 = _split3(a)
    return _dot(m01, a1) + _dot(m01, a2) + _dot(m01, a3)


def _rmsnorm(x, g):
    ms = jnp.mean(x * x, axis=-1, keepdims=True)
    return (x * lax.rsqrt(ms + EPS)) * g


def _gelu(x):
    return 0.5 * x * (1.0 + lax.erf(x * (0.5 ** 0.5)))


def _silu(x):
    return x * jax.nn.sigmoid(x)


def _softplus(x):
    return jnp.maximum(x, 0.0) + jnp.log1p(jnp.exp(-jnp.abs(x)))


def _block_diag_pair(t):
    lane = lax.broadcasted_iota(jnp.int32, t.shape, 1)
    lo = jnp.where(lane < SSM_HEAD_DIM, t, 0.0)
    hi = jnp.where(lane >= SSM_HEAD_DIM, t, 0.0)
    return _bf(jnp.concatenate([lo, hi], axis=0))


def _causal_taps_rows(xin, prev8, w_ref, n_taps):
    row8 = lax.broadcasted_iota(jnp.int32, prev8.shape, 0)
    out = None
    for k in range(n_taps):
        back = n_taps - 1 - k
        if back == 0:
            sh = xin
        else:
            r = pltpu.roll(xin, back, axis=0)
            first = jnp.where(row8 >= back, r[0:SUBLANES], pltpu.roll(prev8, back, axis=0))
            sh = jnp.concatenate([first, r[SUBLANES:]], axis=0)
        term = sh * w_ref[k:k + 1, :]
        out = term if out is None else out + term
    return out


def _causal_taps_seqs(buf_ref, s0, n_seq, w_ref, n_taps, width):
    out = None
    for k in range(n_taps):
        back = n_taps - 1 - k
        sh = buf_ref[s0:s0 + n_seq, SUBLANES - back:2 * SUBLANES - back, :].reshape(n_seq * DEC_SEQ, width)
        term = sh * w_ref[k:k + 1, :]
        out = term if out is None else out + term
    return out


def _advance(gen):
    try:
        next(gen)
        return True
    except StopIteration:
        return False


def _mix_kernel(*refs, prompt, rows, n_in, n_carry, steps=None, final=False):
    refs = refs[:n_in] + refs[n_in + n_carry:]
    if prompt:
        (x_ref, xn_ref, g1_ref, win_ref, wdt_ref, wa_ref, ba_ref, cw_ref, scw_ref, scb_ref, dtb_ref, alog_ref,
         dsk_ref, snorm_ref, wout_ref, g2_ref, w1_ref, w2_ref, gf_ref,
         xo_ref, vrow_ref, convo_ref, sconvo_ref, ssmo_ref,
         tailb_ref, tailc_ref, st_ref, proja_ref, projb_ref, dtta_ref, dttb_ref, hna_ref, hnb_ref,
         mixa_ref, mixb_ref,
         xm_ref, hff_ref, f_ref) = refs
    else:
        (x_ref, g1_ref, win_ref, wdt_ref, wa_ref, ba_ref, cw_ref, scw_ref, scb_ref, dtb_ref, alog_ref,
         dsk_ref, snorm_ref, wout_ref, conv0_ref, sconv0_ref, ssm0_ref,
         xo_ref, vrow_ref, convo_ref, sconvo_ref, ssmo_ref,
         cbufb_ref, cbufc_ref) = refs
    n_sub = rows // CHUNK
    seqs_per_sub = CHUNK // DEC_SEQ

    def in_proj(x_rows):
        h = _bf(_rmsnorm(x_rows, g1_ref[...]))
        return jnp.concatenate([_dot(h, win_ref[:, 0:COL_DT]), _dot(h, wdt_ref[...])], axis=1)

    ups_left = [D_FF // FILL_COLS]

    def ff_up(k):
        ups_left[0] -= 1
        cs = slice(k * FILL_COLS, (k + 1) * FILL_COLS)
        f = jnp.maximum(_dot(hff_ref[...], w1_ref[:, cs]), 0.0)
        f_ref[:, cs] = _bf(f * f)

    if prompt:
        half = rows // 2
        subs_per_half = n_sub // 2
        n_fill = -(-D_IN_PAD // FILL_COLS)
        t = pl.program_id(0)
        j = t % steps
        live = t < pl.num_programs(0) - 1

        @pl.when(j == 0)
        def _():
            tailb_ref[...] = jnp.zeros(tailb_ref.shape, F32)
            tailc_ref[...] = jnp.zeros(tailc_ref.shape, F32)
            st_ref[...] = jnp.zeros(st_ref.shape, F32)

        @pl.when(t == 0)
        def _():
            h0 = _bf(_rmsnorm(x_ref[0:half, :], g1_ref[...]))
            proja_ref[...] = _dot(h0, win_ref[:, 0:COL_DT])
            dtta_ref[...] = _dot_nt(wdt_ref[...], h0)
            xm_ref[...] = jnp.zeros(xm_ref.shape, F32)
            hff_ref[...] = jnp.zeros(hff_ref.shape, BF16)

        for k in range(FFN_HEAD_START):
            ff_up(k)
        xo_ref[...] = xm_ref[...]
        hn_refs = (hna_ref, hnb_ref)
        hna_ref[...] = _bf(_rmsnorm(x_ref[half:rows, :], g1_ref[...]))
    else:
        cbufb_ref[:, SUBLANES - (CONV_B - 1):SUBLANES, :] = conv0_ref[...]
        cbufc_ref[:, SUBLANES - (CONV_C - 1):SUBLANES, :] = sconv0_ref[...]

    row = lax.broadcasted_iota(jnp.int32, (CHUNK, CHUNK), 0)
    col = lax.broadcasted_iota(jnp.int32, (CHUNK, CHUNK), 1)
    if prompt:
        causal = col <= row
    else:
        causal = ((row >> 3) == (col >> 3)) & (col <= row)
    causal01 = _bf(jnp.where(causal, 1.0, 0.0))
    erow = lax.broadcasted_iota(jnp.int32, (LANES, C_WIDTH), 0)
    ecol = lax.broadcasted_iota(jnp.int32, (LANES, C_WIDTH), 1)
    expand01 = _bf(jnp.where((ecol >> 6) == erow, 1.0, 0.0))

    def expand(a):
        a1 = _bf(a)
        a2 = _bf(a - a1.astype(F32))
        return _dot(a1, expand01) + _dot(a2, expand01)

    a_neg = -jnp.exp(alog_ref[...])
    if prompt:
        upper01 = _bf(jnp.where(row <= col, 1.0, 0.0))
        arow = lax.broadcasted_iota(jnp.int32, (CHUNK, A_HEADS * CHUNK), 0)
        acol = lax.broadcasted_iota(jnp.int32, (CHUNK, A_HEADS * CHUNK), 1)
        wa_b = _bf(jnp.where((acol & (CHUNK - 1)) <= arow, wa_ref[...], 0.0))
        hist = {"b": tailb_ref[...], "c": tailc_ref[...]}
        mix_refs = (mixa_ref, mixb_ref)

    fills = []
    points_left = [0]
    ffn_q = []
    step_points_left = [FILL_POINTS * n_sub]

    def fill_cols(dst, src, k):
        if k * FILL_COLS < COL_DT:
            cs = slice(k * FILL_COLS, (k + 1) * FILL_COLS)
            dst[:, cs] = _dot(src[...], win_ref[:, cs])
        else:
            dtt = dttb_ref if dst is projb_ref else dtta_ref
            dtt[...] = _dot_nt(wdt_ref[...], src[...])

    def norm_next_step():
        hnb_ref[...] = _bf(_rmsnorm(xn_ref[...], g1_ref[...]))

    def norm_ffn(h):
        assert ups_left[0] == 0, "hff_ref is still being read by this step's up-projections"
        rs = slice(h * half, (h + 1) * half)
        hff_ref[rs, :] = _bf(_rmsnorm(xm_ref[rs, :], g2_ref[...]))

    def ff_down(kc, nc):
        ks = slice(kc * FF_DOWN_K, (kc + 1) * FF_DOWN_K)
        cs = slice(nc * FILL_COLS, (nc + 1) * FILL_COLS)
        xo_ref[:, cs] += _dot(f_ref[:, ks], w2_ref[ks, cs])

    def ff_final_norm():
        xo_ref[...] = _rmsnorm(xo_ref[...], gf_ref[...])

    if prompt:
        ffn_q.extend(functools.partial(ff_up, k) for k in range(FFN_HEAD_START, D_FF // FILL_COLS))
        for kc in range(D_FF // FF_DOWN_K):
            ffn_q.extend(functools.partial(ff_down, kc, nc) for nc in range(D_MODEL // FILL_COLS))
        ffn_tail = ffn_q[-FFN_TAIL:]
        del ffn_q[-FFN_TAIL:]

    def fill():
        for queue, left in ((fills, points_left), (ffn_q, step_points_left)):
            n = -(-len(queue) // max(left[0], 1))
            left[0] -= 1
            for _ in range(min(n, len(queue))):
                queue.pop(0)()

    def sub_block(c):
        sl = slice(c * CHUNK, (c + 1) * CHUNK)
        x = x_ref[sl, :]
        if prompt:
            if c % subs_per_half == 0:
                h = c // subs_per_half
                dst = projb_ref if h == 0 else proja_ref
                fills.extend(functools.partial(fill_cols, dst, hn_refs[h], k) for k in range(n_fill))
                if h == 0:
                    fills.append(norm_next_step)
            pref = proja_ref if c < subs_per_half else projb_ref
            lr = slice((c % subs_per_half) * CHUNK, (c % subs_per_half + 1) * CHUNK)
            P = lambda c0, w, pref=pref, lr=lr: pref[lr, c0:c0 + w]
        else:
            proj = in_proj(x)
            P = lambda c0, w, proj=proj: proj[:, c0:c0 + w]

        if prompt:
            dtt = dtta_ref if c < subs_per_half else dttb_ref
            dt_t = _softplus(dtt[:, lr] + dtb_ref[...])
            a = dt_t * a_neg
            a1 = _bf(a).astype(F32)
            a2 = _bf(a - a1).astype(F32)
            r = _dot(_bf(jnp.concatenate([a1, a2, a - a1 - a2, jnp.zeros_like(a)], axis=0)), upper01)
            cum_t = r[0:8] + r[8:16] + r[16:24]
            expcum_t = jnp.exp(cum_t)
            wst_t = jnp.exp(cum_t[:, CHUNK - 1:CHUNK] - cum_t) * dt_t
            pack = jnp.concatenate([wst_t, expcum_t, cum_t,
                                    jnp.zeros((CHUNK - 3 * SSM_HEADS, CHUNK), F32)], axis=0)
            cols = pack.T

            def per_channel(lane0):
                tiles = []
                for p in range(SSM_HEADS // 2):
                    even = jnp.broadcast_to(cols[:, lane0 + 2 * p:lane0 + 2 * p + 1], (CHUNK, LANES))
                    odd = jnp.broadcast_to(cols[:, lane0 + 2 * p + 1:lane0 + 2 * p + 2], (CHUNK, LANES))
                    tiles.append(jnp.where(col < SSM_HEAD_DIM, even, odd))
                return jnp.concatenate(tiles, axis=1)

            w_state_x, expcum_x = per_channel(0), per_channel(SSM_HEADS)
            cum, cum_lane0 = cols, 2 * SSM_HEADS
        else:
            cum_lane0 = 0
            dt = _softplus(P(COL_DT, LANES) + dtb_ref[...])
            cum = _sel_left(causal01, dt * a_neg)
            cum3 = cum.reshape(seqs_per_sub, DEC_SEQ, LANES)
            tot = jnp.broadcast_to(cum3[:, DEC_SEQ - 1:DEC_SEQ, :], cum3.shape).reshape(CHUNK, LANES)
            expcum = jnp.exp(cum)
            w_state = jnp.exp(tot - cum) * dt
            ex = expand(jnp.concatenate([w_state, expcum], axis=0))
            w_state_x, expcum_x = ex[0:CHUNK], ex[CHUNK:]
            cum_t = cum.T
            dt_t = dt.T

        u = _gelu(P(COL_U, A_WIDTH))
        v = _gelu(P(COL_V, A_WIDTH))
        if prompt:
            pairs = []
            for p in range(A_HEADS // 2):
                rhs = _block_diag_pair(v[:, p * LANES:(p + 1) * LANES])
                pairs.append(_dot(wa_b[:, p * 2 * CHUNK:(p + 1) * 2 * CHUNK], rhs))
            s = jnp.concatenate(pairs, axis=1) + ba_ref[...]
            if c == n_sub - 1:
                hist["v"] = v
        else:
            s0 = c * seqs_per_sub
            v3 = v.reshape(seqs_per_sub, DEC_SEQ, A_WIDTH)
            trow = lax.broadcasted_iota(jnp.int32, (DEC_SEQ, A_WIDTH), 0)
            s3 = jnp.zeros((seqs_per_sub, DEC_SEQ, A_WIDTH), F32)
            for jj in range(DEC_SEQ):
                coef = jnp.where(trow >= jj, wa_ref[jj], 0.0)
                s3 = s3 + coef[None] * v3[:, jj:jj + 1, :]
            s = (s3 + ba_ref[0:DEC_SEQ, :][None]).reshape(CHUNK, A_WIDTH)
            vrow_ref[s0:s0 + seqs_per_sub] = v3
        ya = u * s
        fill()
        yield

        cg_in = P(COL_CG, B_WIDTH) * P(COL_HB, B_WIDTH)
        xbc_in = P(COL_XBC, SSM_CONV_DIM)
        if prompt:
            conv = _causal_taps_rows(cg_in, hist["b"], cw_ref, CONV_B)
            xbc = _causal_taps_rows(xbc_in, hist["c"], scw_ref, CONV_C)
            hist["b"] = cg_in[CHUNK - SUBLANES:, :]
            hist["c"] = xbc_in[CHUNK - SUBLANES:, :]
        else:
            cbufb_ref[s0:s0 + seqs_per_sub, SUBLANES:, :] = cg_in.reshape(seqs_per_sub, DEC_SEQ, B_WIDTH)
            cbufc_ref[s0:s0 + seqs_per_sub, SUBLANES:, :] = xbc_in.reshape(seqs_per_sub, DEC_SEQ, SSM_CONV_DIM)
            conv = _causal_taps_seqs(cbufb_ref, s0, seqs_per_sub, cw_ref, CONV_B, B_WIDTH)
            xbc = _causal_taps_seqs(cbufc_ref, s0, seqs_per_sub, scw_ref, CONV_C, SSM_CONV_DIM)
        yb = P(COL_BG, B_WIDTH) * conv
        fill()
        yield

        xbc = _silu(xbc + scb_ref[...])
        xs = xbc[:, 0:C_WIDTH]
        bm = xbc[:, C_WIDTH:C_WIDTH + SSM_GROUPS * D_STATE]
        cm = xbc[:, C_WIDTH + SSM_GROUPS * D_STATE:]
        xw = xs * w_state_x
        bm_b = _bf(bm)
        cm_b = _bf(cm)
        fill()
        yield

        zero_blk = jnp.zeros((CHUNK, D_STATE), BF16)
        b_diag = jnp.concatenate([jnp.concatenate([bm_b[:, 0:D_STATE], zero_blk], axis=1),
                                  jnp.concatenate([zero_blk, bm_b[:, D_STATE:]], axis=1)], axis=0)
        cb_all = _dot_nt(cm_b, b_diag)
        y_pairs = []
        for g in range(SSM_GROUPS):
            gs = slice(g * D_STATE, (g + 1) * D_STATE)
            cb = cb_all[:, gs]
            wts = []
            for kk in range(HEADS_PER_GROUP):
                k = g * HEADS_PER_GROUP + kk
                seg = cum[:, cum_lane0 + k:cum_lane0 + k + 1] - cum_t[k:k + 1, :]
                decay = jnp.exp(jnp.where(causal, seg, -jnp.inf))
                wts.append(_bf(cb * decay * dt_t[k:k + 1, :]))
            for p in range(HEADS_PER_GROUP // 2):
                k0 = g * HEADS_PER_GROUP + 2 * p
                lhs = jnp.concatenate([wts[2 * p], wts[2 * p + 1]], axis=1)
                rhs = _block_diag_pair(xs[:, k0 * SSM_HEAD_DIM:(k0 + 2) * SSM_HEAD_DIM])
                y_pairs.append(_dot(lhs, rhs))
            fill()
            yield
        y = jnp.concatenate(y_pairs, axis=1)

        if prompt:
            xw_b = _bf(xw)
            y_off = []
            for g in range(SSM_GROUPS):
                gs = slice(g * D_STATE, (g + 1) * D_STATE)
                gw = slice(g * GROUP_W, (g + 1) * GROUP_W)
                st_g = st_ref[:, gw]
                y_off.append(_dot(cm_b[:, gs], _bf(st_g)))
                st_ref[:, gw] = expcum_x[CHUNK - 1:CHUNK, gw] * st_g + _dot(_bf(bm[:, gs].T), xw_b[:, gw])
            y = y + jnp.concatenate(y_off, axis=1) * expcum_x
        else:
            exptot = jnp.exp(tot)
            xw_t = [_bf(xw[:, g * GROUP_W:(g + 1) * GROUP_W].T) for g in range(SSM_GROUPS)]
            y_off = []
            for i in range(seqs_per_sub):
                rs = slice(i * DEC_SEQ, (i + 1) * DEC_SEQ)
                own_rows = (row >> 3) == i
                parts = []
                for g in range(SSM_GROUPS):
                    gs = slice(g * D_STATE, (g + 1) * D_STATE)
                    gw = slice(g * GROUP_W, (g + 1) * GROUP_W)
                    h0 = ssm0_ref[s0 + i, gw, :]
                    parts.append(_dot_nt(_bf(cm[rs, gs]), _bf(h0)))
                    upd = _dot(xw_t[g], _bf(jnp.where(own_rows, bm[:, gs], 0.0)))
                    for kk in range(HEADS_PER_GROUP):
                        k = g * HEADS_PER_GROUP + kk
                        hs = slice(kk * SSM_HEAD_DIM, (kk + 1) * SSM_HEAD_DIM)
                        dec = jnp.broadcast_to(exptot[i * DEC_SEQ:i * DEC_SEQ + 1, k:k + 1],
                                               (SSM_HEAD_DIM, D_STATE))
                        ssmo_ref[s0 + i, k * SSM_HEAD_DIM:(k + 1) * SSM_HEAD_DIM, :] = (
                            dec * h0[hs, :] + upd[hs, :])
                y_off.append(jnp.concatenate(parts, axis=1))
            y = y + jnp.concatenate(y_off, axis=0) * expcum_x

        y = y + dsk_ref[...] * xs
        yc = _rmsnorm(y * _silu(P(COL_Z, C_WIDTH)), snorm_ref[...])
        fill()
        yield
        mix = jnp.concatenate([_bf(ya), _bf(yb), _bf(yc)], axis=1)
        if prompt:
            mix_refs[c // subs_per_half][lr, :] = mix
        else:
            xo_ref[sl, :] = x + _dot(mix, wout_ref[...])

    def out_cols(h, k):
        rs = slice(h * half, (h + 1) * half)
        cs = slice(k * FILL_COLS, (k + 1) * FILL_COLS)
        xm_ref[rs, cs] = x_ref[rs, cs] + _dot(mix_refs[h][...], wout_ref[:, cs])

    group = subs_per_half if prompt else 1
    for c0 in range(0, n_sub, group):
        if prompt:
            points_left[0] = FILL_POINTS * group
        gens = [sub_block(c) for c in range(c0, c0 + group)]
        while gens:
            gens = [g for g in gens if _advance(g)]
        while fills:
            fills.pop(0)()
        if prompt:
            h = c0 // group
            outs = [functools.partial(out_cols, h, k) for k in range(D_MODEL // FILL_COLS)]
            outs.append(functools.partial(norm_ffn, h))
            if c0 + group < n_sub:
                fills.extend(outs)
            else:
                while ffn_q:
                    ffn_q.pop(0)()
                for task in outs:
                    task()
                    if ffn_tail:
                        ffn_tail.pop(0)()
                while ffn_tail:
                    ffn_tail.pop(0)()
                if final:
                    ff_final_norm()

    if prompt:
        tailb_ref[...] = hist["b"]
        tailc_ref[...] = hist["c"]

        @pl.when(live)
        def _():
            vrow_ref[...] = hist["v"]
            convo_ref[...] = hist["b"][SUBLANES - (CONV_B - 1):, :]
            sconvo_ref[...] = hist["c"][SUBLANES - (CONV_C - 1):, :]

        @pl.when(live & (j == steps - 1))
        def _():
            ssmo_ref[...] = st_ref[...].T
    else:
        convo_ref[...] = cbufb_ref[:, 2 * SUBLANES - (CONV_B - 1):, :]
        sconvo_ref[...] = cbufc_ref[:, 2 * SUBLANES - (CONV_C - 1):, :]


def _ff_kernel(x_ref, g2_ref, w1_ref, w2_ref, gf_ref, o_ref, f_ref, *, final):
    x = x_ref[...]
    h = _bf(_rmsnorm(x, g2_ref[...]))
    for c in range(D_FF // FF_COLS):
        cs = slice(c * FF_COLS, (c + 1) * FF_COLS)
        f = jnp.maximum(_dot(h, w1_ref[:, cs]), 0.0)
        f_ref[:, cs] = _bf(f * f)
    y = x + _dot(f_ref[...], w2_ref[...])
    if final:
        y = _rmsnorm(y, gf_ref[...])
    o_ref[...] = y


def _layer_spec(shape, layer, n_grid):
    zeros = (0,) * len(shape)
    if n_grid == 1:
        index_map = lambda i: (layer,) + zeros
    else:
        index_map = lambda i, j: (layer,) + zeros
    return pl.BlockSpec((None,) + tuple(shape), index_map, pipeline_mode=pl.Buffered(1))


def _mix_call(prompt, layer, x2d, params, states=None, carry=None, seq=None, final=False):
    n_rows = x2d.shape[0]
    steps = None
    if prompt:
        rows = PROMPT_ROWS
        n_batch = n_rows // seq
        steps = seq // rows
        n_blocks = n_rows // rows
        grid = (n_blocks + 1,)
        xmap = lambda t: (jnp.minimum(t, n_blocks - 1), 0)
        omap = lambda t: (jnp.maximum(t - 1, 0), 0)
        smap = lambda t: (layer, jnp.minimum(t // steps, n_batch - 1), 0, 0)
        sblk = (None, None)
        state_rows = (CHUNK, CONV_B - 1, CONV_C - 1, C_WIDTH)
    else:
        rows = SAMPLE_SEQS * DEC_SEQ
        n_batch = n_rows // DEC_SEQ
        grid = (n_batch // SAMPLE_SEQS,)
        xmap = omap = lambda i: (i, 0)
        smap = lambda i: (layer, i, 0, 0)
        sblk = (None, SAMPLE_SEQS)
        state_rows = (DEC_SEQ, CONV_B - 1, CONV_C - 1, C_WIDTH)
    state_cols = (A_WIDTH, B_WIDTH, SSM_CONV_DIM, D_STATE)

    par_specs = [_layer_spec(p.shape[1:], layer, 1) if p.ndim > 2 else
                 pl.BlockSpec(p.shape, lambda i: (0, 0), pipeline_mode=pl.Buffered(1)) for p in params]
    in_specs = [pl.BlockSpec((rows, D_MODEL), xmap)] + par_specs
    args = [x2d] + list(params)
    if prompt:
        half = rows // 2
        last_half = n_rows // half - 1
        nmap = lambda t: (jnp.minimum(2 * t + 2, last_half), 0)
        in_specs.insert(1, pl.BlockSpec((half, D_MODEL), nmap))
        args.insert(1, x2d)
        scratch = [
            pltpu.VMEM((SUBLANES, B_WIDTH), F32),
            pltpu.VMEM((SUBLANES, SSM_CONV_DIM), F32),
            pltpu.VMEM((D_STATE, C_WIDTH), F32),
            pltpu.VMEM((half, COL_DT), F32),
            pltpu.VMEM((half, COL_DT), F32),
            pltpu.VMEM((SSM_HEADS, half), F32),
            pltpu.VMEM((SSM_HEADS, half), F32),
            pltpu.VMEM((half, D_MODEL), BF16),
            pltpu.VMEM((half, D_MODEL), BF16),
            pltpu.VMEM((half, D_MODEL), BF16),
            pltpu.VMEM((half, D_MODEL), BF16),
            pltpu.VMEM((rows, D_MODEL), F32),
            pltpu.VMEM((rows, D_MODEL), BF16),
            pltpu.VMEM((rows, D_FF), BF16),
        ]
        sem = ("arbitrary",)
    else:
        in_specs += [pl.BlockSpec(sblk + (r, w), smap)
                     for r, w in zip(state_rows[1:], state_cols[1:])]
        args += list(states)
        scratch = [
            pltpu.VMEM((SAMPLE_SEQS, 2 * SUBLANES, B_WIDTH), F32),
            pltpu.VMEM((SAMPLE_SEQS, 2 * SUBLANES, SSM_CONV_DIM), F32),
        ]
        sem = ("parallel",)

    out_shape = [jax.ShapeDtypeStruct((n_rows, D_MODEL), F32)]
    out_specs = [pl.BlockSpec((rows, D_MODEL), omap)]
    for r, w in zip(state_rows, state_cols):
        out_shape.append(jax.ShapeDtypeStruct((DEPTH, n_batch, r, w), F32))
        out_specs.append(pl.BlockSpec(sblk + (r, w), smap))
    n_in = len(args)
    aliases = {}
    if carry is not None:
        in_specs += [pl.BlockSpec(memory_space=pl.ANY)] * len(carry)
        args += list(carry)
        aliases = {n_in + k: 1 + k for k in range(len(carry))}

    return pl.pallas_call(
        functools.partial(_mix_kernel, prompt=prompt, rows=rows, n_in=n_in, n_carry=len(aliases),
                          steps=steps, final=final),
        grid=grid,
        in_specs=in_specs,
        out_specs=out_specs,
        out_shape=out_shape,
        scratch_shapes=scratch,
        input_output_aliases=aliases,
        compiler_params=pltpu.CompilerParams(dimension_semantics=sem, vmem_limit_bytes=VMEM_LIMIT),
        name="mix_prompt" if prompt else "mix_sample",
    )(*args)


def _ff_call(layer, x2d, g2, w1, w2, gf, final):
    n_rows = x2d.shape[0]
    return pl.pallas_call(
        functools.partial(_ff_kernel, final=final),
        grid=(n_rows // FF_ROWS,),
        in_specs=[
            pl.BlockSpec((FF_ROWS, D_MODEL), lambda i: (i, 0)),
            _layer_spec(g2.shape[1:], layer, 1),
            _layer_spec(w1.shape[1:], layer, 1),
            _layer_spec(w2.shape[1:], layer, 1),
            pl.BlockSpec(gf.shape, lambda i: (0, 0), pipeline_mode=pl.Buffered(1)),
        ],
        out_specs=pl.BlockSpec((FF_ROWS, D_MODEL), lambda i: (i, 0)),
        out_shape=jax.ShapeDtypeStruct((n_rows, D_MODEL), F32),
        scratch_shapes=[pltpu.VMEM((FF_ROWS, D_FF), BF16)],
        compiler_params=pltpu.CompilerParams(dimension_semantics=("parallel",),
                                             vmem_limit_bytes=VMEM_LIMIT),
        name="ffn",
    )(x2d, g2, w1, w2, gf)


def kernel(x_prompt, x_sample, state_conv, state_ssm_conv, state_ssm, norm1, w_in, w_s, b_s, conv_w,
           ssm_conv_w, ssm_conv_b, dt_bias, a_log, d_skip, ssm_norm, w_out, norm2, w_ff1, w_ff2,
           final_norm):
    bp, seq, _ = x_prompt.shape
    bs, dseq, _ = x_sample.shape
    assert seq % PROMPT_ROWS == 0 and dseq == DEC_SEQ and bs % SAMPLE_SEQS == 0

    wdt_b = _bf(jnp.pad(w_in[:, :, COL_DT:], ((0, 0), (0, 0), (0, D_IN_PAD - D_IN))))
    wdt_t = _bf(jnp.transpose(w_in[:, :, COL_DT:], (0, 2, 1)))
    win_all = jnp.concatenate([_bf(w_in), jnp.zeros((DEPTH, D_MODEL, D_IN_PAD - D_IN), BF16)], axis=-1)
    wout_all, w1_all, w2_all = _bf(w_out), _bf(w_ff1), _bf(w_ff2)
    g1 = norm1.reshape(DEPTH, 1, D_MODEL)
    g2 = norm2.reshape(DEPTH, 1, D_MODEL)
    gf = final_norm.reshape(1, D_MODEL)
    wa_prompt = jnp.transpose(w_s, (0, 2, 1, 3)).reshape(DEPTH, CHUNK, A_HEADS * CHUNK)
    wa_sample = jnp.repeat(jnp.transpose(w_s[:, :, :DEC_SEQ, :DEC_SEQ], (0, 3, 2, 1)),
                           A_WIDTH // A_HEADS, axis=-1)
    ba = jnp.repeat(jnp.transpose(b_s, (0, 2, 1)), A_WIDTH // A_HEADS, axis=-1)
    scb = ssm_conv_b.reshape(DEPTH, 1, SSM_CONV_DIM)
    pad_heads = lambda p: jnp.pad(p, ((0, 0), (0, LANES - SSM_HEADS))).reshape(DEPTH, 1, LANES)
    dtb = pad_heads(dt_bias)
    alog = pad_heads(a_log)
    dsk = jnp.repeat(d_skip, SSM_HEAD_DIM, axis=-1).reshape(DEPTH, 1, C_WIDTH)
    snorm = ssm_norm.reshape(DEPTH, 1, C_WIDTH)
    over_time = lambda p: jnp.broadcast_to(p[:, :, None], (DEPTH, SSM_HEADS, CHUNK))
    mid_prompt = (wa_prompt, ba, conv_w, ssm_conv_w, scb, over_time(dt_bias), over_time(a_log), dsk, snorm)
    mid_sample = (wa_sample, ba, conv_w, ssm_conv_w, scb, dtb, alog, dsk, snorm)

    xp = x_prompt.reshape(bp * seq, D_MODEL)
    xs = x_sample.reshape(bs * dseq, D_MODEL)
    ssm_in = state_ssm.reshape(DEPTH, bs, C_WIDTH, D_STATE)
    st_p = st_s = None
    for l in range(DEPTH):
        final = l == DEPTH - 1
        win_b, wout_b, w1_b, w2_b = win_all, wout_all, w1_all, w2_all
        xp, *st_p = _mix_call(True, l, xp, (g1, win_b, wdt_t) + mid_prompt + (wout_b, g2, w1_b, w2_b, gf),
                              carry=st_p, seq=seq, final=final)
        xs, *st_s = _mix_call(False, l, xs, (g1, win_b, wdt_b) + mid_sample + (wout_b,),
                              (state_conv, state_ssm_conv, ssm_in), carry=st_s)
        xs = _ff_call(l, xs, g2, w1_b, w2_b, gf, final)

    y_prompt = xp.reshape(bp, seq, D_MODEL)
    y_sample = xs.reshape(bs, dseq, D_MODEL)
    ssm_shape = lambda b: (DEPTH, b, SSM_HEADS, SSM_HEAD_DIM, D_STATE)
    return (y_prompt, y_sample,
            st_p[0], st_p[1], st_p[2], st_p[3].reshape(ssm_shape(bp)),
            st_s[0], st_s[1], st_s[2], st_s[3].reshape(ssm_shape(bs)))
```

```python
import functools

import jax
import jax.numpy as jnp
from jax import lax
from jax.experimental import pallas as pl
from jax.experimental.pallas import tpu as pltpu

D_MODEL = 1024
DEPTH = 4
A_WIDTH = 256
A_HEADS = 4
B_WIDTH = 256
C_WIDTH = 512
CONV_B = 3
CONV_C = 4
SSM_HEADS = 8
SSM_HEAD_DIM = 64
SSM_GROUPS = 2
HEADS_PER_GROUP = SSM_HEADS // SSM_GROUPS
D_STATE = 128
SSM_CONV_DIM = 1024
CHUNK = 128
D_FF = 4096
EPS = 1e-5
DEC_SEQ = 8

COL_U, COL_V, COL_BG, COL_CG, COL_HB, COL_Z, COL_XBC, COL_DT = 0, 256, 512, 768, 1024, 1280, 1792, 2816
D_IN = 2824
LANES = 128
SUBLANES = 8
D_IN_PAD = 2944
GROUP_W = C_WIDTH // SSM_GROUPS

PROMPT_ROWS = 512
SAMPLE_SEQS = 16
FILL_COLS = 256
FILL_POINTS = 6
FFN_HEAD_START = 2
FF_DOWN_K = 1024
FFN_TAIL = 6
FF_ROWS = 1024
FF_COLS = 1024
VMEM_LIMIT = 56 * 1024 * 1024

F32 = jnp.float32
BF16 = jnp.bfloat16


def _bf(x):
    return x.astype(BF16)


def _dot(a, b):
    return jnp.dot(a, b, preferred_element_type=F32)


def _dot_nt(a, b):
    return lax.dot_general(a, b, (((1,), (1,)), ((), ())), preferred_element_type=F32)


def _split3(a):
    a1 = _bf(a)
    r1 = a - a1.astype(F32)
    a2 = _bf(r1)
    r2 = r1 - a2.astype(F32)
    return a1, a2, _bf(r2)


def _sel_left(m01, a):
    a1, a2, a3 = _split3(a)
    return _dot(m01, a1) + _dot(m01, a2) + _dot(m01, a3)


def _rmsnorm(x, g):
    ms = jnp.mean(x * x, axis=-1, keepdims=True)
    return (x * lax.rsqrt(ms + EPS)) * g


def _gelu(x):
    return 0.5 * x * (1.0 + lax.erf(x * (0.5 ** 0.5)))


def _silu(x):
    return x * jax.nn.sigmoid(x)


def _softplus(x):
    return jnp.maximum(x, 0.0) + jnp.log1p(jnp.exp(-jnp.abs(x)))


def _block_diag_pair(t):
    lane = lax.broadcasted_iota(jnp.int32, t.shape, 1)
    lo = jnp.where(lane < SSM_HEAD_DIM, t, 0.0)
    hi = jnp.where(lane >= SSM_HEAD_DIM, t, 0.0)
    return _bf(jnp.concatenate([lo, hi], axis=0))


def _causal_taps_rows(xin, prev8, w_ref, n_taps):
    row8 = lax.broadcasted_iota(jnp.int32, prev8.shape, 0)
    out = None
    for k in range(n_taps):
        back = n_taps - 1 - k
        if back == 0:
            sh = xin
        else:
            r = pltpu.roll(xin, back, axis=0)
            first = jnp.where(row8 >= back, r[0:SUBLANES], pltpu.roll(prev8, back, axis=0))
            sh = jnp.concatenate([first, r[SUBLANES:]], axis=0)
        term = sh * w_ref[k:k + 1, :]
        out = term if out is None else out + term
    return out


def _causal_taps_seqs(buf_ref, s0, n_seq, w_ref, n_taps, width):
    out = None
    for k in range(n_taps):
        back = n_taps - 1 - k
        sh = buf_ref[s0:s0 + n_seq, SUBLANES - back:2 * SUBLANES - back, :].reshape(n_seq * DEC_SEQ, width)
        term = sh * w_ref[k:k + 1, :]
        out = term if out is None else out + term
    return out


def _advance(gen):
    try:
        next(gen)
        return True
    except StopIteration:
        return False


def _mix_kernel(*refs, prompt, rows, n_in, n_carry, steps=None, final=False):
    refs = refs[:n_in] + refs[n_in + n_carry:]
    if prompt:
        (x_ref, xn_ref, g1_ref, win_ref, wdt_ref, wa_ref, ba_ref, cw_ref, scw_ref, scb_ref, dtb_ref, alog_ref,
         dsk_ref, snorm_ref, wout_ref, g2_ref, w1_ref, w2_ref, gf_ref,
         xo_ref, vrow_ref, convo_ref, sconvo_ref, ssmo_ref,
         tailb_ref, tailc_ref, st_ref, proja_ref, projb_ref, dtta_ref, dttb_ref, hna_ref, hnb_ref,
         mixa_ref, mixb_ref,
         xm_ref, hff_ref, f_ref) = refs
    else:
        (x_ref, g1_ref, win_ref, wdt_ref, wa_ref, ba_ref, cw_ref, scw_ref, scb_ref, dtb_ref, alog_ref,
         dsk_ref, snorm_ref, wout_ref, conv0_ref, sconv0_ref, ssm0_ref,
         xo_ref, vrow_ref, convo_ref, sconvo_ref, ssmo_ref,
         cbufb_ref, cbufc_ref) = refs
    n_sub = rows // CHUNK
    seqs_per_sub = CHUNK // DEC_SEQ

    def in_proj(x_rows):
        h = _bf(_rmsnorm(x_rows, g1_ref[...]))
        return jnp.concatenate([_dot(h, win_ref[:, 0:COL_DT]), _dot(h, wdt_ref[...])], axis=1)

    ups_left = [D_FF // FILL_COLS]

    def ff_up(k):
        ups_left[0] -= 1
        cs = slice(k * FILL_COLS, (k + 1) * FILL_COLS)
        f = jnp.maximum(_dot(hff_ref[...], w1_ref[:, cs]), 0.0)
        f_ref[:, cs] = _bf(f * f)

    if prompt:
        half = rows // 2
        subs_per_half = n_sub // 2
        n_fill = -(-D_IN_PAD // FILL_COLS)
        t = pl.program_id(0)
        j = t % steps
        live = t < pl.num_programs(0) - 1

        @pl.when(j == 0)
        def _():
            tailb_ref[...] = jnp.zeros(tailb_ref.shape, F32)
            tailc_ref[...] = jnp.zeros(tailc_ref.shape, F32)
            st_ref[...] = jnp.zeros(st_ref.shape, F32)

        @pl.when(t == 0)
        def _():
            h0 = _bf(_rmsnorm(x_ref[0:half, :], g1_ref[...]))
            proja_ref[...] = _dot(h0, win_ref[:, 0:COL_DT])
            dtta_ref[...] = _dot_nt(wdt_ref[...], h0)
            xm_ref[...] = jnp.zeros(xm_ref.shape, F32)
            hff_ref[...] = jnp.zeros(hff_ref.shape, BF16)

        for k in range(FFN_HEAD_START):
            ff_up(k)
        xo_ref[...] = xm_ref[...]
        hn_refs = (hna_ref, hnb_ref)
        hna_ref[...] = _bf(_rmsnorm(x_ref[half:rows, :], g1_ref[...]))
    else:
        cbufb_ref[:, SUBLANES - (CONV_B - 1):SUBLANES, :] = conv0_ref[...]
        cbufc_ref[:, SUBLANES - (CONV_C - 1):SUBLANES, :] = sconv0_ref[...]

    row = lax.broadcasted_iota(jnp.int32, (CHUNK, CHUNK), 0)
    col = lax.broadcasted_iota(jnp.int32, (CHUNK, CHUNK), 1)
    if prompt:
        causal = col <= row
    else:
        causal = ((row >> 3) == (col >> 3)) & (col <= row)
    causal01 = _bf(jnp.where(causal, 1.0, 0.0))
    erow = lax.broadcasted_iota(jnp.int32, (LANES, C_WIDTH), 0)
    ecol = lax.broadcasted_iota(jnp.int32, (LANES, C_WIDTH), 1)
    expand01 = _bf(jnp.where((ecol >> 6) == erow, 1.0, 0.0))

    def expand(a):
        a1 = _bf(a)
        a2 = _bf(a - a1.astype(F32))
        return _dot(a1, expand01) + _dot(a2, expand01)

    a_neg = -jnp.exp(alog_ref[...])
    if prompt:
        upper01 = _bf(jnp.where(row <= col, 1.0, 0.0))
        arow = lax.broadcasted_iota(jnp.int32, (CHUNK, A_HEADS * CHUNK), 0)
        acol = lax.broadcasted_iota(jnp.int32, (CHUNK, A_HEADS * CHUNK), 1)
        wa_b = _bf(jnp.where((acol & (CHUNK - 1)) <= arow, wa_ref[...], 0.0))
        hist = {"b": tailb_ref[...], "c": tailc_ref[...]}
        mix_refs = (mixa_ref, mixb_ref)

    fills = []
    points_left = [0]
    ffn_q = []
    step_points_left = [FILL_POINTS * n_sub]

    def fill_cols(dst, src, k):
        if k * FILL_COLS < COL_DT:
            cs = slice(k * FILL_COLS, (k + 1) * FILL_COLS)
            dst[:, cs] = _dot(src[...], win_ref[:, cs])
        else:
            dtt = dttb_ref if dst is projb_ref else dtta_ref
            dtt[...] = _dot_nt(wdt_ref[...], src[...])

    def norm_next_step():
        hnb_ref[...] = _bf(_rmsnorm(xn_ref[...], g1_ref[...]))

    def norm_ffn(h):
        assert ups_left[0] == 0, "hff_ref is still being read by this step's up-projections"
        rs = slice(h * half, (h + 1) * half)
        hff_ref[rs, :] = _bf(_rmsnorm(xm_ref[rs, :], g2_ref[...]))

    def ff_down(kc, nc):
        ks = slice(kc * FF_DOWN_K, (kc + 1) * FF_DOWN_K)
        cs = slice(nc * FILL_COLS, (nc + 1) * FILL_COLS)
        xo_ref[:, cs] += _dot(f_ref[:, ks], w2_ref[ks, cs])

    def ff_final_norm():
        xo_ref[...] = _rmsnorm(xo_ref[...], gf_ref[...])

    if prompt:
        ffn_q.extend(functools.partial(ff_up, k) for k in range(FFN_HEAD_START, D_FF // FILL_COLS))
        for kc in range(D_FF // FF_DOWN_K):
            ffn_q.extend(functools.partial(ff_down, kc, nc) for nc in range(D_MODEL // FILL_COLS))
        ffn_tail = ffn_q[-FFN_TAIL:]
        del ffn_q[-FFN_TAIL:]

    def fill():
        for queue, left in ((fills, points_left), (ffn_q, step_points_left)):
            n = -(-len(queue) // max(left[0], 1))
            left[0] -= 1
            for _ in range(min(n, len(queue))):
                queue.pop(0)()

    def sub_block(c):
        sl = slice(c * CHUNK, (c + 1) * CHUNK)
        x = x_ref[sl, :]
        if prompt:
            if c % subs_per_half == 0:
                h = c // subs_per_half
                dst = projb_ref if h == 0 else proja_ref
                fills.extend(functools.partial(fill_cols, dst, hn_refs[h], k) for k in range(n_fill))
                if h == 0:
                    fills.append(norm_next_step)
            pref = proja_ref if c < subs_per_half else projb_ref
            lr = slice((c % subs_per_half) * CHUNK, (c % subs_per_half + 1) * CHUNK)
            P = lambda c0, w, pref=pref, lr=lr: pref[lr, c0:c0 + w]
        else:
            proj = in_proj(x)
            P = lambda c0, w, proj=proj: proj[:, c0:c0 + w]

        if prompt:
            dtt = dtta_ref if c < subs_per_half else dttb_ref
            dt_t = _softplus(dtt[:, lr] + dtb_ref[...])
            a = dt_t * a_neg
            a1 = _bf(a).astype(F32)
            a2 = _bf(a - a1).astype(F32)
            r = _dot(_bf(jnp.concatenate([a1, a2, a - a1 - a2, jnp.zeros_like(a)], axis=0)), upper01)
            cum_t = r[0:8] + r[8:16] + r[16:24]
            expcum_t = jnp.exp(cum_t)
            wst_t = jnp.exp(cum_t[:, CHUNK - 1:CHUNK] - cum_t) * dt_t
            pack = jnp.concatenate([wst_t, expcum_t, cum_t,
                                    jnp.zeros((CHUNK - 3 * SSM_HEADS, CHUNK), F32)], axis=0)
            cols = pack.T

            def per_channel(lane0):
                tiles = []
                for p in range(SSM_HEADS // 2):
                    even = jnp.broadcast_to(cols[:, lane0 + 2 * p:lane0 + 2 * p + 1], (CHUNK, LANES))
                    odd = jnp.broadcast_to(cols[:, lane0 + 2 * p + 1:lane0 + 2 * p + 2], (CHUNK, LANES))
                    tiles.append(jnp.where(col < SSM_HEAD_DIM, even, odd))
                return jnp.concatenate(tiles, axis=1)

            w_state_x, expcum_x = per_channel(0), per_channel(SSM_HEADS)
            cum, cum_lane0 = cols, 2 * SSM_HEADS
        else:
            cum_lane0 = 0
            dt = _softplus(P(COL_DT, LANES) + dtb_ref[...])
            cum = _sel_left(causal01, dt * a_neg)
            cum3 = cum.reshape(seqs_per_sub, DEC_SEQ, LANES)
            tot = jnp.broadcast_to(cum3[:, DEC_SEQ - 1:DEC_SEQ, :], cum3.shape).reshape(CHUNK, LANES)
            expcum = jnp.exp(cum)
            w_state = jnp.exp(tot - cum) * dt
            ex = expand(jnp.concatenate([w_state, expcum], axis=0))
            w_state_x, expcum_x = ex[0:CHUNK], ex[CHUNK:]
            cum_t = cum.T
            dt_t = dt.T

        u = _gelu(P(COL_U, A_WIDTH))
        v = _gelu(P(COL_V, A_WIDTH))
        if prompt:
            pairs = []
            for p in range(A_HEADS // 2):
                rhs = _block_diag_pair(v[:, p * LANES:(p + 1) * LANES])
                pairs.append(_dot(wa_b[:, p * 2 * CHUNK:(p + 1) * 2 * CHUNK], rhs))
            s = jnp.concatenate(pairs, axis=1) + ba_ref[...]
            if c == n_sub - 1:
                hist["v"] = v
        else:
            s0 = c * seqs_per_sub
            v3 = v.reshape(seqs_per_sub, DEC_SEQ, A_WIDTH)
            trow = lax.broadcasted_iota(jnp.int32, (DEC_SEQ, A_WIDTH), 0)
            s3 = jnp.zeros((seqs_per_sub, DEC_SEQ, A_WIDTH), F32)
            for jj in range(DEC_SEQ):
                coef = jnp.where(trow >= jj, wa_ref[jj], 0.0)
                s3 = s3 + coef[None] * v3[:, jj:jj + 1, :]
            s = (s3 + ba_ref[0:DEC_SEQ, :][None]).reshape(CHUNK, A_WIDTH)
            vrow_ref[s0:s0 + seqs_per_sub] = v3
        ya = u * s
        fill()
        yield

        cg_in = P(COL_CG, B_WIDTH) * P(COL_HB, B_WIDTH)
        xbc_in = P(COL_XBC, SSM_CONV_DIM)
        if prompt:
            conv = _causal_taps_rows(cg_in, hist["b"], cw_ref, CONV_B)
            xbc = _causal_taps_rows(xbc_in, hist["c"], scw_ref, CONV_C)
            hist["b"] = cg_in[CHUNK - SUBLANES:, :]
            hist["c"] = xbc_in[CHUNK - SUBLANES:, :]
        else:
            cbufb_ref[s0:s0 + seqs_per_sub, SUBLANES:, :] = cg_in.reshape(seqs_per_sub, DEC_SEQ, B_WIDTH)
            cbufc_ref[s0:s0 + seqs_per_sub, SUBLANES:, :] = xbc_in.reshape(seqs_per_sub, DEC_SEQ, SSM_CONV_DIM)
            conv = _causal_taps_seqs(cbufb_ref, s0, seqs_per_sub, cw_ref, CONV_B, B_WIDTH)
            xbc = _causal_taps_seqs(cbufc_ref, s0, seqs_per_sub, scw_ref, CONV_C, SSM_CONV_DIM)
        yb = P(COL_BG, B_WIDTH) * conv
        fill()
        yield

        xbc = _silu(xbc + scb_ref[...])
        xs = xbc[:, 0:C_WIDTH]
        bm = xbc[:, C_WIDTH:C_WIDTH + SSM_GROUPS * D_STATE]
        cm = xbc[:, C_WIDTH + SSM_GROUPS * D_STATE:]
        xw = xs * w_state_x
        bm_b = _bf(bm)
        cm_b = _bf(cm)
        fill()
        yield

        zero_blk = jnp.zeros((CHUNK, D_STATE), BF16)
        b_diag = jnp.concatenate([jnp.concatenate([bm_b[:, 0:D_STATE], zero_blk], axis=1),
                                  jnp.concatenate([zero_blk, bm_b[:, D_STATE:]], axis=1)], axis=0)
        cb_all = _dot_nt(cm_b, b_diag)
        y_pairs = []
        for g in range(SSM_GROUPS):
            gs = slice(g * D_STATE, (g + 1) * D_STATE)
            cb = cb_all[:, gs]
            wts = []
            for kk in range(HEADS_PER_GROUP):
                k = g * HEADS_PER_GROUP + kk
                seg = cum[:, cum_lane0 + k:cum_lane0 + k + 1] - cum_t[k:k + 1, :]
                decay = jnp.exp(jnp.where(causal, seg, -jnp.inf))
                wts.append(_bf(cb * decay * dt_t[k:k + 1, :]))
            for p in range(HEADS_PER_GROUP // 2):
                k0 = g * HEADS_PER_GROUP + 2 * p
                lhs = jnp.concatenate([wts[2 * p], wts[2 * p + 1]], axis=1)
                rhs = _block_diag_pair(xs[:, k0 * SSM_HEAD_DIM:(k0 + 2) * SSM_HEAD_DIM])
                y_pairs.append(_dot(lhs, rhs))
            fill()
            yield
        y = jnp.concatenate(y_pairs, axis=1)

        if prompt:
            xw_b = _bf(xw)
            y_off = []
            for g in range(SSM_GROUPS):
                gs = slice(g * D_STATE, (g + 1) * D_STATE)
                gw = slice(g * GROUP_W, (g + 1) * GROUP_W)
                st_g = st_ref[:, gw]
                y_off.append(_dot(cm_b[:, gs], _bf(st_g)))
                st_ref[:, gw] = expcum_x[CHUNK - 1:CHUNK, gw] * st_g + _dot(_bf(bm[:, gs].T), xw_b[:, gw])
            y = y + jnp.concatenate(y_off, axis=1) * expcum_x
        else:
            exptot = jnp.exp(tot)
            xw_t = [_bf(xw[:, g * GROUP_W:(g + 1) * GROUP_W].T) for g in range(SSM_GROUPS)]
            y_off = []
            for i in range(seqs_per_sub):
                rs = slice(i * DEC_SEQ, (i + 1) * DEC_SEQ)
                own_rows = (row >> 3) == i
                parts = []
                for g in range(SSM_GROUPS):
                    gs = slice(g * D_STATE, (g + 1) * D_STATE)
                    gw = slice(g * GROUP_W, (g + 1) * GROUP_W)
                    h0 = ssm0_ref[s0 + i, gw, :]
                    parts.append(_dot_nt(_bf(cm[rs, gs]), _bf(h0)))
                    upd = _dot(xw_t[g], _bf(jnp.where(own_rows, bm[:, gs], 0.0)))
                    for kk in range(HEADS_PER_GROUP):
                        k = g * HEADS_PER_GROUP + kk
                        hs = slice(kk * SSM_HEAD_DIM, (kk + 1) * SSM_HEAD_DIM)
                        dec = jnp.broadcast_to(exptot[i * DEC_SEQ:i * DEC_SEQ + 1, k:k + 1],
                                               (SSM_HEAD_DIM, D_STATE))
                        ssmo_ref[s0 + i, k * SSM_HEAD_DIM:(k + 1) * SSM_HEAD_DIM, :] = (
                            dec * h0[hs, :] + upd[hs, :])
                y_off.append(jnp.concatenate(parts, axis=1))
            y = y + jnp.concatenate(y_off, axis=0) * expcum_x

        y = y + dsk_ref[...] * xs
        yc = _rmsnorm(y * _silu(P(COL_Z, C_WIDTH)), snorm_ref[...])
        fill()
        yield
        mix = jnp.concatenate([_bf(ya), _bf(yb), _bf(yc)], axis=1)
        if prompt:
            mix_refs[c // subs_per_half][lr, :] = mix
        else:
            xo_ref[sl, :] = x + _dot(mix, wout_ref[...])

    def out_cols(h, k):
        rs = slice(h * half, (h + 1) * half)
        cs = slice(k * FILL_COLS, (k + 1) * FILL_COLS)
        xm_ref[rs, cs] = x_ref[rs, cs] + _dot(mix_refs[h][...], wout_ref[:, cs])

    group = subs_per_half if prompt else 1
    for c0 in range(0, n_sub, group):
        if prompt:
            points_left[0] = FILL_POINTS * group
        gens = [sub_block(c) for c in range(c0, c0 + group)]
        while gens:
            gens = [g for g in gens if _advance(g)]
        while fills:
            fills.pop(0)()
        if prompt:
            h = c0 // group
            outs = [functools.partial(out_cols, h, k) for k in range(D_MODEL // FILL_COLS)]
            outs.append(functools.partial(norm_ffn, h))
            if c0 + group < n_sub:
                fills.extend(outs)
            else:
                while ffn_q:
                    ffn_q.pop(0)()
                for task in outs:
                    task()
                    if ffn_tail:
                        ffn_tail.pop(0)()
                while ffn_tail:
                    ffn_tail.pop(0)()
                if final:
                    ff_final_norm()

    if prompt:
        tailb_ref[...] = hist["b"]
        tailc_ref[...] = hist["c"]

        @pl.when(live)
        def _():
            vrow_ref[...] = hist["v"]
            convo_ref[...] = hist["b"][SUBLANES - (CONV_B - 1):, :]
            sconvo_ref[...] = hist["c"][SUBLANES - (CONV_C - 1):, :]

        @pl.when(live & (j == steps - 1))
        def _():
            ssmo_ref[...] = st_ref[...].T
    else:
        convo_ref[...] = cbufb_ref[:, 2 * SUBLANES - (CONV_B - 1):, :]
        sconvo_ref[...] = cbufc_ref[:, 2 * SUBLANES - (CONV_C - 1):, :]


def _ff_kernel(x_ref, g2_ref, w1_ref, w2_ref, gf_ref, o_ref, f_ref, *, final):
    x = x_ref[...]
    h = _bf(_rmsnorm(x, g2_ref[...]))
    for c in range(D_FF // FF_COLS):
        cs = slice(c * FF_COLS, (c + 1) * FF_COLS)
        f = jnp.maximum(_dot(h, w1_ref[:, cs]), 0.0)
        f_ref[:, cs] = _bf(f * f)
    y = x + _dot(f_ref[...], w2_ref[...])
    if final:
        y = _rmsnorm(y, gf_ref[...])
    o_ref[...] = y


def _layer_spec(shape, layer, n_grid):
    zeros = (0,) * len(shape)
    if n_grid == 1:
        index_map = lambda i: (layer,) + zeros
    else:
        index_map = lambda i, j: (layer,) + zeros
    return pl.BlockSpec((None,) + tuple(shape), index_map, pipeline_mode=pl.Buffered(1))


def _mix_call(prompt, layer, x2d, params, states=None, carry=None, seq=None, final=False):
    n_rows = x2d.shape[0]
    steps = None
    if prompt:
        rows = PROMPT_ROWS
        n_batch = n_rows // seq
        steps = seq // rows
        n_blocks = n_rows // rows
        grid = (n_blocks + 1,)
        xmap = lambda t: (jnp.minimum(t, n_blocks - 1), 0)
        omap = lambda t: (jnp.maximum(t - 1, 0), 0)
        smap = lambda t: (layer, jnp.minimum(t // steps, n_batch - 1), 0, 0)
        sblk = (None, None)
        state_rows = (CHUNK, CONV_B - 1, CONV_C - 1, C_WIDTH)
    else:
        rows = SAMPLE_SEQS * DEC_SEQ
        n_batch = n_rows // DEC_SEQ
        grid = (n_batch // SAMPLE_SEQS,)
        xmap = omap = lambda i: (i, 0)
        smap = lambda i: (layer, i, 0, 0)
        sblk = (None, SAMPLE_SEQS)
        state_rows = (DEC_SEQ, CONV_B - 1, CONV_C - 1, C_WIDTH)
    state_cols = (A_WIDTH, B_WIDTH, SSM_CONV_DIM, D_STATE)

    par_specs = [_layer_spec(p.shape[1:], layer, 1) if p.ndim > 2 else
                 pl.BlockSpec(p.shape, lambda i: (0, 0), pipeline_mode=pl.Buffered(1)) for p in params]
    in_specs = [pl.BlockSpec((rows, D_MODEL), xmap)] + par_specs
    args = [x2d] + list(params)
    if prompt:
        half = rows // 2
        last_half = n_rows // half - 1
        nmap = lambda t: (jnp.minimum(2 * t + 2, last_half), 0)
        in_specs.insert(1, pl.BlockSpec((half, D_MODEL), nmap))
        args.insert(1, x2d)
        scratch = [
            pltpu.VMEM((SUBLANES, B_WIDTH), F32),
            pltpu.VMEM((SUBLANES, SSM_CONV_DIM), F32),
            pltpu.VMEM((D_STATE, C_WIDTH), F32),
            pltpu.VMEM((half, COL_DT), F32),
            pltpu.VMEM((half, COL_DT), F32),
            pltpu.VMEM((SSM_HEADS, half), F32),
            pltpu.VMEM((SSM_HEADS, half), F32),
            pltpu.VMEM((half, D_MODEL), BF16),
            pltpu.VMEM((half, D_MODEL), BF16),
            pltpu.VMEM((half, D_MODEL), BF16),
            pltpu.VMEM((half, D_MODEL), BF16),
            pltpu.VMEM((rows, D_MODEL), F32),
            pltpu.VMEM((rows, D_MODEL), BF16),
            pltpu.VMEM((rows, D_FF), BF16),
        ]
        sem = ("arbitrary",)
    else:
        in_specs += [pl.BlockSpec(sblk + (r, w), smap)
                     for r, w in zip(state_rows[1:], state_cols[1:])]
        args += list(states)
        scratch = [
            pltpu.VMEM((SAMPLE_SEQS, 2 * SUBLANES, B_WIDTH), F32),
            pltpu.VMEM((SAMPLE_SEQS, 2 * SUBLANES, SSM_CONV_DIM), F32),
        ]
        sem = ("parallel",)

    out_shape = [jax.ShapeDtypeStruct((n_rows, D_MODEL), F32)]
    out_specs = [pl.BlockSpec((rows, D_MODEL), omap)]
    for r, w in zip(state_rows, state_cols):
        out_shape.append(jax.ShapeDtypeStruct((DEPTH, n_batch, r, w), F32))
        out_specs.append(pl.BlockSpec(sblk + (r, w), smap))
    n_in = len(args)
    aliases = {}
    if carry is not None:
        in_specs += [pl.BlockSpec(memory_space=pl.ANY)] * len(carry)
        args += list(carry)
        aliases = {n_in + k: 1 + k for k in range(len(carry))}

    return pl.pallas_call(
        functools.partial(_mix_kernel, prompt=prompt, rows=rows, n_in=n_in, n_carry=len(aliases),
                          steps=steps, final=final),
        grid=grid,
        in_specs=in_specs,
        out_specs=out_specs,
        out_shape=out_shape,
        scratch_shapes=scratch,
        input_output_aliases=aliases,
        compiler_params=pltpu.CompilerParams(dimension_semantics=sem, vmem_limit_bytes=VMEM_LIMIT),
        name="mix_prompt" if prompt else "mix_sample",
    )(*args)


def _ff_call(layer, x2d, g2, w1, w2, gf, final):
    n_rows = x2d.shape[0]
    return pl.pallas_call(
        functools.partial(_ff_kernel, final=final),
        grid=(n_rows // FF_ROWS,),
        in_specs=[
            pl.BlockSpec((FF_ROWS, D_MODEL), lambda i: (i, 0)),
            _layer_spec(g2.shape[1:], layer, 1),
            _layer_spec(w1.shape[1:], layer, 1),
            _layer_spec(w2.shape[1:], layer, 1),
            pl.BlockSpec(gf.shape, lambda i: (0, 0), pipeline_mode=pl.Buffered(1)),
        ],
        out_specs=pl.BlockSpec((FF_ROWS, D_MODEL), lambda i: (i, 0)),
        out_shape=jax.ShapeDtypeStruct((n_rows, D_MODEL), F32),
        scratch_shapes=[pltpu.VMEM((FF_ROWS, D_FF), BF16)],
        compiler_params=pltpu.CompilerParams(dimension_semantics=("parallel",),
                                             vmem_limit_bytes=VMEM_LIMIT),
        name="ffn",
    )(x2d, g2, w1, w2, gf)


def kernel(x_prompt, x_sample, state_conv, state_ssm_conv, state_ssm, norm1, w_in, w_s, b_s, conv_w,
           ssm_conv_w, ssm_conv_b, dt_bias, a_log, d_skip, ssm_norm, w_out, norm2, w_ff1, w_ff2,
           final_norm):
    bp, seq, _ = x_prompt.shape
    bs, dseq, _ = x_sample.shape
    assert seq % PROMPT_ROWS == 0 and dseq == DEC_SEQ and bs % SAMPLE_SEQS == 0

    wdt_b = _bf(jnp.pad(w_in[:, :, COL_DT:], ((0, 0), (0, 0), (0, D_IN_PAD - D_IN))))
    wdt_t = _bf(jnp.transpose(w_in[:, :, COL_DT:], (0, 2, 1)))
    win_all, wout_all, w1_all, w2_all = _bf(w_in), _bf(w_out), _bf(w_ff1), _bf(w_ff2)
    g1 = norm1.reshape(DEPTH, 1, D_MODEL)
    g2 = norm2.reshape(DEPTH, 1, D_MODEL)
    gf = final_norm.reshape(1, D_MODEL)
    wa_prompt = jnp.transpose(w_s, (0, 2, 1, 3)).reshape(DEPTH, CHUNK, A_HEADS * CHUNK)
    wa_sample = jnp.repeat(jnp.transpose(w_s[:, :, :DEC_SEQ, :DEC_SEQ], (0, 3, 2, 1)),
                           A_WIDTH // A_HEADS, axis=-1)
    ba = jnp.repeat(jnp.transpose(b_s, (0, 2, 1)), A_WIDTH // A_HEADS, axis=-1)
    scb = ssm_conv_b.reshape(DEPTH, 1, SSM_CONV_DIM)
    pad_heads = lambda p: jnp.pad(p, ((0, 0), (0, LANES - SSM_HEADS))).reshape(DEPTH, 1, LANES)
    dtb = pad_heads(dt_bias)
    alog = pad_heads(a_log)
    dsk = jnp.repeat(d_skip, SSM_HEAD_DIM, axis=-1).reshape(DEPTH, 1, C_WIDTH)
    snorm = ssm_norm.reshape(DEPTH, 1, C_WIDTH)
    over_time = lambda p: jnp.broadcast_to(p[:, :, None], (DEPTH, SSM_HEADS, CHUNK))
    mid_prompt = (wa_prompt, ba, conv_w, ssm_conv_w, scb, over_time(dt_bias), over_time(a_log), dsk, snorm)
    mid_sample = (wa_sample, ba, conv_w, ssm_conv_w, scb, dtb, alog, dsk, snorm)

    xp = x_prompt.reshape(bp * seq, D_MODEL)
    xs = x_sample.reshape(bs * dseq, D_MODEL)
    ssm_in = state_ssm.reshape(DEPTH, bs, C_WIDTH, D_STATE)
    st_p = st_s = None
    for l in range(DEPTH):
        final = l == DEPTH - 1
        win_b, wout_b, w1_b, w2_b = win_all, wout_all, w1_all, w2_all
        xp, *st_p = _mix_call(True, l, xp, (g1, win_b, wdt_t) + mid_prompt + (wout_b, g2, w1_b, w2_b, gf),
                              carry=st_p, seq=seq, final=final)
        xs, *st_s = _mix_call(False, l, xs, (g1, win_b, wdt_b) + mid_sample + (wout_b,),
                              (state_conv, state_ssm_conv, ssm_in), carry=st_s)
        xs = _ff_call(l, xs, g2, w1_b, w2_b, gf, final)

    y_prompt = xp.reshape(bp, seq, D_MODEL)
    y_sample = xs.reshape(bs, dseq, D_MODEL)
    ssm_shape = lambda b: (DEPTH, b, SSM_HEADS, SSM_HEAD_DIM, D_STATE)
    return (y_prompt, y_sample,
            st_p[0], st_p[1], st_p[2], st_p[3].reshape(ssm_shape(bp)),
            st_s[0], st_s[1], st_s[2], st_s[3].reshape(ssm_shape(bs)))
```

```python
import functools

import jax
import jax.numpy as jnp
from jax import lax
from jax.experimental import pallas as pl
from jax.experimental.pallas import tpu as pltpu

D_MODEL = 1024
DEPTH = 4
A_WIDTH = 256
A_HEADS = 4
B_WIDTH = 256
C_WIDTH = 512
CONV_B = 3
CONV_C = 4
SSM_HEADS = 8
SSM_HEAD_DIM = 64
SSM_GROUPS = 2
HEADS_PER_GROUP = SSM_HEADS // SSM_GROUPS
D_STATE = 128
SSM_CONV_DIM = 1024
CHUNK = 128
D_FF = 4096
EPS = 1e-5
DEC_SEQ = 8

COL_U, COL_V, COL_BG, COL_CG, COL_HB, COL_Z, COL_XBC, COL_DT = 0, 256, 512, 768, 1024, 1280, 1792, 2816
D_IN = 2824
LANES = 128
SUBLANES = 8
D_IN_PAD = 2944
GROUP_W = C_WIDTH // SSM_GROUPS

PROMPT_ROWS = 512
SAMPLE_SEQS = 16
FILL_COLS = 256
FILL_POINTS = 6
FFN_HEAD_START = 2
FF_DOWN_K = 1024
FFN_TAIL = 6
FF_ROWS = 1024
FF_COLS = 1024
VMEM_LIMIT = 56 * 1024 * 1024

F32 = jnp.float32
BF16 = jnp.bfloat16


def _bf(x):
    return x.astype(BF16)


def _dot(a, b):
    return jnp.dot(a, b, preferred_element_type=F32)


def _dot_nt(a, b):
    return lax.dot_general(a, b, (((1,), (1,)), ((), ())), preferred_element_type=F32)


def _split3(a):
    a1 = _bf(a)
    r1 = a - a1.astype(F32)
    a2 = _bf(r1)
    r2 = r1 - a2.astype(F32)
    return a1, a2, _bf(r2)


def _sel_left(m01, a):
    a1, a2, a3 = _split3(a)
    return _dot(m01, a1) + _dot(m01, a2) + _dot(m01, a3)


def _rmsnorm(x, g):
    ms = jnp.mean(x * x, axis=-1, keepdims=True)
    return (x * lax.rsqrt(ms + EPS)) * g


def _gelu(x):
    return 0.5 * x * (1.0 + lax.erf(x * (0.5 ** 0.5)))


def _silu(x):
    return x * jax.nn.sigmoid(x)


def _softplus(x):
    return jnp.maximum(x, 0.0) + jnp.log1p(jnp.exp(-jnp.abs(x)))


def _block_diag_pair(t):
    lane = lax.broadcasted_iota(jnp.int32, t.shape, 1)
    lo = jnp.where(lane < SSM_HEAD_DIM, t, 0.0)
    hi = jnp.where(lane >= SSM_HEAD_DIM, t, 0.0)
    return _bf(jnp.concatenate([lo, hi], axis=0))


def _causal_taps_rows(xin, prev8, w_ref, n_taps):
    row8 = lax.broadcasted_iota(jnp.int32, prev8.shape, 0)
    out = None
    for k in range(n_taps):
        back = n_taps - 1 - k
        if back == 0:
            sh = xin
        else:
            r = pltpu.roll(xin, back, axis=0)
            first = jnp.where(row8 >= back, r[0:SUBLANES], pltpu.roll(prev8, back, axis=0))
            sh = jnp.concatenate([first, r[SUBLANES:]], axis=0)
        term = sh * w_ref[k:k + 1, :]
        out = term if out is None else out + term
    return out


def _causal_taps_seqs(buf_ref, s0, n_seq, w_ref, n_taps, width):
    out = None
    for k in range(n_taps):
        back = n_taps - 1 - k
        sh = buf_ref[s0:s0 + n_seq, SUBLANES - back:2 * SUBLANES - back, :].reshape(n_seq * DEC_SEQ, width)
        term = sh * w_ref[k:k + 1, :]
        out = term if out is None else out + term
    return out


def _advance(gen):
    try:
        next(gen)
        return True
    except StopIteration:
        return False


def _mix_kernel(*refs, prompt, rows, n_in, n_carry, steps=None, final=False):
    refs = refs[:n_in] + refs[n_in + n_carry:]
    if prompt:
        (x_ref, xn_ref, g1_ref, win_ref, wdt_ref, wa_ref, ba_ref, cw_ref, scw_ref, scb_ref, dtb_ref, alog_ref,
         dsk_ref, snorm_ref, wout_ref, g2_ref, w1_ref, w2_ref, gf_ref,
         xo_ref, vrow_ref, convo_ref, sconvo_ref, ssmo_ref,
         tailb_ref, tailc_ref, st_ref, proja_ref, projb_ref, dtta_ref, dttb_ref, hna_ref, hnb_ref,
         mixa_ref, mixb_ref,
         xm_ref, hff_ref, f_ref) = refs
    else:
        (x_ref, g1_ref, win_ref, wdt_ref, wa_ref, ba_ref, cw_ref, scw_ref, scb_ref, dtb_ref, alog_ref,
         dsk_ref, snorm_ref, wout_ref, conv0_ref, sconv0_ref, ssm0_ref,
         xo_ref, vrow_ref, convo_ref, sconvo_ref, ssmo_ref,
         cbufb_ref, cbufc_ref) = refs
    n_sub = rows // CHUNK
    seqs_per_sub = CHUNK // DEC_SEQ

    def in_proj(x_rows):
        h = _bf(_rmsnorm(x_rows, g1_ref[...]))
        return jnp.concatenate([_dot(h, win_ref[:, 0:COL_DT]), _dot(h, wdt_ref[...])], axis=1)

    ups_left = [D_FF // FILL_COLS]

    def ff_up(k):
        ups_left[0] -= 1
        cs = slice(k * FILL_COLS, (k + 1) * FILL_COLS)
        f = jnp.maximum(_dot(hff_ref[...], w1_ref[:, cs]), 0.0)
        f_ref[:, cs] = _bf(f * f)

    if prompt:
        half = rows // 2
        subs_per_half = n_sub // 2
        n_fill = -(-D_IN_PAD // FILL_COLS)
        t = pl.program_id(0)
        j = t % steps
        live = t < pl.num_programs(0) - 1

        @pl.when(j == 0)
        def _():
            tailb_ref[...] = jnp.zeros(tailb_ref.shape, F32)
            tailc_ref[...] = jnp.zeros(tailc_ref.shape, F32)
            st_ref[...] = jnp.zeros(st_ref.shape, F32)

        @pl.when(t == 0)
        def _():
            h0 = _bf(_rmsnorm(x_ref[0:half, :], g1_ref[...]))
            proja_ref[...] = _dot(h0, win_ref[:, 0:COL_DT])
            dtta_ref[...] = _dot_nt(wdt_ref[...], h0)
            xm_ref[...] = jnp.zeros(xm_ref.shape, F32)
            hff_ref[...] = jnp.zeros(hff_ref.shape, BF16)

        for k in range(FFN_HEAD_START):
            ff_up(k)
        xo_ref[...] = xm_ref[...]
        hn_refs = (hna_ref, hnb_ref)
        hna_ref[...] = _bf(_rmsnorm(x_ref[half:rows, :], g1_ref[...]))
    else:
        cbufb_ref[:, SUBLANES - (CONV_B - 1):SUBLANES, :] = conv0_ref[...]
        cbufc_ref[:, SUBLANES - (CONV_C - 1):SUBLANES, :] = sconv0_ref[...]

    row = lax.broadcasted_iota(jnp.int32, (CHUNK, CHUNK), 0)
    col = lax.broadcasted_iota(jnp.int32, (CHUNK, CHUNK), 1)
    if prompt:
        causal = col <= row
    else:
        causal = ((row >> 3) == (col >> 3)) & (col <= row)
    causal01 = _bf(jnp.where(causal, 1.0, 0.0))
    erow = lax.broadcasted_iota(jnp.int32, (LANES, C_WIDTH), 0)
    ecol = lax.broadcasted_iota(jnp.int32, (LANES, C_WIDTH), 1)
    expand01 = _bf(jnp.where((ecol >> 6) == erow, 1.0, 0.0))

    def expand(a):
        a1 = _bf(a)
        a2 = _bf(a - a1.astype(F32))
        return _dot(a1, expand01) + _dot(a2, expand01)

    a_neg = -jnp.exp(alog_ref[...])
    if prompt:
        upper01 = _bf(jnp.where(row <= col, 1.0, 0.0))
        arow = lax.broadcasted_iota(jnp.int32, (CHUNK, A_HEADS * CHUNK), 0)
        acol = lax.broadcasted_iota(jnp.int32, (CHUNK, A_HEADS * CHUNK), 1)
        wa_b = _bf(jnp.where((acol & (CHUNK - 1)) <= arow, wa_ref[...], 0.0))
        hist = {"b": tailb_ref[...], "c": tailc_ref[...]}
        mix_refs = (mixa_ref, mixb_ref)

    fills = []
    points_left = [0]
    ffn_q = []
    step_points_left = [FILL_POINTS * n_sub]

    def fill_cols(dst, src, k):
        if k * FILL_COLS < COL_DT:
            cs = slice(k * FILL_COLS, (k + 1) * FILL_COLS)
            dst[:, cs] = _dot(src[...], win_ref[:, cs])
        else:
            dtt = dttb_ref if dst is projb_ref else dtta_ref
            dtt[...] = _dot_nt(wdt_ref[...], src[...])

    def norm_next_step():
        hnb_ref[...] = _bf(_rmsnorm(xn_ref[...], g1_ref[...]))

    def norm_ffn(h):
        assert ups_left[0] == 0, "hff_ref is still being read by this step's up-projections"
        rs = slice(h * half, (h + 1) * half)
        hff_ref[rs, :] = _bf(_rmsnorm(xm_ref[rs, :], g2_ref[...]))

    def ff_down(kc, nc):
        ks = slice(kc * FF_DOWN_K, (kc + 1) * FF_DOWN_K)
        cs = slice(nc * FILL_COLS, (nc + 1) * FILL_COLS)
        xo_ref[:, cs] += _dot(f_ref[:, ks], w2_ref[ks, cs])

    def ff_final_norm():
        xo_ref[...] = _rmsnorm(xo_ref[...], gf_ref[...])

    if prompt:
        ffn_q.extend(functools.partial(ff_up, k) for k in range(FFN_HEAD_START, D_FF // FILL_COLS))
        for kc in range(D_FF // FF_DOWN_K):
            ffn_q.extend(functools.partial(ff_down, kc, nc) for nc in range(D_MODEL // FILL_COLS))
        ffn_tail = ffn_q[-FFN_TAIL:]
        del ffn_q[-FFN_TAIL:]

    def fill():
        for queue, left in ((fills, points_left), (ffn_q, step_points_left)):
            n = -(-len(queue) // max(left[0], 1))
            left[0] -= 1
            for _ in range(min(n, len(queue))):
                queue.pop(0)()

    def sub_block(c):
        sl = slice(c * CHUNK, (c + 1) * CHUNK)
        x = x_ref[sl, :]
        if prompt:
            if c % subs_per_half == 0:
                h = c // subs_per_half
                dst = projb_ref if h == 0 else proja_ref
                fills.extend(functools.partial(fill_cols, dst, hn_refs[h], k) for k in range(n_fill))
                if h == 0:
                    fills.append(norm_next_step)
            pref = proja_ref if c < subs_per_half else projb_ref
            lr = slice((c % subs_per_half) * CHUNK, (c % subs_per_half + 1) * CHUNK)
            P = lambda c0, w, pref=pref, lr=lr: pref[lr, c0:c0 + w]
        else:
            proj = in_proj(x)
            P = lambda c0, w, proj=proj: proj[:, c0:c0 + w]

        if prompt:
            dtt = dtta_ref if c < subs_per_half else dttb_ref
            dt_t = _softplus(dtt[:, lr] + dtb_ref[...])
            a = dt_t * a_neg
            a1 = _bf(a).astype(F32)
            a2 = _bf(a - a1).astype(F32)
            r = _dot(_bf(jnp.concatenate([a1, a2, a - a1 - a2, jnp.zeros_like(a)], axis=0)), upper01)
            cum_t = r[0:8] + r[8:16] + r[16:24]
            expcum_t = jnp.exp(cum_t)
            wst_t = jnp.exp(cum_t[:, CHUNK - 1:CHUNK] - cum_t) * dt_t
            pack = jnp.concatenate([wst_t, expcum_t, cum_t,
                                    jnp.zeros((CHUNK - 3 * SSM_HEADS, CHUNK), F32)], axis=0)
            cols = pack.T

            def per_channel(lane0):
                tiles = []
                for p in range(SSM_HEADS // 2):
                    even = jnp.broadcast_to(cols[:, lane0 + 2 * p:lane0 + 2 * p + 1], (CHUNK, LANES))
                    odd = jnp.broadcast_to(cols[:, lane0 + 2 * p + 1:lane0 + 2 * p + 2], (CHUNK, LANES))
                    tiles.append(jnp.where(col < SSM_HEAD_DIM, even, odd))
                return jnp.concatenate(tiles, axis=1)

            w_state_x, expcum_x = per_channel(0), per_channel(SSM_HEADS)
            cum, cum_lane0 = cols, 2 * SSM_HEADS
        else:
            cum_lane0 = 0
            dt = _softplus(P(COL_DT, LANES) + dtb_ref[...])
            cum = _sel_left(causal01, dt * a_neg)
            cum3 = cum.reshape(seqs_per_sub, DEC_SEQ, LANES)
            tot = jnp.broadcast_to(cum3[:, DEC_SEQ - 1:DEC_SEQ, :], cum3.shape).reshape(CHUNK, LANES)
            expcum = jnp.exp(cum)
            w_state = jnp.exp(tot - cum) * dt
            ex = expand(jnp.concatenate([w_state, expcum], axis=0))
            w_state_x, expcum_x = ex[0:CHUNK], ex[CHUNK:]
            cum_t = cum.T
            dt_t = dt.T

        u = _gelu(P(COL_U, A_WIDTH))
        v = _gelu(P(COL_V, A_WIDTH))
        if prompt:
            pairs = []
            for p in range(A_HEADS // 2):
                rhs = _block_diag_pair(v[:, p * LANES:(p + 1) * LANES])
                pairs.append(_dot(wa_b[:, p * 2 * CHUNK:(p + 1) * 2 * CHUNK], rhs))
            s = jnp.concatenate(pairs, axis=1) + ba_ref[...]
            if c == n_sub - 1:
                hist["v"] = v
        else:
            s0 = c * seqs_per_sub
            v3 = v.reshape(seqs_per_sub, DEC_SEQ, A_WIDTH)
            trow = lax.broadcasted_iota(jnp.int32, (DEC_SEQ, A_WIDTH), 0)
            s3 = jnp.zeros((seqs_per_sub, DEC_SEQ, A_WIDTH), F32)
            for jj in range(DEC_SEQ):
                coef = jnp.where(trow >= jj, wa_ref[jj], 0.0)
                s3 = s3 + coef[None] * v3[:, jj:jj + 1, :]
            s = (s3 + ba_ref[0:DEC_SEQ, :][None]).reshape(CHUNK, A_WIDTH)
            vrow_ref[s0:s0 + seqs_per_sub] = v3
        ya = u * s
        fill()
        yield

        cg_in = P(COL_CG, B_WIDTH) * P(COL_HB, B_WIDTH)
        xbc_in = P(COL_XBC, SSM_CONV_DIM)
        if prompt:
            conv = _causal_taps_rows(cg_in, hist["b"], cw_ref, CONV_B)
            xbc = _causal_taps_rows(xbc_in, hist["c"], scw_ref, CONV_C)
            hist["b"] = cg_in[CHUNK - SUBLANES:, :]
            hist["c"] = xbc_in[CHUNK - SUBLANES:, :]
        else:
            cbufb_ref[s0:s0 + seqs_per_sub, SUBLANES:, :] = cg_in.reshape(seqs_per_sub, DEC_SEQ, B_WIDTH)
            cbufc_ref[s0:s0 + seqs_per_sub, SUBLANES:, :] = xbc_in.reshape(seqs_per_sub, DEC_SEQ, SSM_CONV_DIM)
            conv = _causal_taps_seqs(cbufb_ref, s0, seqs_per_sub, cw_ref, CONV_B, B_WIDTH)
            xbc = _causal_taps_seqs(cbufc_ref, s0, seqs_per_sub, scw_ref, CONV_C, SSM_CONV_DIM)
        yb = P(COL_BG, B_WIDTH) * conv
        fill()
        yield

        xbc = _silu(xbc + scb_ref[...])
        xs = xbc[:, 0:C_WIDTH]
        bm = xbc[:, C_WIDTH:C_WIDTH + SSM_GROUPS * D_STATE]
        cm = xbc[:, C_WIDTH + SSM_GROUPS * D_STATE:]
        xw = xs * w_state_x
        bm_b = _bf(bm)
        cm_b = _bf(cm)
        fill()
        yield

        zero_blk = jnp.zeros((CHUNK, D_STATE), BF16)
        b_diag = jnp.concatenate([jnp.concatenate([bm_b[:, 0:D_STATE], zero_blk], axis=1),
                                  jnp.concatenate([zero_blk, bm_b[:, D_STATE:]], axis=1)], axis=0)
        cb_all = _dot_nt(cm_b, b_diag)
        y_pairs = []
        for g in range(SSM_GROUPS):
            gs = slice(g * D_STATE, (g + 1) * D_STATE)
            cb = cb_all[:, gs]
            wts = []
            for kk in range(HEADS_PER_GROUP):
                k = g * HEADS_PER_GROUP + kk
                seg = cum[:, cum_lane0 + k:cum_lane0 + k + 1] - cum_t[k:k + 1, :]
                decay = jnp.exp(jnp.where(causal, seg, -jnp.inf))
                wts.append(_bf(cb * decay * dt_t[k:k + 1, :]))
            for p in range(HEADS_PER_GROUP // 2):
                k0 = g * HEADS_PER_GROUP + 2 * p
                lhs = jnp.concatenate([wts[2 * p], wts[2 * p + 1]], axis=1)
                rhs = _block_diag_pair(xs[:, k0 * SSM_HEAD_DIM:(k0 + 2) * SSM_HEAD_DIM])
                y_pairs.append(_dot(lhs, rhs))
            fill()
            yield
        y = jnp.concatenate(y_pairs, axis=1)

        if prompt:
            xw_b = _bf(xw)
            y_off = []
            for g in range(SSM_GROUPS):
                gs = slice(g * D_STATE, (g + 1) * D_STATE)
                gw = slice(g * GROUP_W, (g + 1) * GROUP_W)
                st_g = st_ref[:, gw]
                y_off.append(_dot(cm_b[:, gs], _bf(st_g)))
                st_ref[:, gw] = expcum_x[CHUNK - 1:CHUNK, gw] * st_g + _dot(_bf(bm[:, gs].T), xw_b[:, gw])
            y = y + jnp.concatenate(y_off, axis=1) * expcum_x
        else:
            exptot = jnp.exp(tot)
            xw_t = [_bf(xw[:, g * GROUP_W:(g + 1) * GROUP_W].T) for g in range(SSM_GROUPS)]
            y_off = []
            for i in range(seqs_per_sub):
                rs = slice(i * DEC_SEQ, (i + 1) * DEC_SEQ)
                own_rows = (row >> 3) == i
                parts = []
                for g in range(SSM_GROUPS):
                    gs = slice(g * D_STATE, (g + 1) * D_STATE)
                    gw = slice(g * GROUP_W, (g + 1) * GROUP_W)
                    h0 = ssm0_ref[s0 + i, gw, :]
                    parts.append(_dot_nt(_bf(cm[rs, gs]), _bf(h0)))
                    upd = _dot(xw_t[g], _bf(jnp.where(own_rows, bm[:, gs], 0.0)))
                    for kk in range(HEADS_PER_GROUP):
                        k = g * HEADS_PER_GROUP + kk
                        hs = slice(kk * SSM_HEAD_DIM, (kk + 1) * SSM_HEAD_DIM)
                        dec = jnp.broadcast_to(exptot[i * DEC_SEQ:i * DEC_SEQ + 1, k:k + 1],
                                               (SSM_HEAD_DIM, D_STATE))
                        ssmo_ref[s0 + i, k * SSM_HEAD_DIM:(k + 1) * SSM_HEAD_DIM, :] = (
                            dec * h0[hs, :] + upd[hs, :])
                y_off.append(jnp.concatenate(parts, axis=1))
            y = y + jnp.concatenate(y_off, axis=0) * expcum_x

        y = y + dsk_ref[...] * xs
        yc = _rmsnorm(y * _silu(P(COL_Z, C_WIDTH)), snorm_ref[...])
        fill()
        yield
        mix = jnp.concatenate([_bf(ya), _bf(yb), _bf(yc)], axis=1)
        if prompt:
            mix_refs[c // subs_per_half][lr, :] = mix
        else:
            xo_ref[sl, :] = x + _dot(mix, wout_ref[...])

    def out_cols(h, k):
        rs = slice(h * half, (h + 1) * half)
        cs = slice(k * FILL_COLS, (k + 1) * FILL_COLS)
        xm_ref[rs, cs] = x_ref[rs, cs] + _dot(mix_refs[h][...], wout_ref[:, cs])

    group = subs_per_half if prompt else 1
    for c0 in range(0, n_sub, group):
        if prompt:
            points_left[0] = FILL_POINTS * group
        gens = [sub_block(c) for c in range(c0, c0 + group)]
        while gens:
            gens = [g for g in gens if _advance(g)]
        while fills:
            fills.pop(0)()
        if prompt:
            h = c0 // group
            outs = [functools.partial(out_cols, h, k) for k in range(D_MODEL // FILL_COLS)]
            outs.append(functools.partial(norm_ffn, h))
            if c0 + group < n_sub:
                fills.extend(outs)
            else:
                while ffn_q:
                    ffn_q.pop(0)()
                for task in outs:
                    task()
                    if ffn_tail:
                        ffn_tail.pop(0)()
                while ffn_tail:
                    ffn_tail.pop(0)()
                if final:
                    ff_final_norm()

    if prompt:
        tailb_ref[...] = hist["b"]
        tailc_ref[...] = hist["c"]

        @pl.when(live)
        def _():
            vrow_ref[...] = hist["v"]
            convo_ref[...] = hist["b"][SUBLANES - (CONV_B - 1):, :]
            sconvo_ref[...] = hist["c"][SUBLANES - (CONV_C - 1):, :]

        @pl.when(live & (j == steps - 1))
        def _():
            ssmo_ref[...] = st_ref[...].T
    else:
        convo_ref[...] = cbufb_ref[:, 2 * SUBLANES - (CONV_B - 1):, :]
        sconvo_ref[...] = cbufc_ref[:, 2 * SUBLANES - (CONV_C - 1):, :]


def _ff_kernel(x_ref, g2_ref, w1_ref, w2_ref, gf_ref, o_ref, h_ref, *, final):
    c = pl.program_id(0)

    @pl.when(c == 0)
    def _():
        x = x_ref[...]
        h_ref[...] = _bf(_rmsnorm(x, g2_ref[...]))
        o_ref[...] = x

    f = jnp.maximum(_dot(h_ref[...], w1_ref[...]), 0.0)
    o_ref[...] += _dot(_bf(f * f), w2_ref[...])
    if final:
        @pl.when(c == pl.num_programs(0) - 1)
        def _():
            o_ref[...] = _rmsnorm(o_ref[...], gf_ref[...])


def _layer_spec(shape, layer, n_grid):
    zeros = (0,) * len(shape)
    if n_grid == 1:
        index_map = lambda i: (layer,) + zeros
    else:
        index_map = lambda i, j: (layer,) + zeros
    return pl.BlockSpec((None,) + tuple(shape), index_map, pipeline_mode=pl.Buffered(1))


def _mix_call(prompt, layer, x2d, params, states=None, carry=None, seq=None, final=False):
    n_rows = x2d.shape[0]
    steps = None
    if prompt:
        rows = PROMPT_ROWS
        n_batch = n_rows // seq
        steps = seq // rows
        n_blocks = n_rows // rows
        grid = (n_blocks + 1,)
        xmap = lambda t: (jnp.minimum(t, n_blocks - 1), 0)
        omap = lambda t: (jnp.maximum(t - 1, 0), 0)
        smap = lambda t: (layer, jnp.minimum(t // steps, n_batch - 1), 0, 0)
        sblk = (None, None)
        state_rows = (CHUNK, CONV_B - 1, CONV_C - 1, C_WIDTH)
    else:
        rows = SAMPLE_SEQS * DEC_SEQ
        n_batch = n_rows // DEC_SEQ
        grid = (n_batch // SAMPLE_SEQS,)
        xmap = omap = lambda i: (i, 0)
        smap = lambda i: (layer, i, 0, 0)
        sblk = (None, SAMPLE_SEQS)
        state_rows = (DEC_SEQ, CONV_B - 1, CONV_C - 1, C_WIDTH)
    state_cols = (A_WIDTH, B_WIDTH, SSM_CONV_DIM, D_STATE)

    par_specs = [_layer_spec(p.shape[1:], layer, 1) if p.ndim > 2 else
                 pl.BlockSpec(p.shape, lambda i: (0, 0), pipeline_mode=pl.Buffered(1)) for p in params]
    in_specs = [pl.BlockSpec((rows, D_MODEL), xmap)] + par_specs
    args = [x2d] + list(params)
    if prompt:
        half = rows // 2
        last_half = n_rows // half - 1
        nmap = lambda t: (jnp.minimum(2 * t + 2, last_half), 0)
        in_specs.insert(1, pl.BlockSpec((half, D_MODEL), nmap))
        args.insert(1, x2d)
        scratch = [
            pltpu.VMEM((SUBLANES, B_WIDTH), F32),
            pltpu.VMEM((SUBLANES, SSM_CONV_DIM), F32),
            pltpu.VMEM((D_STATE, C_WIDTH), F32),
            pltpu.VMEM((half, COL_DT), F32),
            pltpu.VMEM((half, COL_DT), F32),
            pltpu.VMEM((SSM_HEADS, half), F32),
            pltpu.VMEM((SSM_HEADS, half), F32),
            pltpu.VMEM((half, D_MODEL), BF16),
            pltpu.VMEM((half, D_MODEL), BF16),
            pltpu.VMEM((half, D_MODEL), BF16),
            pltpu.VMEM((half, D_MODEL), BF16),
            pltpu.VMEM((rows, D_MODEL), F32),
            pltpu.VMEM((rows, D_MODEL), BF16),
            pltpu.VMEM((rows, D_FF), BF16),
        ]
        sem = ("arbitrary",)
    else:
        in_specs += [pl.BlockSpec(sblk + (r, w), smap)
                     for r, w in zip(state_rows[1:], state_cols[1:])]
        args += list(states)
        scratch = [
            pltpu.VMEM((SAMPLE_SEQS, 2 * SUBLANES, B_WIDTH), F32),
            pltpu.VMEM((SAMPLE_SEQS, 2 * SUBLANES, SSM_CONV_DIM), F32),
        ]
        sem = ("parallel",)

    out_shape = [jax.ShapeDtypeStruct((n_rows, D_MODEL), F32)]
    out_specs = [pl.BlockSpec((rows, D_MODEL), omap)]
    for r, w in zip(state_rows, state_cols):
        out_shape.append(jax.ShapeDtypeStruct((DEPTH, n_batch, r, w), F32))
        out_specs.append(pl.BlockSpec(sblk + (r, w), smap))
    n_in = len(args)
    aliases = {}
    if carry is not None:
        in_specs += [pl.BlockSpec(memory_space=pl.ANY)] * len(carry)
        args += list(carry)
        aliases = {n_in + k: 1 + k for k in range(len(carry))}

    return pl.pallas_call(
        functools.partial(_mix_kernel, prompt=prompt, rows=rows, n_in=n_in, n_carry=len(aliases),
                          steps=steps, final=final),
        grid=grid,
        in_specs=in_specs,
        out_specs=out_specs,
        out_shape=out_shape,
        scratch_shapes=scratch,
        input_output_aliases=aliases,
        compiler_params=pltpu.CompilerParams(dimension_semantics=sem, vmem_limit_bytes=VMEM_LIMIT),
        name="mix_prompt" if prompt else "mix_sample",
    )(*args)


def _ff_call(layer, x2d, g2, w1, w2, gf, final):
    n_rows = x2d.shape[0]
    assert n_rows == FF_ROWS
    return pl.pallas_call(
        functools.partial(_ff_kernel, final=final),
        grid=(D_FF // FF_COLS,),
        in_specs=[
            pl.BlockSpec((FF_ROWS, D_MODEL), lambda c: (0, 0), pipeline_mode=pl.Buffered(1)),
            _layer_spec(g2.shape[1:], layer, 1),
            pl.BlockSpec((None, D_MODEL, FF_COLS), lambda c: (layer, 0, c)),
            pl.BlockSpec((None, FF_COLS, D_MODEL), lambda c: (layer, c, 0)),
            pl.BlockSpec(gf.shape, lambda c: (0, 0), pipeline_mode=pl.Buffered(1)),
        ],
        out_specs=pl.BlockSpec((FF_ROWS, D_MODEL), lambda c: (0, 0)),
        out_shape=jax.ShapeDtypeStruct((n_rows, D_MODEL), F32),
        scratch_shapes=[pltpu.VMEM((FF_ROWS, D_MODEL), BF16)],
        compiler_params=pltpu.CompilerParams(dimension_semantics=("arbitrary",),
                                             vmem_limit_bytes=VMEM_LIMIT),
        name="ffn",
    )(x2d, g2, w1, w2, gf)


def kernel(x_prompt, x_sample, state_conv, state_ssm_conv, state_ssm, norm1, w_in, w_s, b_s, conv_w,
           ssm_conv_w, ssm_conv_b, dt_bias, a_log, d_skip, ssm_norm, w_out, norm2, w_ff1, w_ff2,
           final_norm):
    bp, seq, _ = x_prompt.shape
    bs, dseq, _ = x_sample.shape
    assert seq % PROMPT_ROWS == 0 and dseq == DEC_SEQ and bs % SAMPLE_SEQS == 0

    wdt_b = _bf(jnp.pad(w_in[:, :, COL_DT:], ((0, 0), (0, 0), (0, D_IN_PAD - D_IN))))
    wdt_t = _bf(jnp.transpose(w_in[:, :, COL_DT:], (0, 2, 1)))
    win_all, wout_all, w1_all, w2_all = _bf(w_in), _bf(w_out), _bf(w_ff1), _bf(w_ff2)
    g1 = norm1.reshape(DEPTH, 1, D_MODEL)
    g2 = norm2.reshape(DEPTH, 1, D_MODEL)
    gf = final_norm.reshape(1, D_MODEL)
    wa_prompt = jnp.transpose(w_s, (0, 2, 1, 3)).reshape(DEPTH, CHUNK, A_HEADS * CHUNK)
    wa_sample = jnp.repeat(jnp.transpose(w_s[:, :, :DEC_SEQ, :DEC_SEQ], (0, 3, 2, 1)),
                           A_WIDTH // A_HEADS, axis=-1)
    ba = jnp.repeat(jnp.transpose(b_s, (0, 2, 1)), A_WIDTH // A_HEADS, axis=-1)
    scb = ssm_conv_b.reshape(DEPTH, 1, SSM_CONV_DIM)
    pad_heads = lambda p: jnp.pad(p, ((0, 0), (0, LANES - SSM_HEADS))).reshape(DEPTH, 1, LANES)
    dtb = pad_heads(dt_bias)
    alog = pad_heads(a_log)
    dsk = jnp.repeat(d_skip, SSM_HEAD_DIM, axis=-1).reshape(DEPTH, 1, C_WIDTH)
    snorm = ssm_norm.reshape(DEPTH, 1, C_WIDTH)
    over_time = lambda p: jnp.broadcast_to(p[:, :, None], (DEPTH, SSM_HEADS, CHUNK))
    mid_prompt = (wa_prompt, ba, conv_w, ssm_conv_w, scb, over_time(dt_bias), over_time(a_log), dsk, snorm)
    mid_sample = (wa_sample, ba, conv_w, ssm_conv_w, scb, dtb, alog, dsk, snorm)

    xp = x_prompt.reshape(bp * seq, D_MODEL)
    xs = x_sample.reshape(bs * dseq, D_MODEL)
    ssm_in = state_ssm.reshape(DEPTH, bs, C_WIDTH, D_STATE)
    st_p = st_s = None
    for l in range(DEPTH):
        final = l == DEPTH - 1
        win_b, wout_b, w1_b, w2_b = win_all, wout_all, w1_all, w2_all
        xp, *st_p = _mix_call(True, l, xp, (g1, win_b, wdt_t) + mid_prompt + (wout_b, g2, w1_b, w2_b, gf),
                              carry=st_p, seq=seq, final=final)
        xs, *st_s = _mix_call(False, l, xs, (g1, win_b, wdt_b) + mid_sample + (wout_b,),
                              (state_conv, state_ssm_conv, ssm_in), carry=st_s)
        xs = _ff_call(l, xs, g2, w1_b, w2_b, gf, final)

    y_prompt = xp.reshape(bp, seq, D_MODEL)
    y_sample = xs.reshape(bs, dseq, D_MODEL)
    ssm_shape = lambda b: (DEPTH, b, SSM_HEADS, SSM_HEAD_DIM, D_STATE)
    return (y_prompt, y_sample,
            st_p[0], st_p[1], st_p[2], st_p[3].reshape(ssm_shape(bp)),
            st_s[0], st_s[1], st_s[2], st_s[3].reshape(ssm_shape(bs)))
```

```python
import functools

import jax
import jax.numpy as jnp
from jax import lax
from jax.experimental import pallas as pl
from jax.experimental.pallas import tpu as pltpu

D_MODEL = 1024
DEPTH = 4
A_WIDTH = 256
A_HEADS = 4
B_WIDTH = 256
C_WIDTH = 512
CONV_B = 3
CONV_C = 4
SSM_HEADS = 8
SSM_HEAD_DIM = 64
SSM_GROUPS = 2
HEADS_PER_GROUP = SSM_HEADS // SSM_GROUPS
D_STATE = 128
SSM_CONV_DIM = 1024
CHUNK = 128
D_FF = 4096
EPS = 1e-5
DEC_SEQ = 8

COL_U, COL_V, COL_BG, COL_CG, COL_HB, COL_Z, COL_XBC, COL_DT = 0, 256, 512, 768, 1024, 1280, 1792, 2816
D_IN = 2824
LANES = 128
SUBLANES = 8
D_IN_PAD = 2944
GROUP_W = C_WIDTH // SSM_GROUPS

PROMPT_ROWS = 512
SAMPLE_SEQS = 16
FILL_COLS = 256
FILL_POINTS = 6
FFN_HEAD_START = 2
FF_DOWN_K = 1024
FFN_TAIL = 6
FF_ROWS = 1024
FF_COLS = 1024
VMEM_LIMIT = 56 * 1024 * 1024

F32 = jnp.float32
BF16 = jnp.bfloat16


def _bf(x):
    return x.astype(BF16)


def _dot(a, b):
    return jnp.dot(a, b, preferred_element_type=F32)


def _dot_nt(a, b):
    return lax.dot_general(a, b, (((1,), (1,)), ((), ())), preferred_element_type=F32)


def _split3(a):
    a1 = _bf(a)
    r1 = a - a1.astype(F32)
    a2 = _bf(r1)
    r2 = r1 - a2.astype(F32)
    return a1, a2, _bf(r2)


def _sel_left(m01, a):
    a1, a2, a3 = _split3(a)
    return _dot(m01, a1) + _dot(m01, a2) + _dot(m01, a3)


def _rmsnorm(x, g):
    ms = jnp.mean(x * x, axis=-1, keepdims=True)
    return (x * lax.rsqrt(ms + EPS)) * g


def _gelu(x):
    return 0.5 * x * (1.0 + lax.erf(x * (0.5 ** 0.5)))


def _silu(x):
    return x * jax.nn.sigmoid(x)


def _softplus(x):
    return jnp.maximum(x, 0.0) + jnp.log1p(jnp.exp(-jnp.abs(x)))


def _block_diag_pair(t):
    lane = lax.broadcasted_iota(jnp.int32, t.shape, 1)
    lo = jnp.where(lane < SSM_HEAD_DIM, t, 0.0)
    hi = jnp.where(lane >= SSM_HEAD_DIM, t, 0.0)
    return _bf(jnp.concatenate([lo, hi], axis=0))


def _causal_taps_rows(xin, prev8, w_ref, n_taps):
    row8 = lax.broadcasted_iota(jnp.int32, prev8.shape, 0)
    out = None
    for k in range(n_taps):
        back = n_taps - 1 - k
        if back == 0:
            sh = xin
        else:
            r = pltpu.roll(xin, back, axis=0)
            first = jnp.where(row8 >= back, r[0:SUBLANES], pltpu.roll(prev8, back, axis=0))
            sh = jnp.concatenate([first, r[SUBLANES:]], axis=0)
        term = sh * w_ref[k:k + 1, :]
        out = term if out is None else out + term
    return out


def _causal_taps_seqs(buf_ref, s0, n_seq, w_ref, n_taps, width):
    out = None
    for k in range(n_taps):
        back = n_taps - 1 - k
        sh = buf_ref[s0:s0 + n_seq, SUBLANES - back:2 * SUBLANES - back, :].reshape(n_seq * DEC_SEQ, width)
        term = sh * w_ref[k:k + 1, :]
        out = term if out is None else out + term
    return out


def _advance(gen):
    try:
        next(gen)
        return True
    except StopIteration:
        return False


def _mix_kernel(*refs, prompt, rows, n_in, n_carry, steps=None, final=False):
    refs = refs[:n_in] + refs[n_in + n_carry:]
    if prompt:
        (x_ref, xn_ref, g1_ref, win_ref, wdt_ref, wa_ref, ba_ref, cw_ref, scw_ref, scb_ref, dtb_ref, alog_ref,
         dsk_ref, snorm_ref, wout_ref, g2_ref, w1_ref, w2_ref, gf_ref,
         xo_ref, vrow_ref, convo_ref, sconvo_ref, ssmo_ref,
         tailb_ref, tailc_ref, st_ref, proja_ref, projb_ref, dtta_ref, dttb_ref, hna_ref, hnb_ref,
         mixa_ref, mixb_ref,
         xm_ref, hff_ref, f_ref) = refs
    else:
        (x_ref, g1_ref, win_ref, wdt_ref, wa_ref, ba_ref, cw_ref, scw_ref, scb_ref, dtb_ref, alog_ref,
         dsk_ref, snorm_ref, wout_ref, conv0_ref, sconv0_ref, ssm0_ref,
         xo_ref, vrow_ref, convo_ref, sconvo_ref, ssmo_ref,
         cbufb_ref, cbufc_ref) = refs
    n_sub = rows // CHUNK
    seqs_per_sub = CHUNK // DEC_SEQ

    def in_proj(x_rows):
        h = _bf(_rmsnorm(x_rows, g1_ref[...]))
        return jnp.concatenate([_dot(h, win_ref[:, 0:COL_DT]), _dot(h, wdt_ref[...])], axis=1)

    ups_left = [D_FF // FILL_COLS]

    def ff_up(k):
        ups_left[0] -= 1
        cs = slice(k * FILL_COLS, (k + 1) * FILL_COLS)
        f = jnp.maximum(_dot(hff_ref[...], w1_ref[:, cs]), 0.0)
        f_ref[:, cs] = _bf(f * f)

    if prompt:
        half = rows // 2
        subs_per_half = n_sub // 2
        n_fill = -(-D_IN_PAD // FILL_COLS)
        t = pl.program_id(0)
        j = t % steps
        live = t < pl.num_programs(0) - 1

        @pl.when(j == 0)
        def _():
            tailb_ref[...] = jnp.zeros(tailb_ref.shape, F32)
            tailc_ref[...] = jnp.zeros(tailc_ref.shape, F32)
            st_ref[...] = jnp.zeros(st_ref.shape, F32)

        @pl.when(t == 0)
        def _():
            h0 = _bf(_rmsnorm(x_ref[0:half, :], g1_ref[...]))
            proja_ref[...] = _dot(h0, win_ref[:, 0:COL_DT])
            dtta_ref[...] = _dot_nt(wdt_ref[...], h0)
            xm_ref[...] = jnp.zeros(xm_ref.shape, F32)
            hff_ref[...] = jnp.zeros(hff_ref.shape, BF16)

        for k in range(FFN_HEAD_START):
            ff_up(k)
        xo_ref[...] = xm_ref[...]
        hn_refs = (hna_ref, hnb_ref)
        hna_ref[...] = _bf(_rmsnorm(x_ref[half:rows, :], g1_ref[...]))
    else:
        cbufb_ref[:, SUBLANES - (CONV_B - 1):SUBLANES, :] = conv0_ref[...]
        cbufc_ref[:, SUBLANES - (CONV_C - 1):SUBLANES, :] = sconv0_ref[...]

    row = lax.broadcasted_iota(jnp.int32, (CHUNK, CHUNK), 0)
    col = lax.broadcasted_iota(jnp.int32, (CHUNK, CHUNK), 1)
    if prompt:
        causal = col <= row
    else:
        causal = ((row >> 3) == (col >> 3)) & (col <= row)
    causal01 = _bf(jnp.where(causal, 1.0, 0.0))
    erow = lax.broadcasted_iota(jnp.int32, (LANES, C_WIDTH), 0)
    ecol = lax.broadcasted_iota(jnp.int32, (LANES, C_WIDTH), 1)
    expand01 = _bf(jnp.where((ecol >> 6) == erow, 1.0, 0.0))

    def expand(a):
        a1 = _bf(a)
        a2 = _bf(a - a1.astype(F32))
        return _dot(a1, expand01) + _dot(a2, expand01)

    a_neg = -jnp.exp(alog_ref[...])
    if prompt:
        upper01 = _bf(jnp.where(row <= col, 1.0, 0.0))
        arow = lax.broadcasted_iota(jnp.int32, (CHUNK, A_HEADS * CHUNK), 0)
        acol = lax.broadcasted_iota(jnp.int32, (CHUNK, A_HEADS * CHUNK), 1)
        wa_b = _bf(jnp.where((acol & (CHUNK - 1)) <= arow, wa_ref[...], 0.0))
        hist = {"b": tailb_ref[...], "c": tailc_ref[...]}
        mix_refs = (mixa_ref, mixb_ref)

    fills = []
    points_left = [0]
    ffn_q = []
    step_points_left = [FILL_POINTS * n_sub]

    def fill_cols(dst, src, k):
        if k * FILL_COLS < COL_DT:
            cs = slice(k * FILL_COLS, (k + 1) * FILL_COLS)
            dst[:, cs] = _dot(src[...], win_ref[:, cs])
        else:
            dtt = dttb_ref if dst is projb_ref else dtta_ref
            dtt[...] = _dot_nt(wdt_ref[...], src[...])

    def norm_next_step():
        hnb_ref[...] = _bf(_rmsnorm(xn_ref[...], g1_ref[...]))

    def norm_ffn(h):
        assert ups_left[0] == 0, "hff_ref is still being read by this step's up-projections"
        rs = slice(h * half, (h + 1) * half)
        hff_ref[rs, :] = _bf(_rmsnorm(xm_ref[rs, :], g2_ref[...]))

    def ff_down(kc, nc):
        ks = slice(kc * FF_DOWN_K, (kc + 1) * FF_DOWN_K)
        cs = slice(nc * FILL_COLS, (nc + 1) * FILL_COLS)
        xo_ref[:, cs] += _dot(f_ref[:, ks], w2_ref[ks, cs])

    def ff_final_norm():
        xo_ref[...] = _rmsnorm(xo_ref[...], gf_ref[...])

    if prompt:
        ffn_q.extend(functools.partial(ff_up, k) for k in range(FFN_HEAD_START, D_FF // FILL_COLS))
        for kc in range(D_FF // FF_DOWN_K):
            ffn_q.extend(functools.partial(ff_down, kc, nc) for nc in range(D_MODEL // FILL_COLS))
        ffn_tail = ffn_q[-FFN_TAIL:]
        del ffn_q[-FFN_TAIL:]

    def fill():
        for queue, left in ((fills, points_left), (ffn_q, step_points_left)):
            n = -(-len(queue) // max(left[0], 1))
            left[0] -= 1
            for _ in range(min(n, len(queue))):
                queue.pop(0)()

    def sub_block(c):
        sl = slice(c * CHUNK, (c + 1) * CHUNK)
        x = x_ref[sl, :]
        if prompt:
            if c % subs_per_half == 0:
                h = c // subs_per_half
                dst = projb_ref if h == 0 else proja_ref
                fills.extend(functools.partial(fill_cols, dst, hn_refs[h], k) for k in range(n_fill))
                if h == 0:
                    fills.append(norm_next_step)
            pref = proja_ref if c < subs_per_half else projb_ref
            lr = slice((c % subs_per_half) * CHUNK, (c % subs_per_half + 1) * CHUNK)
            P = lambda c0, w, pref=pref, lr=lr: pref[lr, c0:c0 + w]
        else:
            proj = in_proj(x)
            P = lambda c0, w, proj=proj: proj[:, c0:c0 + w]

        if prompt:
            dtt = dtta_ref if c < subs_per_half else dttb_ref
            dt_t = _softplus(dtt[:, lr] + dtb_ref[...])
            a = dt_t * a_neg
            a1 = _bf(a).astype(F32)
            a2 = _bf(a - a1).astype(F32)
            r = _dot(_bf(jnp.concatenate([a1, a2, a - a1 - a2, jnp.zeros_like(a)], axis=0)), upper01)
            cum_t = r[0:8] + r[8:16] + r[16:24]
            expcum_t = jnp.exp(cum_t)
            wst_t = jnp.exp(cum_t[:, CHUNK - 1:CHUNK] - cum_t) * dt_t
            pack = jnp.concatenate([wst_t, expcum_t, cum_t,
                                    jnp.zeros((CHUNK - 3 * SSM_HEADS, CHUNK), F32)], axis=0)
            cols = pack.T

            def per_channel(lane0):
                tiles = []
                for p in range(SSM_HEADS // 2):
                    even = jnp.broadcast_to(cols[:, lane0 + 2 * p:lane0 + 2 * p + 1], (CHUNK, LANES))
                    odd = jnp.broadcast_to(cols[:, lane0 + 2 * p + 1:lane0 + 2 * p + 2], (CHUNK, LANES))
                    tiles.append(jnp.where(col < SSM_HEAD_DIM, even, odd))
                return jnp.concatenate(tiles, axis=1)

            w_state_x, expcum_x = per_channel(0), per_channel(SSM_HEADS)
            cum, cum_lane0 = cols, 2 * SSM_HEADS
        else:
            cum_lane0 = 0
            dt = _softplus(P(COL_DT, LANES) + dtb_ref[...])
            cum = _sel_left(causal01, dt * a_neg)
            cum3 = cum.reshape(seqs_per_sub, DEC_SEQ, LANES)
            tot = jnp.broadcast_to(cum3[:, DEC_SEQ - 1:DEC_SEQ, :], cum3.shape).reshape(CHUNK, LANES)
            expcum = jnp.exp(cum)
            w_state = jnp.exp(tot - cum) * dt
            ex = expand(jnp.concatenate([w_state, expcum], axis=0))
            w_state_x, expcum_x = ex[0:CHUNK], ex[CHUNK:]
            cum_t = cum.T
            dt_t = dt.T

        u = _gelu(P(COL_U, A_WIDTH))
        v = _gelu(P(COL_V, A_WIDTH))
        if prompt:
            pairs = []
            for p in range(A_HEADS // 2):
                rhs = _block_diag_pair(v[:, p * LANES:(p + 1) * LANES])
                pairs.append(_dot(wa_b[:, p * 2 * CHUNK:(p + 1) * 2 * CHUNK], rhs))
            s = jnp.concatenate(pairs, axis=1) + ba_ref[...]
            if c == n_sub - 1:
                hist["v"] = v
        else:
            s0 = c * seqs_per_sub
            v3 = v.reshape(seqs_per_sub, DEC_SEQ, A_WIDTH)
            trow = lax.broadcasted_iota(jnp.int32, (DEC_SEQ, A_WIDTH), 0)
            s3 = jnp.zeros((seqs_per_sub, DEC_SEQ, A_WIDTH), F32)
            for jj in range(DEC_SEQ):
                coef = jnp.where(trow >= jj, wa_ref[jj], 0.0)
                s3 = s3 + coef[None] * v3[:, jj:jj + 1, :]
            s = (s3 + ba_ref[0:DEC_SEQ, :][None]).reshape(CHUNK, A_WIDTH)
            vrow_ref[s0:s0 + seqs_per_sub] = v3
        ya = u * s
        fill()
        yield

        cg_in = P(COL_CG, B_WIDTH) * P(COL_HB, B_WIDTH)
        xbc_in = P(COL_XBC, SSM_CONV_DIM)
        if prompt:
            conv = _causal_taps_rows(cg_in, hist["b"], cw_ref, CONV_B)
            xbc = _causal_taps_rows(xbc_in, hist["c"], scw_ref, CONV_C)
            hist["b"] = cg_in[CHUNK - SUBLANES:, :]
            hist["c"] = xbc_in[CHUNK - SUBLANES:, :]
        else:
            cbufb_ref[s0:s0 + seqs_per_sub, SUBLANES:, :] = cg_in.reshape(seqs_per_sub, DEC_SEQ, B_WIDTH)
            cbufc_ref[s0:s0 + seqs_per_sub, SUBLANES:, :] = xbc_in.reshape(seqs_per_sub, DEC_SEQ, SSM_CONV_DIM)
            conv = _causal_taps_seqs(cbufb_ref, s0, seqs_per_sub, cw_ref, CONV_B, B_WIDTH)
            xbc = _causal_taps_seqs(cbufc_ref, s0, seqs_per_sub, scw_ref, CONV_C, SSM_CONV_DIM)
        yb = P(COL_BG, B_WIDTH) * conv
        fill()
        yield

        xbc = _silu(xbc + scb_ref[...])
        xs = xbc[:, 0:C_WIDTH]
        bm = xbc[:, C_WIDTH:C_WIDTH + SSM_GROUPS * D_STATE]
        cm = xbc[:, C_WIDTH + SSM_GROUPS * D_STATE:]
        xw = xs * w_state_x
        bm_b = _bf(bm)
        cm_b = _bf(cm)
        fill()
        yield

        zero_blk = jnp.zeros((CHUNK, D_STATE), BF16)
        b_diag = jnp.concatenate([jnp.concatenate([bm_b[:, 0:D_STATE], zero_blk], axis=1),
                                  jnp.concatenate([zero_blk, bm_b[:, D_STATE:]], axis=1)], axis=0)
        cb_all = _dot_nt(cm_b, b_diag)
        y_pairs = []
        for g in range(SSM_GROUPS):
            gs = slice(g * D_STATE, (g + 1) * D_STATE)
            cb = cb_all[:, gs]
            wts = []
            for kk in range(HEADS_PER_GROUP):
                k = g * HEADS_PER_GROUP + kk
                seg = cum[:, cum_lane0 + k:cum_lane0 + k + 1] - cum_t[k:k + 1, :]
                decay = jnp.exp(jnp.where(causal, seg, -jnp.inf))
                wts.append(_bf(cb * decay * dt_t[k:k + 1, :]))
            for p in range(HEADS_PER_GROUP // 2):
                k0 = g * HEADS_PER_GROUP + 2 * p
                lhs = jnp.concatenate([wts[2 * p], wts[2 * p + 1]], axis=1)
                rhs = _block_diag_pair(xs[:, k0 * SSM_HEAD_DIM:(k0 + 2) * SSM_HEAD_DIM])
                y_pairs.append(_dot(lhs, rhs))
            fill()
            yield
        y = jnp.concatenate(y_pairs, axis=1)

        if prompt:
            xw_b = _bf(xw)
            y_off = []
            for g in range(SSM_GROUPS):
                gs = slice(g * D_STATE, (g + 1) * D_STATE)
                gw = slice(g * GROUP_W, (g + 1) * GROUP_W)
                st_g = st_ref[:, gw]
                y_off.append(_dot(cm_b[:, gs], _bf(st_g)))
                st_ref[:, gw] = expcum_x[CHUNK - 1:CHUNK, gw] * st_g + _dot(_bf(bm[:, gs].T), xw_b[:, gw])
            y = y + jnp.concatenate(y_off, axis=1) * expcum_x
        else:
            exptot = jnp.exp(tot)
            xw_t = [_bf(xw[:, g * GROUP_W:(g + 1) * GROUP_W].T) for g in range(SSM_GROUPS)]
            y_off = []
            for i in range(seqs_per_sub):
                rs = slice(i * DEC_SEQ, (i + 1) * DEC_SEQ)
                own_rows = (row >> 3) == i
                parts = []
                for g in range(SSM_GROUPS):
                    gs = slice(g * D_STATE, (g + 1) * D_STATE)
                    gw = slice(g * GROUP_W, (g + 1) * GROUP_W)
                    h0 = ssm0_ref[s0 + i, gw, :]
                    parts.append(_dot_nt(_bf(cm[rs, gs]), _bf(h0)))
                    upd = _dot(xw_t[g], _bf(jnp.where(own_rows, bm[:, gs], 0.0)))
                    for kk in range(HEADS_PER_GROUP):
                        k = g * HEADS_PER_GROUP + kk
                        hs = slice(kk * SSM_HEAD_DIM, (kk + 1) * SSM_HEAD_DIM)
                        dec = jnp.broadcast_to(exptot[i * DEC_SEQ:i * DEC_SEQ + 1, k:k + 1],
                                               (SSM_HEAD_DIM, D_STATE))
                        ssmo_ref[s0 + i, k * SSM_HEAD_DIM:(k + 1) * SSM_HEAD_DIM, :] = (
                            dec * h0[hs, :] + upd[hs, :])
                y_off.append(jnp.concatenate(parts, axis=1))
            y = y + jnp.concatenate(y_off, axis=0) * expcum_x

        y = y + dsk_ref[...] * xs
        yc = _rmsnorm(y * _silu(P(COL_Z, C_WIDTH)), snorm_ref[...])
        fill()
        yield
        mix = jnp.concatenate([_bf(ya), _bf(yb), _bf(yc)], axis=1)
        if prompt:
            mix_refs[c // subs_per_half][lr, :] = mix
        else:
            xo_ref[sl, :] = x + _dot(mix, wout_ref[...])

    def out_cols(h, k):
        rs = slice(h * half, (h + 1) * half)
        cs = slice(k * FILL_COLS, (k + 1) * FILL_COLS)
        xm_ref[rs, cs] = x_ref[rs, cs] + _dot(mix_refs[h][...], wout_ref[:, cs])

    group = subs_per_half if prompt else 1
    for c0 in range(0, n_sub, group):
        if prompt:
            points_left[0] = FILL_POINTS * group
        gens = [sub_block(c) for c in range(c0, c0 + group)]
        while gens:
            gens = [g for g in gens if _advance(g)]
        while fills:
            fills.pop(0)()
        if prompt:
            h = c0 // group
            outs = [functools.partial(out_cols, h, k) for k in range(D_MODEL // FILL_COLS)]
            outs.append(functools.partial(norm_ffn, h))
            if c0 + group < n_sub:
                fills.extend(outs)
            else:
                while ffn_q:
                    ffn_q.pop(0)()
                for task in outs:
                    task()
                    if ffn_tail:
                        ffn_tail.pop(0)()
                while ffn_tail:
                    ffn_tail.pop(0)()
                if final:
                    ff_final_norm()

    if prompt:
        tailb_ref[...] = hist["b"]
        tailc_ref[...] = hist["c"]

        @pl.when(live)
        def _():
            vrow_ref[...] = hist["v"]
            convo_ref[...] = hist["b"][SUBLANES - (CONV_B - 1):, :]
            sconvo_ref[...] = hist["c"][SUBLANES - (CONV_C - 1):, :]

        @pl.when(live & (j == steps - 1))
        def _():
            ssmo_ref[...] = st_ref[...].T
    else:
        convo_ref[...] = cbufb_ref[:, 2 * SUBLANES - (CONV_B - 1):, :]
        sconvo_ref[...] = cbufc_ref[:, 2 * SUBLANES - (CONV_C - 1):, :]


def _ff_kernel(x_ref, g2_ref, w1_ref, w2_ref, gf_ref, o_ref, h_ref, *, final):
    c = pl.program_id(0)

    @pl.when(c == 0)
    def _():
        x = x_ref[...]
        h_ref[...] = _bf(_rmsnorm(x, g2_ref[...]))
        o_ref[...] = x

    f = jnp.maximum(_dot(h_ref[...], w1_ref[...]), 0.0)
    o_ref[...] += _dot(_bf(f * f), w2_ref[...])
    if final:
        @pl.when(c == pl.num_programs(0) - 1)
        def _():
            o_ref[...] = _rmsnorm(o_ref[...], gf_ref[...])


def _layer_spec(shape, layer, n_grid):
    zeros = (0,) * len(shape)
    if n_grid == 1:
        index_map = lambda i: (layer,) + zeros
    else:
        index_map = lambda i, j: (layer,) + zeros
    return pl.BlockSpec((None,) + tuple(shape), index_map, pipeline_mode=pl.Buffered(1))


def _mix_call(prompt, layer, x2d, params, states=None, carry=None, seq=None, final=False):
    n_rows = x2d.shape[0]
    steps = None
    if prompt:
        rows = PROMPT_ROWS
        n_batch = n_rows // seq
        steps = seq // rows
        n_blocks = n_rows // rows
        grid = (n_blocks + 1,)
        xmap = lambda t: (jnp.minimum(t, n_blocks - 1), 0)
        omap = lambda t: (jnp.maximum(t - 1, 0), 0)
        smap = lambda t: (layer, jnp.minimum(t // steps, n_batch - 1), 0, 0)
        sblk = (None, None)
        state_rows = (CHUNK, CONV_B - 1, CONV_C - 1, C_WIDTH)
    else:
        rows = SAMPLE_SEQS * DEC_SEQ
        n_batch = n_rows // DEC_SEQ
        grid = (n_batch // SAMPLE_SEQS,)
        xmap = omap = lambda i: (i, 0)
        smap = lambda i: (layer, i, 0, 0)
        sblk = (None, SAMPLE_SEQS)
        state_rows = (DEC_SEQ, CONV_B - 1, CONV_C - 1, C_WIDTH)
    state_cols = (A_WIDTH, B_WIDTH, SSM_CONV_DIM, D_STATE)

    par_specs = [_layer_spec(p.shape[1:], layer, 1) if p.ndim > 2 else
                 pl.BlockSpec(p.shape, lambda i: (0, 0), pipeline_mode=pl.Buffered(1)) for p in params]
    in_specs = [pl.BlockSpec((rows, D_MODEL), xmap)] + par_specs
    args = [x2d] + list(params)
    if prompt:
        half = rows // 2
        last_half = n_rows // half - 1
        nmap = lambda t: (jnp.minimum(2 * t + 2, last_half), 0)
        in_specs.insert(1, pl.BlockSpec((half, D_MODEL), nmap))
        args.insert(1, x2d)
        scratch = [
            pltpu.VMEM((SUBLANES, B_WIDTH), F32),
            pltpu.VMEM((SUBLANES, SSM_CONV_DIM), F32),
            pltpu.VMEM((D_STATE, C_WIDTH), F32),
            pltpu.VMEM((half, COL_DT), F32),
            pltpu.VMEM((half, COL_DT), F32),
            pltpu.VMEM((SSM_HEADS, half), F32),
            pltpu.VMEM((SSM_HEADS, half), F32),
            pltpu.VMEM((half, D_MODEL), BF16),
            pltpu.VMEM((half, D_MODEL), BF16),
            pltpu.VMEM((half, D_MODEL), BF16),
            pltpu.VMEM((half, D_MODEL), BF16),
            pltpu.VMEM((rows, D_MODEL), F32),
            pltpu.VMEM((rows, D_MODEL), BF16),
            pltpu.VMEM((rows, D_FF), BF16),
        ]
        sem = ("arbitrary",)
    else:
        in_specs += [pl.BlockSpec(sblk + (r, w), smap)
                     for r, w in zip(state_rows[1:], state_cols[1:])]
        args += list(states)
        scratch = [
            pltpu.VMEM((SAMPLE_SEQS, 2 * SUBLANES, B_WIDTH), F32),
            pltpu.VMEM((SAMPLE_SEQS, 2 * SUBLANES, SSM_CONV_DIM), F32),
        ]
        sem = ("parallel",)

    out_shape = [jax.ShapeDtypeStruct((n_rows, D_MODEL), F32)]
    out_specs = [pl.BlockSpec((rows, D_MODEL), omap)]
    for r, w in zip(state_rows, state_cols):
        out_shape.append(jax.ShapeDtypeStruct((DEPTH, n_batch, r, w), F32))
        out_specs.append(pl.BlockSpec(sblk + (r, w), smap))
    n_in = len(args)
    aliases = {}
    if carry is not None:
        in_specs += [pl.BlockSpec(memory_space=pl.ANY)] * len(carry)
        args += list(carry)
        aliases = {n_in + k: 1 + k for k in range(len(carry))}

    return pl.pallas_call(
        functools.partial(_mix_kernel, prompt=prompt, rows=rows, n_in=n_in, n_carry=len(aliases),
                          steps=steps, final=final),
        grid=grid,
        in_specs=in_specs,
        out_specs=out_specs,
        out_shape=out_shape,
        scratch_shapes=scratch,
        input_output_aliases=aliases,
        compiler_params=pltpu.CompilerParams(dimension_semantics=sem, vmem_limit_bytes=VMEM_LIMIT),
        name="mix_prompt" if prompt else "mix_sample",
    )(*args)


def _ff_call(layer, x2d, g2, w1, w2, gf, final):
    n_rows = x2d.shape[0]
    assert n_rows == FF_ROWS
    return pl.pallas_call(
        functools.partial(_ff_kernel, final=final),
        grid=(D_FF // FF_COLS,),
        in_specs=[
            pl.BlockSpec((FF_ROWS, D_MODEL), lambda c: (0, 0), pipeline_mode=pl.Buffered(1)),
            _layer_spec(g2.shape[1:], layer, 1),
            pl.BlockSpec((None, D_MODEL, FF_COLS), lambda c: (layer, 0, c)),
            pl.BlockSpec((None, FF_COLS, D_MODEL), lambda c: (layer, c, 0)),
            pl.BlockSpec(gf.shape, lambda c: (0, 0), pipeline_mode=pl.Buffered(1)),
        ],
        out_specs=pl.BlockSpec((FF_ROWS, D_MODEL), lambda c: (0, 0)),
        out_shape=jax.ShapeDtypeStruct((n_rows, D_MODEL), F32),
        scratch_shapes=[pltpu.VMEM((FF_ROWS, D_MODEL), BF16)],
        compiler_params=pltpu.CompilerParams(dimension_semantics=("arbitrary",),
                                             vmem_limit_bytes=VMEM_LIMIT),
        name="ffn",
    )(x2d, g2, w1, w2, gf)


def kernel(x_prompt, x_sample, state_conv, state_ssm_conv, state_ssm, norm1, w_in, w_s, b_s, conv_w,
           ssm_conv_w, ssm_conv_b, dt_bias, a_log, d_skip, ssm_norm, w_out, norm2, w_ff1, w_ff2,
           final_norm):
    bp, seq, _ = x_prompt.shape
    bs, dseq, _ = x_sample.shape
    assert seq % PROMPT_ROWS == 0 and dseq == DEC_SEQ and bs % SAMPLE_SEQS == 0

    wdt_b = _bf(jnp.pad(w_in[:, :, COL_DT:], ((0, 0), (0, 0), (0, D_IN_PAD - D_IN))))
    wdt_t = _bf(jnp.transpose(w_in[:, :, COL_DT:], (0, 2, 1)))
    win_all = _bf(jnp.pad(w_in, ((0, 0), (0, 0), (0, D_IN_PAD - D_IN))))
    wout_all, w1_all, w2_all = _bf(w_out), _bf(w_ff1), _bf(w_ff2)
    g1 = norm1.reshape(DEPTH, 1, D_MODEL)
    g2 = norm2.reshape(DEPTH, 1, D_MODEL)
    gf = final_norm.reshape(1, D_MODEL)
    wa_prompt = jnp.transpose(w_s, (0, 2, 1, 3)).reshape(DEPTH, CHUNK, A_HEADS * CHUNK)
    wa_sample = jnp.repeat(jnp.transpose(w_s[:, :, :DEC_SEQ, :DEC_SEQ], (0, 3, 2, 1)),
                           A_WIDTH // A_HEADS, axis=-1)
    ba = jnp.repeat(jnp.transpose(b_s, (0, 2, 1)), A_WIDTH // A_HEADS, axis=-1)
    scb = ssm_conv_b.reshape(DEPTH, 1, SSM_CONV_DIM)
    pad_heads = lambda p: jnp.pad(p, ((0, 0), (0, LANES - SSM_HEADS))).reshape(DEPTH, 1, LANES)
    dtb = pad_heads(dt_bias)
    alog = pad_heads(a_log)
    dsk = jnp.repeat(d_skip, SSM_HEAD_DIM, axis=-1).reshape(DEPTH, 1, C_WIDTH)
    snorm = ssm_norm.reshape(DEPTH, 1, C_WIDTH)
    over_time = lambda p: jnp.broadcast_to(p[:, :, None], (DEPTH, SSM_HEADS, CHUNK))
    mid_prompt = (wa_prompt, ba, conv_w, ssm_conv_w, scb, over_time(dt_bias), over_time(a_log), dsk, snorm)
    mid_sample = (wa_sample, ba, conv_w, ssm_conv_w, scb, dtb, alog, dsk, snorm)

    xp = x_prompt.reshape(bp * seq, D_MODEL)
    xs = x_sample.reshape(bs * dseq, D_MODEL)
    ssm_in = state_ssm.reshape(DEPTH, bs, C_WIDTH, D_STATE)
    st_p = st_s = None
    for l in range(DEPTH):
        final = l == DEPTH - 1
        win_b, wout_b, w1_b, w2_b = win_all, wout_all, w1_all, w2_all
        xp, *st_p = _mix_call(True, l, xp, (g1, win_b, wdt_t) + mid_prompt + (wout_b, g2, w1_b, w2_b, gf),
                              carry=st_p, seq=seq, final=final)
        xs, *st_s = _mix_call(False, l, xs, (g1, win_b, wdt_b) + mid_sample + (wout_b,),
                              (state_conv, state_ssm_conv, ssm_in), carry=st_s)
        xs = _ff_call(l, xs, g2, w1_b, w2_b, gf, final)

    y_prompt = xp.reshape(bp, seq, D_MODEL)
    y_sample = xs.reshape(bs, dseq, D_MODEL)
    ssm_shape = lambda b: (DEPTH, b, SSM_HEADS, SSM_HEAD_DIM, D_STATE)
    return (y_prompt, y_sample,
            st_p[0], st_p[1], st_p[2], st_p[3].reshape(ssm_shape(bp)),
            st_s[0], st_s[1], st_s[2], st_s[3].reshape(ssm_shape(bs)))
```

```python
import functools

import jax
import jax.numpy as jnp
from jax import lax
from jax.experimental import pallas as pl
from jax.experimental.pallas import tpu as pltpu

D_MODEL = 1024
DEPTH = 4
A_WIDTH = 256
A_HEADS = 4
B_WIDTH = 256
C_WIDTH = 512
CONV_B = 3
CONV_C = 4
SSM_HEADS = 8
SSM_HEAD_DIM = 64
SSM_GROUPS = 2
HEADS_PER_GROUP = SSM_HEADS // SSM_GROUPS
D_STATE = 128
SSM_CONV_DIM = 1024
CHUNK = 128
D_FF = 4096
EPS = 1e-5
DEC_SEQ = 8

COL_U, COL_V, COL_BG, COL_CG, COL_HB, COL_Z, COL_XBC, COL_DT = 0, 256, 512, 768, 1024, 1280, 1792, 2816
D_IN = 2824
LANES = 128
SUBLANES = 8
D_IN_PAD = 2944
GROUP_W = C_WIDTH // SSM_GROUPS

PROMPT_ROWS = 512
SAMPLE_SEQS = 16
FILL_COLS = 256
FILL_POINTS = 6
FFN_HEAD_START = 2
FF_DOWN_K = 1024
FFN_TAIL = 6
FF_ROWS = 1024
FF_COLS = 1024
VMEM_LIMIT = 56 * 1024 * 1024

F32 = jnp.float32
BF16 = jnp.bfloat16


def _bf(x):
    return x.astype(BF16)


def _dot(a, b):
    return jnp.dot(a, b, preferred_element_type=F32)


def _dot_nt(a, b):
    return lax.dot_general(a, b, (((1,), (1,)), ((), ())), preferred_element_type=F32)


def _split3(a):
    a1 = _bf(a)
    r1 = a - a1.astype(F32)
    a2 = _bf(r1)
    r2 = r1 - a2.astype(F32)
    return a1, a2, _bf(r2)


def _sel_left(m01, a):
    a1, a2, a3 = _split3(a)
    return _dot(m01, a1) + _dot(m01, a2) + _dot(m01, a3)


def _rmsnorm(x, g):
    ms = jnp.mean(x * x, axis=-1, keepdims=True)
    return (x * lax.rsqrt(ms + EPS)) * g


def _gelu(x):
    return 0.5 * x * (1.0 + lax.erf(x * (0.5 ** 0.5)))


def _silu(x):
    return x * jax.nn.sigmoid(x)


def _softplus(x):
    return jnp.maximum(x, 0.0) + jnp.log1p(jnp.exp(-jnp.abs(x)))


def _block_diag_pair(t):
    lane = lax.broadcasted_iota(jnp.int32, t.shape, 1)
    lo = jnp.where(lane < SSM_HEAD_DIM, t, 0.0)
    hi = jnp.where(lane >= SSM_HEAD_DIM, t, 0.0)
    return _bf(jnp.concatenate([lo, hi], axis=0))


def _causal_taps_rows(xin, prev8, w_ref, n_taps):
    row8 = lax.broadcasted_iota(jnp.int32, prev8.shape, 0)
    out = None
    for k in range(n_taps):
        back = n_taps - 1 - k
        if back == 0:
            sh = xin
        else:
            r = pltpu.roll(xin, back, axis=0)
            first = jnp.where(row8 >= back, r[0:SUBLANES], pltpu.roll(prev8, back, axis=0))
            sh = jnp.concatenate([first, r[SUBLANES:]], axis=0)
        term = sh * w_ref[k:k + 1, :]
        out = term if out is None else out + term
    return out


def _causal_taps_seqs(buf_ref, s0, n_seq, w_ref, n_taps, width):
    out = None
    for k in range(n_taps):
        back = n_taps - 1 - k
        sh = buf_ref[s0:s0 + n_seq, SUBLANES - back:2 * SUBLANES - back, :].reshape(n_seq * DEC_SEQ, width)
        term = sh * w_ref[k:k + 1, :]
        out = term if out is None else out + term
    return out


def _advance(gen):
    try:
        next(gen)
        return True
    except StopIteration:
        return False


def _mix_kernel(*refs, prompt, rows, n_in, n_carry, steps=None, final=False):
    refs = refs[:n_in] + refs[n_in + n_carry:]
    if prompt:
        (x_ref, xn_ref, g1_ref, win_ref, wdt_ref, wa_ref, ba_ref, cw_ref, scw_ref, scb_ref, dtb_ref, alog_ref,
         dsk_ref, snorm_ref, wout_ref, g2_ref, w1_ref, w2_ref, gf_ref,
         xo_ref, vrow_ref, convo_ref, sconvo_ref, ssmo_ref,
         tailb_ref, tailc_ref, st_ref, proja_ref, projb_ref, dtta_ref, dttb_ref, hna_ref, hnb_ref,
         mixa_ref, mixb_ref,
         xm_ref, hff_ref, f_ref) = refs
    else:
        (x_ref, g1_ref, win_ref, wdt_ref, wa_ref, ba_ref, cw_ref, scw_ref, scb_ref, dtb_ref, alog_ref,
         dsk_ref, snorm_ref, wout_ref, conv0_ref, sconv0_ref, ssm0_ref,
         xo_ref, vrow_ref, convo_ref, sconvo_ref, ssmo_ref,
         cbufb_ref, cbufc_ref) = refs
    n_sub = rows // CHUNK
    seqs_per_sub = CHUNK // DEC_SEQ

    def in_proj(x_rows):
        h = _bf(_rmsnorm(x_rows, g1_ref[...]))
        return jnp.concatenate([_dot(h, win_ref[:, 0:COL_DT]), _dot(h, wdt_ref[...])], axis=1)

    ups_left = [D_FF // FILL_COLS]

    def ff_up(k):
        ups_left[0] -= 1
        cs = slice(k * FILL_COLS, (k + 1) * FILL_COLS)
        f = jnp.maximum(_dot(hff_ref[...], w1_ref[:, cs]), 0.0)
        f_ref[:, cs] = _bf(f * f)

    if prompt:
        half = rows // 2
        subs_per_half = n_sub // 2
        n_fill = -(-D_IN_PAD // FILL_COLS)
        t = pl.program_id(0)
        j = t % steps
        live = t < pl.num_programs(0) - 1

        @pl.when(j == 0)
        def _():
            tailb_ref[...] = jnp.zeros(tailb_ref.shape, F32)
            tailc_ref[...] = jnp.zeros(tailc_ref.shape, F32)
            st_ref[...] = jnp.zeros(st_ref.shape, F32)

        @pl.when(t == 0)
        def _():
            h0 = _bf(_rmsnorm(x_ref[0:half, :], g1_ref[...]))
            proja_ref[...] = _dot(h0, win_ref[:, 0:COL_DT])
            dtta_ref[...] = _dot_nt(wdt_ref[...], h0)
            xm_ref[...] = jnp.zeros(xm_ref.shape, F32)
            hff_ref[...] = jnp.zeros(hff_ref.shape, BF16)

        for k in range(FFN_HEAD_START):
            ff_up(k)
        xo_ref[...] = xm_ref[...]
        hn_refs = (hna_ref, hnb_ref)
        hna_ref[...] = _bf(_rmsnorm(x_ref[half:rows, :], g1_ref[...]))
    else:
        cbufb_ref[:, SUBLANES - (CONV_B - 1):SUBLANES, :] = conv0_ref[...]
        cbufc_ref[:, SUBLANES - (CONV_C - 1):SUBLANES, :] = sconv0_ref[...]

    row = lax.broadcasted_iota(jnp.int32, (CHUNK, CHUNK), 0)
    col = lax.broadcasted_iota(jnp.int32, (CHUNK, CHUNK), 1)
    if prompt:
        causal = col <= row
    else:
        causal = ((row >> 3) == (col >> 3)) & (col <= row)
    causal01 = _bf(jnp.where(causal, 1.0, 0.0))
    erow = lax.broadcasted_iota(jnp.int32, (LANES, C_WIDTH), 0)
    ecol = lax.broadcasted_iota(jnp.int32, (LANES, C_WIDTH), 1)
    expand01 = _bf(jnp.where((ecol >> 6) == erow, 1.0, 0.0))

    def expand(a):
        a1 = _bf(a)
        a2 = _bf(a - a1.astype(F32))
        return _dot(a1, expand01) + _dot(a2, expand01)

    a_neg = -jnp.exp(alog_ref[...])
    if prompt:
        upper01 = _bf(jnp.where(row <= col, 1.0, 0.0))
        arow = lax.broadcasted_iota(jnp.int32, (CHUNK, A_HEADS * CHUNK), 0)
        acol = lax.broadcasted_iota(jnp.int32, (CHUNK, A_HEADS * CHUNK), 1)
        wa_b = _bf(jnp.where((acol & (CHUNK - 1)) <= arow, wa_ref[...], 0.0))
        hist = {"b": tailb_ref[...], "c": tailc_ref[...]}
        mix_refs = (mixa_ref, mixb_ref)

    fills = []
    points_left = [0]
    ffn_q = []
    step_points_left = [FILL_POINTS * n_sub]

    def fill_cols(dst, src, k):
        if k * FILL_COLS < COL_DT:
            cs = slice(k * FILL_COLS, (k + 1) * FILL_COLS)
            dst[:, cs] = _dot(src[...], win_ref[:, cs])
        else:
            dtt = dttb_ref if dst is projb_ref else dtta_ref
            dtt[...] = _dot_nt(wdt_ref[...], src[...])

    def norm_next_step():
        hnb_ref[...] = _bf(_rmsnorm(xn_ref[...], g1_ref[...]))

    def norm_ffn(h):
        assert ups_left[0] == 0, "hff_ref is still being read by this step's up-projections"
        rs = slice(h * half, (h + 1) * half)
        hff_ref[rs, :] = _bf(_rmsnorm(xm_ref[rs, :], g2_ref[...]))

    def ff_down(kc, nc):
        ks = slice(kc * FF_DOWN_K, (kc + 1) * FF_DOWN_K)
        cs = slice(nc * FILL_COLS, (nc + 1) * FILL_COLS)
        xo_ref[:, cs] += _dot(f_ref[:, ks], w2_ref[ks, cs])

    def ff_final_norm():
        xo_ref[...] = _rmsnorm(xo_ref[...], gf_ref[...])

    if prompt:
        ffn_q.extend(functools.partial(ff_up, k) for k in range(FFN_HEAD_START, D_FF // FILL_COLS))
        for kc in range(D_FF // FF_DOWN_K):
            ffn_q.extend(functools.partial(ff_down, kc, nc) for nc in range(D_MODEL // FILL_COLS))
        ffn_tail = ffn_q[-FFN_TAIL:]
        del ffn_q[-FFN_TAIL:]

    def fill():
        for queue, left in ((fills, points_left), (ffn_q, step_points_left)):
            n = -(-len(queue) // max(left[0], 1))
            left[0] -= 1
            for _ in range(min(n, len(queue))):
                queue.pop(0)()

    def sub_block(c):
        sl = slice(c * CHUNK, (c + 1) * CHUNK)
        x = x_ref[sl, :]
        if prompt:
            if c % subs_per_half == 0:
                h = c // subs_per_half
                dst = projb_ref if h == 0 else proja_ref
                fills.extend(functools.partial(fill_cols, dst, hn_refs[h], k) for k in range(n_fill))
                if h == 0:
                    fills.append(norm_next_step)
            pref = proja_ref if c < subs_per_half else projb_ref
            lr = slice((c % subs_per_half) * CHUNK, (c % subs_per_half + 1) * CHUNK)
            P = lambda c0, w, pref=pref, lr=lr: pref[lr, c0:c0 + w]
        else:
            proj = in_proj(x)
            P = lambda c0, w, proj=proj: proj[:, c0:c0 + w]

        if prompt:
            dtt = dtta_ref if c < subs_per_half else dttb_ref
            dt_t = _softplus(dtt[:, lr] + dtb_ref[...])
            a = dt_t * a_neg
            a1 = _bf(a).astype(F32)
            a2 = _bf(a - a1).astype(F32)
            r = _dot(_bf(jnp.concatenate([a1, a2, a - a1 - a2, jnp.zeros_like(a)], axis=0)), upper01)
            cum_t = r[0:8] + r[8:16] + r[16:24]
            expcum_t = jnp.exp(cum_t)
            wst_t = jnp.exp(cum_t[:, CHUNK - 1:CHUNK] - cum_t) * dt_t
            pack = jnp.concatenate([wst_t, expcum_t, cum_t,
                                    jnp.zeros((CHUNK - 3 * SSM_HEADS, CHUNK), F32)], axis=0)
            cols = pack.T

            def per_channel(lane0):
                tiles = []
                for p in range(SSM_HEADS // 2):
                    even = jnp.broadcast_to(cols[:, lane0 + 2 * p:lane0 + 2 * p + 1], (CHUNK, LANES))
                    odd = jnp.broadcast_to(cols[:, lane0 + 2 * p + 1:lane0 + 2 * p + 2], (CHUNK, LANES))
                    tiles.append(jnp.where(col < SSM_HEAD_DIM, even, odd))
                return jnp.concatenate(tiles, axis=1)

            w_state_x, expcum_x = per_channel(0), per_channel(SSM_HEADS)
            cum, cum_lane0 = cols, 2 * SSM_HEADS
        else:
            cum_lane0 = 0
            dt = _softplus(P(COL_DT, LANES) + dtb_ref[...])
            cum = _sel_left(causal01, dt * a_neg)
            cum3 = cum.reshape(seqs_per_sub, DEC_SEQ, LANES)
            tot = jnp.broadcast_to(cum3[:, DEC_SEQ - 1:DEC_SEQ, :], cum3.shape).reshape(CHUNK, LANES)
            expcum = jnp.exp(cum)
            w_state = jnp.exp(tot - cum) * dt
            ex = expand(jnp.concatenate([w_state, expcum], axis=0))
            w_state_x, expcum_x = ex[0:CHUNK], ex[CHUNK:]
            cum_t = cum.T
            dt_t = dt.T

        u = _gelu(P(COL_U, A_WIDTH))
        v = _gelu(P(COL_V, A_WIDTH))
        if prompt:
            pairs = []
            for p in range(A_HEADS // 2):
                rhs = _block_diag_pair(v[:, p * LANES:(p + 1) * LANES])
                pairs.append(_dot(wa_b[:, p * 2 * CHUNK:(p + 1) * 2 * CHUNK], rhs))
            s = jnp.concatenate(pairs, axis=1) + ba_ref[...]
            if c == n_sub - 1:
                hist["v"] = v
        else:
            s0 = c * seqs_per_sub
            v3 = v.reshape(seqs_per_sub, DEC_SEQ, A_WIDTH)
            trow = lax.broadcasted_iota(jnp.int32, (DEC_SEQ, A_WIDTH), 0)
            s3 = jnp.zeros((seqs_per_sub, DEC_SEQ, A_WIDTH), F32)
            for jj in range(DEC_SEQ):
                coef = jnp.where(trow >= jj, wa_ref[jj], 0.0)
                s3 = s3 + coef[None] * v3[:, jj:jj + 1, :]
            s = (s3 + ba_ref[0:DEC_SEQ, :][None]).reshape(CHUNK, A_WIDTH)
            vrow_ref[s0:s0 + seqs_per_sub] = v3
        ya = u * s
        fill()
        yield

        cg_in = P(COL_CG, B_WIDTH) * P(COL_HB, B_WIDTH)
        xbc_in = P(COL_XBC, SSM_CONV_DIM)
        if prompt:
            conv = _causal_taps_rows(cg_in, hist["b"], cw_ref, CONV_B)
            xbc = _causal_taps_rows(xbc_in, hist["c"], scw_ref, CONV_C)
            hist["b"] = cg_in[CHUNK - SUBLANES:, :]
            hist["c"] = xbc_in[CHUNK - SUBLANES:, :]
        else:
            cbufb_ref[s0:s0 + seqs_per_sub, SUBLANES:, :] = cg_in.reshape(seqs_per_sub, DEC_SEQ, B_WIDTH)
            cbufc_ref[s0:s0 + seqs_per_sub, SUBLANES:, :] = xbc_in.reshape(seqs_per_sub, DEC_SEQ, SSM_CONV_DIM)
            conv = _causal_taps_seqs(cbufb_ref, s0, seqs_per_sub, cw_ref, CONV_B, B_WIDTH)
            xbc = _causal_taps_seqs(cbufc_ref, s0, seqs_per_sub, scw_ref, CONV_C, SSM_CONV_DIM)
        yb = P(COL_BG, B_WIDTH) * conv
        fill()
        yield

        xbc = _silu(xbc + scb_ref[...])
        xs = xbc[:, 0:C_WIDTH]
        bm = xbc[:, C_WIDTH:C_WIDTH + SSM_GROUPS * D_STATE]
        cm = xbc[:, C_WIDTH + SSM_GROUPS * D_STATE:]
        xw = xs * w_state_x
        bm_b = _bf(bm)
        cm_b = _bf(cm)
        fill()
        yield

        zero_blk = jnp.zeros((CHUNK, D_STATE), BF16)
        b_diag = jnp.concatenate([jnp.concatenate([bm_b[:, 0:D_STATE], zero_blk], axis=1),
                                  jnp.concatenate([zero_blk, bm_b[:, D_STATE:]], axis=1)], axis=0)
        cb_all = _dot_nt(cm_b, b_diag)
        y_pairs = []
        for g in range(SSM_GROUPS):
            gs = slice(g * D_STATE, (g + 1) * D_STATE)
            cb = cb_all[:, gs]
            wts = []
            for kk in range(HEADS_PER_GROUP):
                k = g * HEADS_PER_GROUP + kk
                seg = cum[:, cum_lane0 + k:cum_lane0 + k + 1] - cum_t[k:k + 1, :]
                decay = jnp.exp(jnp.where(causal, seg, -jnp.inf))
                wts.append(_bf(cb * decay * dt_t[k:k + 1, :]))
            for p in range(HEADS_PER_GROUP // 2):
                k0 = g * HEADS_PER_GROUP + 2 * p
                lhs = jnp.concatenate([wts[2 * p], wts[2 * p + 1]], axis=1)
                rhs = _block_diag_pair(xs[:, k0 * SSM_HEAD_DIM:(k0 + 2) * SSM_HEAD_DIM])
                y_pairs.append(_dot(lhs, rhs))
            fill()
            yield
        y = jnp.concatenate(y_pairs, axis=1)

        if prompt:
            xw_b = _bf(xw)
            y_off = []
            for g in range(SSM_GROUPS):
                gs = slice(g * D_STATE, (g + 1) * D_STATE)
                gw = slice(g * GROUP_W, (g + 1) * GROUP_W)
                st_g = st_ref[:, gw]
                y_off.append(_dot(cm_b[:, gs], _bf(st_g)))
                st_ref[:, gw] = expcum_x[CHUNK - 1:CHUNK, gw] * st_g + _dot(_bf(bm[:, gs].T), xw_b[:, gw])
            y = y + jnp.concatenate(y_off, axis=1) * expcum_x
        else:
            exptot = jnp.exp(tot)
            xw_t = [_bf(xw[:, g * GROUP_W:(g + 1) * GROUP_W].T) for g in range(SSM_GROUPS)]
            y_off = []
            for i in range(seqs_per_sub):
                rs = slice(i * DEC_SEQ, (i + 1) * DEC_SEQ)
                own_rows = (row >> 3) == i
                parts = []
                for g in range(SSM_GROUPS):
                    gs = slice(g * D_STATE, (g + 1) * D_STATE)
                    gw = slice(g * GROUP_W, (g + 1) * GROUP_W)
                    h0 = ssm0_ref[s0 + i, gw, :]
                    parts.append(_dot_nt(_bf(cm[rs, gs]), _bf(h0)))
                    upd = _dot(xw_t[g], _bf(jnp.where(own_rows, bm[:, gs], 0.0)))
                    for kk in range(HEADS_PER_GROUP):
                        k = g * HEADS_PER_GROUP + kk
                        hs = slice(kk * SSM_HEAD_DIM, (kk + 1) * SSM_HEAD_DIM)
                        dec = jnp.broadcast_to(exptot[i * DEC_SEQ:i * DEC_SEQ + 1, k:k + 1],
                                               (SSM_HEAD_DIM, D_STATE))
                        ssmo_ref[s0 + i, k * SSM_HEAD_DIM:(k + 1) * SSM_HEAD_DIM, :] = (
                            dec * h0[hs, :] + upd[hs, :])
                y_off.append(jnp.concatenate(parts, axis=1))
            y = y + jnp.concatenate(y_off, axis=0) * expcum_x

        y = y + dsk_ref[...] * xs
        yc = _rmsnorm(y * _silu(P(COL_Z, C_WIDTH)), snorm_ref[...])
        fill()
        yield
        mix = jnp.concatenate([_bf(ya), _bf(yb), _bf(yc)], axis=1)
        if prompt:
            mix_refs[c // subs_per_half][lr, :] = mix
        else:
            xo_ref[sl, :] = x + _dot(mix, wout_ref[...])

    def out_cols(h, k):
        rs = slice(h * half, (h + 1) * half)
        cs = slice(k * FILL_COLS, (k + 1) * FILL_COLS)
        xm_ref[rs, cs] = x_ref[rs, cs] + _dot(mix_refs[h][...], wout_ref[:, cs])

    group = subs_per_half if prompt else 1
    for c0 in range(0, n_sub, group):
        if prompt:
            points_left[0] = FILL_POINTS * group
        gens = [sub_block(c) for c in range(c0, c0 + group)]
        while gens:
            gens = [g for g in gens if _advance(g)]
        while fills:
            fills.pop(0)()
        if prompt:
            h = c0 // group
            outs = [functools.partial(out_cols, h, k) for k in range(D_MODEL // FILL_COLS)]
            outs.append(functools.partial(norm_ffn, h))
            if c0 + group < n_sub:
                fills.extend(outs)
            else:
                while ffn_q:
                    ffn_q.pop(0)()
                for task in outs:
                    task()
                    if ffn_tail:
                        ffn_tail.pop(0)()
                while ffn_tail:
                    ffn_tail.pop(0)()
                if final:
                    ff_final_norm()

    if prompt:
        tailb_ref[...] = hist["b"]
        tailc_ref[...] = hist["c"]

        @pl.when(live)
        def _():
            vrow_ref[...] = hist["v"]
            convo_ref[...] = hist["b"][SUBLANES - (CONV_B - 1):, :]
            sconvo_ref[...] = hist["c"][SUBLANES - (CONV_C - 1):, :]

        @pl.when(live & (j == steps - 1))
        def _():
            ssmo_ref[...] = st_ref[...].T
    else:
        convo_ref[...] = cbufb_ref[:, 2 * SUBLANES - (CONV_B - 1):, :]
        sconvo_ref[...] = cbufc_ref[:, 2 * SUBLANES - (CONV_C - 1):, :]


def _ff_kernel(x_ref, g2_ref, w1_ref, w2_ref, gf_ref, o_ref, h_ref, *, final):
    c = pl.program_id(0)

    @pl.when(c == 0)
    def _():
        x = x_ref[...]
        h_ref[...] = _bf(_rmsnorm(x, g2_ref[...]))
        o_ref[...] = x

    f = jnp.maximum(_dot(h_ref[...], w1_ref[...]), 0.0)
    o_ref[...] += _dot(_bf(f * f), w2_ref[...])
    if final:
        @pl.when(c == pl.num_programs(0) - 1)
        def _():
            o_ref[...] = _rmsnorm(o_ref[...], gf_ref[...])


def _layer_spec(shape, layer, n_grid):
    zeros = (0,) * len(shape)
    if n_grid == 1:
        index_map = lambda i: (layer,) + zeros
    else:
        index_map = lambda i, j: (layer,) + zeros
    return pl.BlockSpec((None,) + tuple(shape), index_map, pipeline_mode=pl.Buffered(1))


def _mix_call(prompt, layer, x2d, params, states=None, carry=None, seq=None, final=False):
    n_rows = x2d.shape[0]
    steps = None
    if prompt:
        rows = PROMPT_ROWS
        n_batch = n_rows // seq
        steps = seq // rows
        n_blocks = n_rows // rows
        grid = (n_blocks + 1,)
        xmap = lambda t: (jnp.minimum(t, n_blocks - 1), 0)
        omap = lambda t: (jnp.maximum(t - 1, 0), 0)
        smap = lambda t: (layer, jnp.minimum(t // steps, n_batch - 1), 0, 0)
        sblk = (None, None)
        state_rows = (CHUNK, CONV_B - 1, CONV_C - 1, C_WIDTH)
    else:
        rows = SAMPLE_SEQS * DEC_SEQ
        n_batch = n_rows // DEC_SEQ
        grid = (n_batch // SAMPLE_SEQS,)
        xmap = omap = lambda i: (i, 0)
        smap = lambda i: (layer, i, 0, 0)
        sblk = (None, SAMPLE_SEQS)
        state_rows = (DEC_SEQ, CONV_B - 1, CONV_C - 1, C_WIDTH)
    state_cols = (A_WIDTH, B_WIDTH, SSM_CONV_DIM, D_STATE)

    par_specs = [_layer_spec((p.shape[1], min(p.shape[2], COL_DT)) if i == 1 else p.shape[1:], layer, 1)
                 if p.ndim > 2 else
                 pl.BlockSpec(p.shape, lambda i: (0, 0), pipeline_mode=pl.Buffered(1))
                 for i, p in enumerate(params)]
    in_specs = [pl.BlockSpec((rows, D_MODEL), xmap)] + par_specs
    args = [x2d] + list(params)
    if prompt:
        half = rows // 2
        last_half = n_rows // half - 1
        nmap = lambda t: (jnp.minimum(2 * t + 2, last_half), 0)
        in_specs.insert(1, pl.BlockSpec((half, D_MODEL), nmap))
        args.insert(1, x2d)
        scratch = [
            pltpu.VMEM((SUBLANES, B_WIDTH), F32),
            pltpu.VMEM((SUBLANES, SSM_CONV_DIM), F32),
            pltpu.VMEM((D_STATE, C_WIDTH), F32),
            pltpu.VMEM((half, COL_DT), F32),
            pltpu.VMEM((half, COL_DT), F32),
            pltpu.VMEM((SSM_HEADS, half), F32),
            pltpu.VMEM((SSM_HEADS, half), F32),
            pltpu.VMEM((half, D_MODEL), BF16),
            pltpu.VMEM((half, D_MODEL), BF16),
            pltpu.VMEM((half, D_MODEL), BF16),
            pltpu.VMEM((half, D_MODEL), BF16),
            pltpu.VMEM((rows, D_MODEL), F32),
            pltpu.VMEM((rows, D_MODEL), BF16),
            pltpu.VMEM((rows, D_FF), BF16),
        ]
        sem = ("arbitrary",)
    else:
        in_specs += [pl.BlockSpec(sblk + (r, w), smap)
                     for r, w in zip(state_rows[1:], state_cols[1:])]
        args += list(states)
        scratch = [
            pltpu.VMEM((SAMPLE_SEQS, 2 * SUBLANES, B_WIDTH), F32),
            pltpu.VMEM((SAMPLE_SEQS, 2 * SUBLANES, SSM_CONV_DIM), F32),
        ]
        sem = ("parallel",)

    out_shape = [jax.ShapeDtypeStruct((n_rows, D_MODEL), F32)]
    out_specs = [pl.BlockSpec((rows, D_MODEL), omap)]
    for r, w in zip(state_rows, state_cols):
        out_shape.append(jax.ShapeDtypeStruct((DEPTH, n_batch, r, w), F32))
        out_specs.append(pl.BlockSpec(sblk + (r, w), smap))
    n_in = len(args)
    aliases = {}
    if carry is not None:
        in_specs += [pl.BlockSpec(memory_space=pl.ANY)] * len(carry)
        args += list(carry)
        aliases = {n_in + k: 1 + k for k in range(len(carry))}

    return pl.pallas_call(
        functools.partial(_mix_kernel, prompt=prompt, rows=rows, n_in=n_in, n_carry=len(aliases),
                          steps=steps, final=final),
        grid=grid,
        in_specs=in_specs,
        out_specs=out_specs,
        out_shape=out_shape,
        scratch_shapes=scratch,
        input_output_aliases=aliases,
        compiler_params=pltpu.CompilerParams(dimension_semantics=sem, vmem_limit_bytes=VMEM_LIMIT),
        name="mix_prompt" if prompt else "mix_sample",
    )(*args)


def _ff_call(layer, x2d, g2, w1, w2, gf, final):
    n_rows = x2d.shape[0]
    assert n_rows == FF_ROWS
    return pl.pallas_call(
        functools.partial(_ff_kernel, final=final),
        grid=(D_FF // FF_COLS,),
        in_specs=[
            pl.BlockSpec((FF_ROWS, D_MODEL), lambda c: (0, 0), pipeline_mode=pl.Buffered(1)),
            _layer_spec(g2.shape[1:], layer, 1),
            pl.BlockSpec((None, D_MODEL, FF_COLS), lambda c: (layer, 0, c)),
            pl.BlockSpec((None, FF_COLS, D_MODEL), lambda c: (layer, c, 0)),
            pl.BlockSpec(gf.shape, lambda c: (0, 0), pipeline_mode=pl.Buffered(1)),
        ],
        out_specs=pl.BlockSpec((FF_ROWS, D_MODEL), lambda c: (0, 0)),
        out_shape=jax.ShapeDtypeStruct((n_rows, D_MODEL), F32),
        scratch_shapes=[pltpu.VMEM((FF_ROWS, D_MODEL), BF16)],
        compiler_params=pltpu.CompilerParams(dimension_semantics=("arbitrary",),
                                             vmem_limit_bytes=VMEM_LIMIT),
        name="ffn",
    )(x2d, g2, w1, w2, gf)


def kernel(x_prompt, x_sample, state_conv, state_ssm_conv, state_ssm, norm1, w_in, w_s, b_s, conv_w,
           ssm_conv_w, ssm_conv_b, dt_bias, a_log, d_skip, ssm_norm, w_out, norm2, w_ff1, w_ff2,
           final_norm):
    bp, seq, _ = x_prompt.shape
    bs, dseq, _ = x_sample.shape
    assert seq % PROMPT_ROWS == 0 and dseq == DEC_SEQ and bs % SAMPLE_SEQS == 0

    wdt_b = _bf(jnp.pad(w_in[:, :, COL_DT:], ((0, 0), (0, 0), (0, D_IN_PAD - D_IN))))
    wdt_t = _bf(jnp.transpose(w_in[:, :, COL_DT:], (0, 2, 1)))
    win_all, wout_all, w1_all, w2_all = _bf(w_in), _bf(w_out), _bf(w_ff1), _bf(w_ff2)
    g1 = norm1.reshape(DEPTH, 1, D_MODEL)
    g2 = norm2.reshape(DEPTH, 1, D_MODEL)
    gf = final_norm.reshape(1, D_MODEL)
    wa_prompt = jnp.transpose(w_s, (0, 2, 1, 3)).reshape(DEPTH, CHUNK, A_HEADS * CHUNK)
    wa_sample = jnp.repeat(jnp.transpose(w_s[:, :, :DEC_SEQ, :DEC_SEQ], (0, 3, 2, 1)),
                           A_WIDTH // A_HEADS, axis=-1)
    ba = jnp.repeat(jnp.transpose(b_s, (0, 2, 1)), A_WIDTH // A_HEADS, axis=-1)
    scb = ssm_conv_b.reshape(DEPTH, 1, SSM_CONV_DIM)
    pad_heads = lambda p: jnp.pad(p, ((0, 0), (0, LANES - SSM_HEADS))).reshape(DEPTH, 1, LANES)
    dtb = pad_heads(dt_bias)
    alog = pad_heads(a_log)
    dsk = jnp.repeat(d_skip, SSM_HEAD_DIM, axis=-1).reshape(DEPTH, 1, C_WIDTH)
    snorm = ssm_norm.reshape(DEPTH, 1, C_WIDTH)
    over_time = lambda p: jnp.broadcast_to(p[:, :, None], (DEPTH, SSM_HEADS, CHUNK))
    mid_prompt = (wa_prompt, ba, conv_w, ssm_conv_w, scb, over_time(dt_bias), over_time(a_log), dsk, snorm)
    mid_sample = (wa_sample, ba, conv_w, ssm_conv_w, scb, dtb, alog, dsk, snorm)

    xp = x_prompt.reshape(bp * seq, D_MODEL)
    xs = x_sample.reshape(bs * dseq, D_MODEL)
    ssm_in = state_ssm.reshape(DEPTH, bs, C_WIDTH, D_STATE)
    st_p = st_s = None
    for l in range(DEPTH):
        final = l == DEPTH - 1
        win_b, wout_b, w1_b, w2_b = win_all, wout_all, w1_all, w2_all
        xp, *st_p = _mix_call(True, l, xp, (g1, win_b, wdt_t) + mid_prompt + (wout_b, g2, w1_b, w2_b, gf),
                              carry=st_p, seq=seq, final=final)
        xs, *st_s = _mix_call(False, l, xs, (g1, win_b, wdt_b) + mid_sample + (wout_b,),
                              (state_conv, state_ssm_conv, ssm_in), carry=st_s)
        xs = _ff_call(l, xs, g2, w1_b, w2_b, gf, final)

    y_prompt = xp.reshape(bp, seq, D_MODEL)
    y_sample = xs.reshape(bs, dseq, D_MODEL)
    ssm_shape = lambda b: (DEPTH, b, SSM_HEADS, SSM_HEAD_DIM, D_STATE)
    return (y_prompt, y_sample,
            st_p[0], st_p[1], st_p[2], st_p[3].reshape(ssm_shape(bp)),
            st_s[0], st_s[1], st_s[2], st_s[3].reshape(ssm_shape(bs)))
```
